```python
import math
import jax, jax.numpy as jnp
from jax import lax
import numpy as np

D_MODEL = 2048
BATCH = 4
SEQ = 4096
DEPTH = 4

N_MIXERS = 4
ALPHA = (2.0 * DEPTH) ** 0.25
BETA = (8.0 * DEPTH) ** -0.25
LN_EPS = 1e-5

CONV_WIDTH = 31
S5_GROUP = 16
S5_GROUPS = D_MODEL // S5_GROUP
S5_STATE = 64
S5_DT_MIN = 1e-3
S5_DT_MAX = 1e-1
ML_HEADS = 8
ML_DV = D_MODEL // ML_HEADS
ML_DK = ML_DV // 2
ML_CHUNK = 64
ML_CONV = 4
GM_CHUNK = 128
GM_GROUPS = 8
GM_GROUP_DIM = D_MODEL // GM_GROUPS
D_FF = 5632
N_EXPERTS = 8
TOP_K = 2
D_FF_EXPERT = 5632
MOE_BLOCK = 256

N_A = (DEPTH + 3) // 4
N_B = (DEPTH + 2) // 4
N_C = (DEPTH + 1) // 4
N_D = DEPTH // 4
N_DENSE = (DEPTH + 1) // 2
N_MOE = DEPTH // 2

kernel_name = "hybrid_conv_s5_mlstm_gmlp_moe_deepnorm_adaln"


def layer_norm(x, g, b):
    xf = x.astype(jnp.float32)
    mu = jnp.mean(xf, -1, keepdims=True)
    var = jnp.mean(jnp.square(xf - mu), -1, keepdims=True)
    y = (xf - mu) * lax.rsqrt(var + LN_EPS) * g.astype(jnp.float32) + b.astype(jnp.float32)
    return y.astype(x.dtype)


def causal_depthwise_conv(x, w, b):
    k = w.shape[0]
    y = lax.conv_general_dilated(x, w[:, None, :].astype(x.dtype), window_strides=(1,),
                                 padding=[(k - 1, 0)], dimension_numbers=('NWC', 'WIO', 'NWC'),
                                 feature_group_count=x.shape[-1])
    return y + b


def conformer_conv(h, w_in, b_in, w_dw, b_dw, ln_g, ln_b, w_out, b_out):
    a, g = jnp.split(h @ w_in + b_in, 2, axis=-1)
    u = causal_depthwise_conv(a * jax.nn.sigmoid(g), w_dw, b_dw)
    u = jax.nn.silu(layer_norm(u, ln_g, ln_b))
    return u @ w_out + b_out


def _complex_affine_combine(e1, e2):
    a1r, a1i, b1r, b1i = e1
    a2r, a2i, b2r, b2i = e2
    ar = a2r * a1r - a2i * a1i
    ai = a2r * a1i + a2i * a1r
    br = a2r * b1r - a2i * b1i + b2r
    bi = a2r * b1i + a2i * b1r + b2i
    return ar, ai, br, bi


def s5_mixer(h, w_in, b_in, a_re, a_im, log_dt, b_re, b_im, c_re, c_im, d_skip, w_glu, b_glu):
    bsz, s, dm = h.shape
    G, P, N = S5_GROUPS, S5_GROUP, S5_STATE
    f32 = jnp.float32
    u = (h @ w_in + b_in).astype(f32).reshape(bsz, s, G, P)
    ar, ai = a_re.astype(f32), a_im.astype(f32)
    dt = jnp.exp(log_dt.astype(f32))[:, None]
    decay = jnp.exp(ar * dt)
    lr, li = decay * jnp.cos(ai * dt), decay * jnp.sin(ai * dt)
    den = ar * ar + ai * ai
    zr = ((lr - 1.0) * ar + li * ai) / den
    zi = (li * ar - (lr - 1.0) * ai) / den
    br, bi = b_re.astype(f32), b_im.astype(f32)
    bbr = zr[..., None] * br - zi[..., None] * bi
    bbi = zr[..., None] * bi + zi[..., None] * br
    xr = jnp.einsum('bsgp,gnp->bsgn', u, bbr)
    xi = jnp.einsum('bsgp,gnp->bsgn', u, bbi)
    lam_r = jnp.broadcast_to(lr, (1, s, G, N))
    lam_i = jnp.broadcast_to(li, (1, s, G, N))
    _, _, sr, si = lax.associative_scan(_complex_affine_combine, (lam_r, lam_i, xr, xi), axis=1)
    y = (jnp.einsum('bsgn,gpn->bsgp', sr, c_re.astype(f32))
         - jnp.einsum('bsgn,gpn->bsgp', si, c_im.astype(f32))
         + d_skip.astype(f32).reshape(G, P) * u)
    y = jax.nn.gelu(y.reshape(bsz, s, dm)).astype(h.dtype)
    a, g = jnp.split(y @ w_glu + b_glu, 2, axis=-1)
    return a * jax.nn.sigmoid(g)


def mlstm_mixer(h, w_in, b_in, w_conv, b_conv, mh_g, w_out, b_out):
    bsz, s, dm = h.shape
    H, DK, DV, L = ML_HEADS, ML_DK, ML_DV, ML_CHUNK
    nc = s // L
    f32 = jnp.float32
    z = h @ w_in + b_in
    split_at = np.cumsum([2 * H * DK, H * DV, H * DV, H]).tolist()
    qk, v, o, gi, gf = jnp.split(z, split_at, axis=-1)
    qk = jax.nn.silu(causal_depthwise_conv(qk, w_conv, b_conv))
    q, k = jnp.split(qk.astype(f32), 2, axis=-1)

    def to_chunks(t, d):
        return t.reshape(bsz, nc, L, H, d).transpose(1, 0, 3, 2, 4)

    def gate_chunks(t):
        return t.reshape(bsz, nc, L, H).transpose(1, 0, 3, 2)

    qc = to_chunks(q, DK)
    kc = to_chunks(k * (DK ** -0.5), DK)
    vc = to_chunks(v.astype(f32), DV)
    lic = gate_chunks(gi.astype(f32))
    lfc = gate_chunks(jax.nn.log_sigmoid(gf.astype(f32)))
    causal = jnp.tril(jnp.ones((L, L), bool))

    def step(carry, xs):
        C, n, m = carry
        qb, kb, vb, li_, lf_ = xs
        b = jnp.cumsum(lf_, axis=-1)
        dmat = jnp.where(causal, b[..., :, None] - b[..., None, :] + li_[..., None, :], -jnp.inf)
        inter = b + m[..., None]
        m_j = jnp.maximum(inter, jnp.max(dmat, axis=-1))
        w_intra = jnp.exp(dmat - m_j[..., None])
        w_inter = jnp.exp(inter - m_j)
        sc = jnp.einsum('bhjd,bhsd->bhjs', qb, kb) * w_intra
        num = w_inter[..., None] * jnp.einsum('bhjd,bhde->bhje', qb, C) + jnp.einsum('bhjs,bhse->bhje', sc, vb)
        den = w_inter * jnp.einsum('bhjd,bhd->bhj', qb, n) + jnp.sum(sc, axis=-1)
        hb = num / jnp.maximum(jnp.abs(den), jnp.exp(-m_j))[..., None]
        bl = b[..., -1]
        gsum = bl[..., None] - b + li_
        m_new = jnp.maximum(bl + m, jnp.max(gsum, axis=-1))
        wc = jnp.exp(bl + m - m_new)
        wk = jnp.exp(gsum - m_new[..., None])
        C_new = wc[..., None, None] * C + jnp.einsum('bhs,bhsd,bhse->bhde', wk, kb, vb)
        n_new = wc[..., None] * n + jnp.einsum('bhs,bhsd->bhd', wk, kb)
        return (C_new, n_new, m_new), hb

    init = (jnp.zeros((bsz, H, DK, DV), f32), jnp.zeros((bsz, H, DK), f32), jnp.zeros((bsz, H), f32))
    _, hs = lax.scan(step, init, (qc, kc, vc, lic, lfc))
    hs = hs.transpose(1, 0, 3, 2, 4).reshape(bsz, s, H, DV)
    hs = jax.nn.sigmoid(o.astype(f32)).reshape(bsz, s, H, DV) * hs
    mu = jnp.mean(hs, -1, keepdims=True)
    var = jnp.mean(jnp.square(hs - mu), -1, keepdims=True)
    hs = (hs - mu) * lax.rsqrt(var + LN_EPS) * mh_g.astype(f32).reshape(H, DV)
    return hs.reshape(bsz, s, dm).astype(h.dtype) @ w_out + b_out


def gmlp_mixer(h, w_in, b_in, ln_g, ln_b, w_sp, b_sp, w_out, b_out):
    bsz, s, dm = h.shape
    L, G, Dg = GM_CHUNK, GM_GROUPS, GM_GROUP_DIM
    zz = jax.nn.gelu(h @ w_in + b_in)
    u, v = jnp.split(zz, 2, axis=-1)
    v = layer_norm(v, ln_g, ln_b).reshape(bsz, s // L, L, G, Dg)
    w = w_sp * jnp.tril(jnp.ones((L, L), w_sp.dtype))
    sv = jnp.einsum('gts,bcsgd->bctgd', w, v) + b_sp.T[:, :, None]
    return (u * sv.reshape(bsz, s, dm)) @ w_out + b_out


def swiglu(h, w1, w3, w2):
    return (jax.nn.silu(h @ w1) * (h @ w3)) @ w2


def moe_swiglu(h, w_router, w1, w3, w2):
    bsz, s, dm = h.shape
    xt = h.reshape(-1, dm)
    n_tok = xt.shape[0]
    logits = (xt @ w_router).astype(jnp.float32)
    top_v, top_e = lax.top_k(logits, TOP_K)
    gates = jax.nn.softmax(top_v, axis=-1)
    e_flat = top_e.reshape(-1).astype(jnp.int32)
    tok_flat = jnp.repeat(jnp.arange(n_tok, dtype=jnp.int32), TOP_K)
    g_flat = gates.reshape(-1)
    order = jnp.argsort(e_flat)
    e_s, tok_s, g_s = e_flat[order], tok_flat[order], g_flat[order]
    counts = jnp.bincount(e_flat, length=N_EXPERTS).astype(jnp.int32)
    start = jnp.cumsum(counts) - counts
    padded = (counts + MOE_BLOCK - 1) // MOE_BLOCK * MOE_BLOCK
    pend = jnp.cumsum(padded)
    pstart = pend - padded
    dest = pstart[e_s] + (jnp.arange(n_tok * TOP_K, dtype=jnp.int32) - start[e_s])
    n_rows = n_tok * TOP_K + N_EXPERTS * MOE_BLOCK
    n_blocks = n_rows // MOE_BLOCK
    row_tok = jnp.full((n_rows,), n_tok, jnp.int32).at[dest].set(tok_s)
    row_gate = jnp.zeros((n_rows,), jnp.float32).at[dest].set(g_s)
    block_e = jnp.minimum(jnp.searchsorted(pend, jnp.arange(n_blocks, dtype=jnp.int32) * MOE_BLOCK,
                                           side='right'), N_EXPERTS - 1)
    x_pad = jnp.concatenate([xt, jnp.zeros((1, dm), xt.dtype)], axis=0)
    xb = x_pad[row_tok].reshape(n_blocks, MOE_BLOCK, dm)

    def expert_block(args):
        xblk, e = args
        return (jax.nn.silu(xblk @ w1[e]) * (xblk @ w3[e])) @ w2[e]

    yb = lax.map(expert_block, (xb, block_e))
    y_rows = yb.reshape(n_rows, dm) * row_gate[:, None].astype(xt.dtype)
    out = jnp.zeros((n_tok + 1, dm), xt.dtype).at[row_tok].add(y_rows)[:n_tok]
    return out.reshape(bsz, s, dm)


def setup_inputs(seed: int = 0) -> dict:
    key = jax.random.key(seed)
    ks = iter(jax.random.split(key, 96))
    f32 = jnp.float32
    D = D_MODEL

    def nrm(shape, scale):
        return jax.random.normal(next(ks), shape, f32) * scale

    x = nrm((BATCH, SEQ, D), 1.0)
    c = nrm((BATCH, D), 1.0)
    ada_w = nrm((DEPTH, D, 6 * D), 0.5 * D ** -0.5)
    ada_b = nrm((DEPTH, 6 * D), 0.02)
    ln1_g = 1.0 + nrm((DEPTH, D), 0.02)
    ln1_b = nrm((DEPTH, D), 0.02)
    ln2_g = 1.0 + nrm((DEPTH, D), 0.02)
    ln2_b = nrm((DEPTH, D), 0.02)
    a_w_in = nrm((N_A, D, 2 * D), D ** -0.5)
    a_b_in = nrm((N_A, 2 * D), 0.02)
    a_w_dw = nrm((N_A, CONV_WIDTH, D), CONV_WIDTH ** -0.5)
    a_b_dw = nrm((N_A, D), 0.02)
    a_ln_g = 1.0 + nrm((N_A, D), 0.02)
    a_ln_b = nrm((N_A, D), 0.02)
    a_w_out = nrm((N_A, D, D), BETA * D ** -0.5)
    a_b_out = nrm((N_A, D), 0.02)
    n_idx = jnp.arange(S5_STATE, dtype=f32)
    b_w_in = nrm((N_B, D, D), D ** -0.5)
    b_b_in = nrm((N_B, D), 0.02)
    b_a_re = -0.5 + nrm((N_B, S5_GROUPS, S5_STATE), 0.01)
    b_a_im = math.pi * n_idx + nrm((N_B, S5_GROUPS, S5_STATE), 0.01)
    b_log_dt = jax.random.uniform(next(ks), (N_B, S5_GROUPS), f32,
                                  minval=math.log(S5_DT_MIN), maxval=math.log(S5_DT_MAX))
    b_b_re = nrm((N_B, S5_GROUPS, S5_STATE, S5_GROUP), (2 * S5_GROUP) ** -0.5)
    b_b_im = nrm((N_B, S5_GROUPS, S5_STATE, S5_GROUP), (2 * S5_GROUP) ** -0.5)
    b_c_re = nrm((N_B, S5_GROUPS, S5_GROUP, S5_STATE), 0.5)
    b_c_im = nrm((N_B, S5_GROUPS, S5_GROUP, S5_STATE), 0.5)
    b_d = nrm((N_B, D), 1.0)
    b_w_glu = nrm((N_B, D, 2 * D), BETA * D ** -0.5)
    b_b_glu = nrm((N_B, 2 * D), 0.02)
    cw = 2 * ML_HEADS * ML_DK + 2 * ML_HEADS * ML_DV + 2 * ML_HEADS
    c_w_in = nrm((N_C, D, cw), D ** -0.5)
    c_b_in = jnp.concatenate([
        nrm((N_C, 2 * ML_HEADS * ML_DK + 2 * ML_HEADS * ML_DV), 0.02),
        nrm((N_C, ML_HEADS), 0.1),
        jnp.linspace(3.0, 6.0, ML_HEADS, dtype=f32)[None] + nrm((N_C, ML_HEADS), 0.1),
    ], axis=-1)
    c_w_conv = nrm((N_C, ML_CONV, 2 * ML_HEADS * ML_DK), ML_CONV ** -0.5)
    c_b_conv = nrm((N_C, 2 * ML_HEADS * ML_DK), 0.02)
    c_mh_g = 1.0 + nrm((N_C, D), 0.02)
    c_w_out = nrm((N_C, D, D), BETA * D ** -0.5)
    c_b_out = nrm((N_C, D), 0.02)
    d_w_in = nrm((N_D, D, 2 * D), D ** -0.5)
    d_b_in = nrm((N_D, 2 * D), 0.02)
    d_ln_g = 1.0 + nrm((N_D, D), 0.02)
    d_ln_b = nrm((N_D, D), 0.02)
    d_w_sp = nrm((N_D, GM_GROUPS, GM_CHUNK, GM_CHUNK), GM_CHUNK ** -0.5)
    d_b_sp = 1.0 + nrm((N_D, GM_GROUPS, GM_CHUNK), 0.02)
    d_w_out = nrm((N_D, D, D), BETA * D ** -0.5)
    d_b_out = nrm((N_D, D), 0.02)
    f_w1 = nrm((N_DENSE, D, D_FF), D ** -0.5)
    f_w3 = nrm((N_DENSE, D, D_FF), D ** -0.5)
    f_w2 = nrm((N_DENSE, D_FF, D), BETA * D_FF ** -0.5)
    m_router = nrm((N_MOE, D, N_EXPERTS), D ** -0.5)
    m_w1 = nrm((N_MOE, N_EXPERTS, D, D_FF_EXPERT), D ** -0.5)
    m_w3 = nrm((N_MOE, N_EXPERTS, D, D_FF_EXPERT), D ** -0.5)
    m_w2 = nrm((N_MOE, N_EXPERTS, D_FF_EXPERT, D), BETA * D_FF_EXPERT ** -0.5)
    return {
        "x": x, "c": c, "ada_w": ada_w, "ada_b": ada_b,
        "ln1_g": ln1_g, "ln1_b": ln1_b, "ln2_g": ln2_g, "ln2_b": ln2_b,
        "a_w_in": a_w_in, "a_b_in": a_b_in, "a_w_dw": a_w_dw, "a_b_dw": a_b_dw,
        "a_ln_g": a_ln_g, "a_ln_b": a_ln_b, "a_w_out": a_w_out, "a_b_out": a_b_out,
        "b_w_in": b_w_in, "b_b_in": b_b_in, "b_a_re": b_a_re, "b_a_im": b_a_im,
        "b_log_dt": b_log_dt, "b_b_re": b_b_re, "b_b_im": b_b_im, "b_c_re": b_c_re,
        "b_c_im": b_c_im, "b_d": b_d, "b_w_glu": b_w_glu, "b_b_glu": b_b_glu,
        "c_w_in": c_w_in, "c_b_in": c_b_in, "c_w_conv": c_w_conv, "c_b_conv": c_b_conv,
        "c_mh_g": c_mh_g, "c_w_out": c_w_out, "c_b_out": c_b_out,
        "d_w_in": d_w_in, "d_b_in": d_b_in, "d_ln_g": d_ln_g, "d_ln_b": d_ln_b,
        "d_w_sp": d_w_sp, "d_b_sp": d_b_sp, "d_w_out": d_w_out, "d_b_out": d_b_out,
        "f_w1": f_w1, "f_w3": f_w3, "f_w2": f_w2,
        "m_router": m_router, "m_w1": m_w1, "m_w3": m_w3, "m_w2": m_w2,
    }


def reference(x, c, ada_w, ada_b, ln1_g, ln1_b, ln2_g, ln2_b,
              a_w_in, a_b_in, a_w_dw, a_b_dw, a_ln_g, a_ln_b, a_w_out, a_b_out,
              b_w_in, b_b_in, b_a_re, b_a_im, b_log_dt, b_b_re, b_b_im, b_c_re,
              b_c_im, b_d, b_w_glu, b_b_glu,
              c_w_in, c_b_in, c_w_conv, c_b_conv, c_mh_g, c_w_out, c_b_out,
              d_w_in, d_b_in, d_ln_g, d_ln_b, d_w_sp, d_b_sp, d_w_out, d_b_out,
              f_w1, f_w3, f_w2,
              m_router, m_w1, m_w3, m_w2):
    cond = jax.nn.silu(c)
    ia = ib = ic = idd = 0
    i_dense = i_moe = 0
    for layer in range(DEPTH):
        mod = (cond @ ada_w[layer] + ada_b[layer])[:, None, :]
        sh1, sc1, g1, sh2, sc2, g2 = jnp.split(mod, 6, axis=-1)
        hm = x * (1.0 + sc1) + sh1
        kind = layer % N_MIXERS
        if kind == 0:
            y = conformer_conv(hm, a_w_in[ia], a_b_in[ia], a_w_dw[ia], a_b_dw[ia],
                               a_ln_g[ia], a_ln_b[ia], a_w_out[ia], a_b_out[ia])
            ia += 1
        elif kind == 1:
            y = s5_mixer(hm, b_w_in[ib], b_b_in[ib], b_a_re[ib], b_a_im[ib], b_log_dt[ib],
                         b_b_re[ib], b_b_im[ib], b_c_re[ib], b_c_im[ib], b_d[ib],
                         b_w_glu[ib], b_b_glu[ib])
            ib += 1
        elif kind == 2:
            y = mlstm_mixer(hm, c_w_in[ic], c_b_in[ic], c_w_conv[ic], c_b_conv[ic],
                            c_mh_g[ic], c_w_out[ic], c_b_out[ic])
            ic += 1
        else:
            y = gmlp_mixer(hm, d_w_in[idd], d_b_in[idd], d_ln_g[idd], d_ln_b[idd],
                           d_w_sp[idd], d_b_sp[idd], d_w_out[idd], d_b_out[idd])
            idd += 1
        x = layer_norm(ALPHA * x + g1 * y, ln1_g[layer], ln1_b[layer])
        hf = x * (1.0 + sc2) + sh2
        if layer % 2 == 0:
            y = swiglu(hf, f_w1[i_dense], f_w3[i_dense], f_w2[i_dense])
            i_dense += 1
        else:
            y = moe_swiglu(hf, m_router[i_moe], m_w1[i_moe], m_w3[i_moe], m_w2[i_moe])
            i_moe += 1
        x = layer_norm(ALPHA * x + g2 * y, ln2_g[layer], ln2_b[layer])
    return x
```

```python
import functools
import math

import jax
import jax.numpy as jnp
from jax import lax
from jax.experimental import pallas as pl
from jax.experimental.pallas import tpu as pltpu

F32 = jnp.float32
BF16 = jnp.bfloat16

LN_EPS = 1e-5
N_MIXERS = 4
CONV_WIDTH = 31
S5_GROUP = 16
S5_STATE = 64
ML_HEADS = 8
ML_CONV = 4
GM_CHUNK = 128
GM_GROUPS = 8
N_EXPERTS = 8
TOP_K = 2

V7X_VMEM_BYTES = 64 * 1024 * 1024
V7X_LANES = 128
V7X_SUBLANES = 8
VMEM_LIMIT = V7X_VMEM_BYTES - 8 * 1024 * 1024

S5_CHUNK = 16
ML_CHUNK_ROWS = 256
CONV_HALO = 32


def _cparams(n_axes):
    return pltpu.CompilerParams(dimension_semantics=("arbitrary",) * n_axes,
                                vmem_limit_bytes=VMEM_LIMIT)


def _pick(n, pref):
    t = min(n, pref)
    while n % t:
        t //= 2
    return t


def _sigmoid(x):
    return 1.0 / (1.0 + jnp.exp(-x))


def _silu(x):
    return x * _sigmoid(x)


def _layer_norm_rows(v, gamma, beta):
    mu = jnp.mean(v, axis=-1, keepdims=True)
    d = v - mu
    var = jnp.mean(d * d, axis=-1, keepdims=True)
    y = d * lax.rsqrt(var + LN_EPS)
    if gamma is not None:
        y = y * gamma
    if beta is not None:
        y = y + beta
    return y


def _bdot(a, b):
    return jnp.dot(a, b, preferred_element_type=F32)


def _ada_kernel(c_ref, w_ref, b_ref, o_ref):
    cond = _silu(c_ref[...]).astype(BF16)
    o_ref[0] = _bdot(cond, w_ref[0].astype(BF16)) + b_ref[0]


def _ada_mods(c, ada_w, ada_b):
    depth, d, d6 = ada_w.shape
    bsz = c.shape[0]
    tn = _pick(d6, 1024)
    return pl.pallas_call(
        _ada_kernel,
        grid=(depth, d6 // tn),
        in_specs=[pl.BlockSpec((bsz, d), lambda l, j: (0, 0)),
                  pl.BlockSpec((1, d, tn), lambda l, j: (l, 0, j)),
                  pl.BlockSpec((1, 1, tn), lambda l, j: (l, 0, j))],
        out_specs=pl.BlockSpec((1, bsz, tn), lambda l, j: (l, 0, j)),
        out_shape=jax.ShapeDtypeStruct((depth, bsz, d6), F32),
        compiler_params=_cparams(2),
        name="ada_mods",
    )(c, ada_w, ada_b.reshape(depth, 1, d6))


def _mm_kernel(*refs, n_w, has_mod, use_scratch, act):
    it = iter(refs)
    x_ref = next(it)
    sc_ref = sh_ref = None
    if has_mod:
        sc_ref, sh_ref = next(it), next(it)
    w_refs = [next(it) for _ in range(n_w)]
    b_refs = [next(it) for _ in range(n_w)]
    o_ref = next(it)
    if use_scratch:
        xb_ref = next(it)

        @pl.when(pl.program_id(1) == 0)
        def _():
            xv = x_ref[...].astype(F32)
            if has_mod:
                xv = xv * (1.0 + sc_ref[0]) + sh_ref[0]
            xb_ref[...] = xv.astype(BF16)

        xb = xb_ref[...]
    else:
        xb = x_ref[...]
    z = [_bdot(xb, w_refs[k][...].astype(BF16)) + b_refs[k][...] for k in range(n_w)]
    if act == "glu":
        out = z[0] * _sigmoid(z[1])
    elif act == "gelu":
        out = jax.nn.gelu(z[0])
    else:
        out = z[0]
    o_ref[...] = out.astype(o_ref.dtype)


def _matmul(x, w, b, *, rows_per_batch, mod=None, act=None, out_dtype=F32, tm=1024, tn=512):
    n, k = x.shape
    nw = w.shape[1]
    n_out = nw // 2 if act == "glu" else nw
    tm = _pick(rows_per_batch, tm)
    tn = _pick(n_out, tn)
    tiles_per_batch = rows_per_batch // tm
    has_mod = mod is not None
    use_scratch = has_mod or x.dtype != BF16
    n_w = 2 if act == "glu" else 1
    half = n_out // tn

    in_specs = [pl.BlockSpec((tm, k), lambda i, j: (i, 0))]
    args = [x]
    if has_mod:
        mspec = pl.BlockSpec((1, 1, k), lambda i, j: (i // tiles_per_batch, 0, 0))
        in_specs += [mspec, mspec]
        args += [mod[0], mod[1]]
    b2 = b.reshape(1, nw)
    in_specs.append(pl.BlockSpec((k, tn), lambda i, j: (0, j)))
    args.append(w)
    if n_w == 2:
        in_specs.append(pl.BlockSpec((k, tn), lambda i, j: (0, j + half)))
        args.append(w)
    in_specs.append(pl.BlockSpec((1, tn), lambda i, j: (0, j)))
    args.append(b2)
    if n_w == 2:
        in_specs.append(pl.BlockSpec((1, tn), lambda i, j: (0, j + half)))
        args.append(b2)
    scratch = [pltpu.VMEM((tm, k), BF16)] if use_scratch else []
    return pl.pallas_call(
        functools.partial(_mm_kernel, n_w=n_w, has_mod=has_mod, use_scratch=use_scratch, act=act),
        grid=(n // tm, n_out // tn),
        in_specs=in_specs,
        out_specs=pl.BlockSpec((tm, tn), lambda i, j: (i, j)),
        out_shape=jax.ShapeDtypeStruct((n, n_out), out_dtype),
        scratch_shapes=scratch,
        compiler_params=_cparams(2),
        name="matmul_" + (act or "bias"),
    )(*args)


def _ln_res_kernel(*refs, alpha, emit_h):
    if emit_h:
        x_ref, y_ref, g_ref, gam_ref, bet_ref, sc_ref, sh_ref, o_ref, h_ref = refs
    else:
        x_ref, y_ref, g_ref, gam_ref, bet_ref, o_ref = refs
    r = alpha * x_ref[...] + g_ref[0] * y_ref[...].astype(F32)
    xn = _layer_norm_rows(r, gam_ref[...], bet_ref[...])
    o_ref[...] = xn
    if emit_h:
        h_ref[...] = (xn * (1.0 + sc_ref[0]) + sh_ref[0]).astype(h_ref.dtype)


def _ln_res(x, y, g, gamma, beta, *, alpha, rows_per_batch, mod=None, h_dtype=BF16, tm=512):
    n, d = x.shape
    tm = _pick(rows_per_batch, tm)
    tpb = rows_per_batch // tm
    row = pl.BlockSpec((tm, d), lambda i: (i, 0))
    per_b = pl.BlockSpec((1, 1, d), lambda i: (i // tpb, 0, 0))
    vec = pl.BlockSpec((1, d), lambda i: (0, 0))
    emit_h = mod is not None
    in_specs = [row, row, per_b, vec, vec]
    args = [x, y, g, gamma.reshape(1, d), beta.reshape(1, d)]
    out_shape = jax.ShapeDtypeStruct((n, d), F32)
    out_specs = row
    if emit_h:
        in_specs += [per_b, per_b]
        args += [mod[0], mod[1]]
        out_shape = (out_shape, jax.ShapeDtypeStruct((n, d), h_dtype))
        out_specs = (row, row)
    return pl.pallas_call(
        functools.partial(_ln_res_kernel, alpha=alpha, emit_h=emit_h),
        grid=(n // tm,),
        in_specs=in_specs,
        out_specs=out_specs,
        out_shape=out_shape,
        compiler_params=_cparams(1),
        name="ln_res",
    )(*args)


def _swiglu_step(xb, w1_ref, w3_ref, w2_ref, o_ref, f):
    a = _bdot(xb, w1_ref[...].astype(BF16))
    b = _bdot(xb, w3_ref[...].astype(BF16))
    h = (_silu(a) * b).astype(BF16)
    p = _bdot(h, w2_ref[...].astype(BF16))

    @pl.when(f == 0)
    def _():
        o_ref[...] = p

    @pl.when(f > 0)
    def _():
        o_ref[...] += p


def _ffn_kernel(x_ref, w1_ref, w3_ref, w2_ref, o_ref):
    _swiglu_step(x_ref[...], w1_ref, w3_ref, w2_ref, o_ref, pl.program_id(1))


def _ffn(hb, w1, w3, w2, *, tm=1024, tf=256):
    n, d = hb.shape
    ff = w1.shape[1]
    tm = _pick(n, tm)
    tf = _pick(ff, tf)
    return pl.pallas_call(
        _ffn_kernel,
        grid=(n // tm, ff // tf),
        in_specs=[pl.BlockSpec((tm, d), lambda i, f: (i, 0)),
                  pl.BlockSpec((d, tf), lambda i, f: (0, f)),
                  pl.BlockSpec((d, tf), lambda i, f: (0, f)),
                  pl.BlockSpec((tf, d), lambda i, f: (f, 0))],
        out_specs=pl.BlockSpec((tm, d), lambda i, f: (i, 0)),
        out_shape=jax.ShapeDtypeStruct((n, d), F32),
        compiler_params=_cparams(2),
        name="ffn_dense",
    )(hb, w1, w3, w2)


def _conv_kernel(x_ref, w_ref, b_ref, g_ref, be_ref, o_ref, xx_ref, u_ref, *, ts, kw, rc, cc):
    i = pl.program_id(1)
    d = x_ref.shape[1]
    halo = CONV_HALO

    @pl.when(i == 0)
    def _():
        xx_ref[0:halo, :] = jnp.zeros((halo, d), F32)

    @pl.when(i > 0)
    def _():
        xx_ref[0:halo, :] = xx_ref[ts:ts + halo, :]

    xx_ref[halo:halo + ts, :] = x_ref[...]
    off = halo - (kw - 1)

    for r0 in range(0, ts, rc):
        for c0 in range(0, d, cc):
            acc = jnp.broadcast_to(b_ref[:, c0:c0 + cc], (rc, cc))
            for j in range(kw):
                acc = acc + w_ref[j:j + 1, c0:c0 + cc] * xx_ref[r0 + off + j:r0 + off + j + rc, c0:c0 + cc]
            u_ref[r0:r0 + rc, c0:c0 + cc] = acc
    y = _layer_norm_rows(u_ref[...], g_ref[...], be_ref[...])
    o_ref[...] = _silu(y).astype(o_ref.dtype)


def _conv_ln_swish(glu, w_dw, b_dw, ln_g, ln_b, *, bsz, seq, ts=128):
    n, d = glu.shape
    kw = w_dw.shape[0]
    ts = _pick(seq, ts)
    nt = seq // ts
    rc = _pick(ts, 32)
    cc = _pick(d, 512)
    row = pl.BlockSpec((ts, d), lambda b, i: (b * nt + i, 0))
    vec = pl.BlockSpec((1, d), lambda b, i: (0, 0))
    return pl.pallas_call(
        functools.partial(_conv_kernel, ts=ts, kw=kw, rc=rc, cc=cc),
        grid=(bsz, nt),
        in_specs=[row, pl.BlockSpec((kw, d), lambda b, i: (0, 0)), vec, vec, vec],
        out_specs=row,
        out_shape=jax.ShapeDtypeStruct((n, d), BF16),
        scratch_shapes=[pltpu.VMEM((CONV_HALO + ts, d), F32), pltpu.VMEM((ts, d), F32)],
        compiler_params=_cparams(2),
        name="conv_ln_swish",
    )(glu, w_dw, b_dw.reshape(1, d), ln_g.reshape(1, d), ln_b.reshape(1, d))


def _s5_kernel(u_ref, t_ref, mr_ref, mi_ref, cr_ref, ci_ref, lr_ref, li_ref, d_ref, o_ref,
               vr_ref, vi_ref, pr_ref, pi_ref, *, nb, nc, half):
    u = u_ref[0]
    ub = u.astype(BF16)
    vr_ref[...] = _bdot(ub, mr_ref[0])
    vi_ref[...] = _bdot(ub, mi_ref[0])
    lr = lr_ref[0]
    li = li_ref[0]

    slab = V7X_SUBLANES
    spc = slab // nb
    n2 = vr_ref.shape[1]
    srow = lax.broadcasted_iota(jnp.int32, (slab, n2), 0)

    def step(k, carry):
        sr, si = carry
        r0 = pl.multiple_of(k * slab, slab)
        xr = vr_ref[pl.ds(r0, slab), :]
        xi = vi_ref[pl.ds(r0, slab), :]
        prev_r = sr
        prev_i = si
        for j in range(spc):
            if j > 0:
                keep = srow < j * nb
                prev_r = jnp.where(keep, prev_r, pltpu.roll(sr, j * nb, 0))
                prev_i = jnp.where(keep, prev_i, pltpu.roll(si, j * nb, 0))
                xrj = pltpu.roll(xr, slab - j * nb, 0)
                xij = pltpu.roll(xi, slab - j * nb, 0)
            else:
                xrj, xij = xr, xi
            sr, si = lr * sr - li * si + xrj, lr * si + li * sr + xij
        pr_ref[pl.ds(r0, slab), :] = prev_r
        pi_ref[pl.ds(r0, slab), :] = prev_i
        return sr, si

    zero = jnp.zeros((slab, n2), F32)
    lax.fori_loop(0, nc // spc, step, (zero, zero))
    y_inter = (_bdot(pr_ref[...].astype(BF16), cr_ref[0]) + _bdot(pi_ref[...].astype(BF16), ci_ref[0]))
    y_intra = jnp.concatenate([_bdot(ub[:, :half], t_ref[0, 0]), _bdot(ub[:, half:], t_ref[0, 1])], axis=1)
    y = y_intra + y_inter + d_ref[0] * u
    o_ref[0] = jax.nn.gelu(y).astype(o_ref.dtype)


def _s5_tables(a_re, a_im, log_dt, b_re, b_im, c_re, c_im, d_skip):
    g, n = a_re.shape
    p = b_re.shape[2]
    lc = S5_CHUNK
    ar, ai = a_re.astype(F32), a_im.astype(F32)
    dt = jnp.exp(log_dt.astype(F32))[:, None]
    decay = jnp.exp(ar * dt)
    lr, li = decay * jnp.cos(ai * dt), decay * jnp.sin(ai * dt)
    den = ar * ar + ai * ai
    zr = ((lr - 1.0) * ar + li * ai) / den
    zi = (li * ar - (lr - 1.0) * ai) / den
    br, bi = b_re.astype(F32), b_im.astype(F32)
    bbr = zr[..., None] * br - zi[..., None] * bi
    bbi = zr[..., None] * bi + zi[..., None] * br
    tau = jnp.arange(lc + 1, dtype=F32)[:, None, None]
    pdec = jnp.exp(tau * (ar * dt)[None])
    pr, pi = pdec * jnp.cos(tau * (ai * dt)[None]), pdec * jnp.sin(tau * (ai * dt)[None])
    cr, ci = c_re.astype(F32), c_im.astype(F32)
    hp = lax.Precision.HIGHEST
    lbr = pr[..., None] * bbr[None] - pi[..., None] * bbi[None]
    lbi = pr[..., None] * bbi[None] + pi[..., None] * bbr[None]
    ktau = (jnp.einsum('gpn,tgnq->tgpq', cr, lbr[:lc], precision=hp)
            - jnp.einsum('gpn,tgnq->tgpq', ci, lbi[:lc], precision=hp))
    s_idx = jnp.arange(lc)[:, None]
    t_idx = jnp.arange(lc)[None, :]
    diff = t_idx - s_idx
    kg = ktau[jnp.clip(diff, 0, lc - 1)]
    kg = jnp.where((diff >= 0)[:, :, None, None, None], kg, 0.0)
    toep = kg.transpose(2, 0, 4, 1, 3).reshape(g, lc * p, lc * p)
    mr = lbr[lc - 1 - jnp.arange(lc)].transpose(1, 0, 3, 2).reshape(g, lc * p, n)
    mi = lbi[lc - 1 - jnp.arange(lc)].transpose(1, 0, 3, 2).reshape(g, lc * p, n)
    pr1, pi1 = pr[1:], pi[1:]
    cmr = (cr[None] * pr1[:, :, None, :] - ci[None] * pi1[:, :, None, :])
    cmi = -(cr[None] * pi1[:, :, None, :] + ci[None] * pr1[:, :, None, :])
    cmr = cmr.transpose(1, 3, 0, 2).reshape(g, n, lc * p)
    cmi = cmi.transpose(1, 3, 0, 2).reshape(g, n, lc * p)
    g2 = g // 2
    w = lc * p

    def pair_in(m):
        m = m.reshape(g2, 2, w, n)
        z = jnp.zeros((g2, w, n), F32)
        top = jnp.concatenate([m[:, 0], z], axis=2)
        bot = jnp.concatenate([z, m[:, 1]], axis=2)
        return jnp.concatenate([top, bot], axis=1)

    def pair_out(m):
        m = m.reshape(g2, 2, n, w)
        z = jnp.zeros((g2, n, w), F32)
        top = jnp.concatenate([m[:, 0], z], axis=2)
        bot = jnp.concatenate([z, m[:, 1]], axis=2)
        return jnp.concatenate([top, bot], axis=1)

    lam_r = pr[lc].reshape(g2, 1, 2 * n)
    lam_i = pi[lc].reshape(g2, 1, 2 * n)
    dvec = jnp.broadcast_to(d_skip.astype(F32).reshape(g2, 2, 1, p), (g2, 2, lc, p)).reshape(g2, 1, 2 * w)
    return (toep.reshape(g2, 2, w, w).astype(BF16), pair_in(mr).astype(BF16), pair_in(mi).astype(BF16),
            pair_out(cmr).astype(BF16), pair_out(cmi).astype(BF16), lam_r, lam_i, dvec)


def _s5_core(u, tables, *, bsz, seq):
    n_rows, d = u.shape
    toep, mr, mi, cmr, cmi, lam_r, lam_i, dvec = tables
    g2 = toep.shape[0]
    lc = S5_CHUNK
    p = S5_GROUP
    w2 = 2 * lc * p
    nc = seq // lc
    m = nc * bsz
    n2 = lam_r.shape[2]
    ut = u.reshape(bsz, nc, lc, g2, 2, p).transpose(3, 1, 0, 4, 2, 5).reshape(g2, m, w2)
    blk = lambda *shape: pl.BlockSpec((1,) + shape, lambda i: (i,) + (0,) * len(shape))
    yt = pl.pallas_call(
        functools.partial(_s5_kernel, nb=bsz, nc=nc, half=w2 // 2),
        grid=(g2,),
        in_specs=[blk(m, w2), blk(2, w2 // 2, w2 // 2), blk(w2, n2), blk(w2, n2), blk(n2, w2), blk(n2, w2),
                  blk(1, n2), blk(1, n2), blk(1, w2)],
        out_specs=blk(m, w2),
        out_shape=jax.ShapeDtypeStruct((g2, m, w2), BF16),
        scratch_shapes=[pltpu.VMEM((m, n2), F32)] * 4,
        compiler_params=_cparams(1),
        name="s5_core",
    )(ut, toep, mr, mi, cmr, cmi, lam_r, lam_i, dvec)
    return yt.reshape(g2, nc, bsz, 2, lc, p).transpose(2, 1, 4, 0, 3, 5).reshape(n_rows, d)


def _log_sigmoid(x):
    return jnp.minimum(x, 0.0) - jnp.log(1.0 + jnp.exp(-jnp.abs(x)))


def _mlstm_kernel(q_ref, k_ref, v_ref, o_ref, g_ref, wq_ref, wk_ref, bq_ref, bk_ref, mh_ref, out_ref,
                  c_st, n_st, m_st, qx_ref, kx_ref, *, lm, dk, heads, kw):
    h = pl.program_id(1)
    c = pl.program_id(2)
    pad = V7X_SUBLANES

    @pl.when(c == 0)
    def _():
        c_st[...] = jnp.zeros(c_st.shape, F32)
        n_st[...] = jnp.zeros(n_st.shape, F32)
        m_st[...] = jnp.zeros(m_st.shape, F32)
        qx_ref[0:pad, :] = jnp.zeros((pad, dk), F32)
        kx_ref[0:pad, :] = jnp.zeros((pad, dk), F32)

    def conv_swish(x_ref, xx_ref, w_ref, b_ref):
        xx_ref[pad:pad + lm, :] = x_ref[...]
        acc = jnp.broadcast_to(b_ref[...], (lm, dk))
        for j in range(kw):
            acc = acc + w_ref[j:j + 1, :] * xx_ref[pad - (kw - 1) + j:pad - (kw - 1) + j + lm, :]
        xx_ref[0:pad, :] = xx_ref[lm:lm + pad, :]
        return _silu(acc)

    q = conv_swish(q_ref, qx_ref, wq_ref, bq_ref)
    k = conv_swish(k_ref, kx_ref, wk_ref, bk_ref) * (dk ** -0.5)
    v = v_ref[...]

    gts = g_ref[...]
    nl = gts.shape[1]
    lane = lax.broadcasted_iota(jnp.int32, (lm, nl), 1)
    lf = _log_sigmoid(gts)
    rows = lax.broadcasted_iota(jnp.int32, (lm, lm), 0)
    cols = lax.broadcasted_iota(jnp.int32, (lm, lm), 1)
    causal = rows >= cols
    tri = causal.astype(BF16)
    lf_hi = lf.astype(BF16)
    r1 = lf - lf_hi.astype(F32)
    lf_mid = r1.astype(BF16)
    lf_lo = (r1 - lf_mid.astype(F32)).astype(BF16)
    bcum = _bdot(tri, lf_hi) + _bdot(tri, lf_mid) + _bdot(tri, lf_lo)

    def col_of(mat, idx):
        return jnp.sum(jnp.where(lane == idx, mat, 0.0), axis=1, keepdims=True)

    def row_of(mat_t, idx):
        sub = lax.broadcasted_iota(jnp.int32, mat_t.shape, 0)
        return jnp.sum(jnp.where(sub == idx, mat_t, 0.0), axis=0, keepdims=True)

    b_col = col_of(bcum, heads + h)
    li_col = col_of(gts, h)
    b_row = row_of(bcum.T, heads + h)
    li_row = row_of(gts.T, h)

    m_prev = m_st[...]
    dmat = jnp.where(causal, b_col - b_row + li_row, -jnp.inf)
    inter = b_col + m_prev
    m_j = jnp.maximum(inter, jnp.max(dmat, axis=1, keepdims=True))
    w_intra = jnp.exp(dmat - m_j)
    w_inter = jnp.exp(inter - m_j)
    qb, kb, vb = q.astype(BF16), k.astype(BF16), v.astype(BF16)
    sc = lax.dot_general(qb, kb, (((1,), (1,)), ((), ())), preferred_element_type=F32) * w_intra
    num = w_inter * _bdot(qb, c_st[...].astype(BF16)) + _bdot(sc.astype(BF16), vb)
    den = w_inter * jnp.sum(q * n_st[...], axis=1, keepdims=True) + jnp.sum(sc, axis=1, keepdims=True)
    hb = num / jnp.maximum(jnp.abs(den), jnp.exp(-m_j))

    bl = b_col[lm - 1:lm, :]
    gsum = bl - b_col + li_col
    m_new = jnp.maximum(bl + m_prev, jnp.max(gsum, axis=0, keepdims=True))
    wc = jnp.exp(bl + m_prev - m_new)
    wk = jnp.exp(gsum - m_new)
    kwt = wk * k
    c_st[...] = wc * c_st[...] + _bdot(kwt.T.astype(BF16), vb)
    n_st[...] = wc * n_st[...] + jnp.sum(kwt, axis=0, keepdims=True)
    m_st[...] = m_new

    hs = _sigmoid(o_ref[...]) * hb
    out_ref[...] = (_layer_norm_rows(hs, None, None) * mh_ref[...]).astype(out_ref.dtype)


def _mlstm_core(z, w_conv, b_conv, mh_g, *, bsz, seq, d):
    n_rows = z.shape[0]
    heads = ML_HEADS
    dv = d // heads
    dk = dv // 2
    lm = _pick(seq, ML_CHUNK_ROWS)
    nc = seq // lm
    kw = w_conv.shape[0]
    gate_blk = (2 * heads * dk + 2 * heads * dv) // V7X_LANES
    vblk0 = (2 * heads * dk) // dv
    row = lambda b, h, c: b * nc + c
    in_specs = [
        pl.BlockSpec((lm, dk), lambda b, h, c: (row(b, h, c), h)),
        pl.BlockSpec((lm, dk), lambda b, h, c: (row(b, h, c), heads + h)),
        pl.BlockSpec((lm, dv), lambda b, h, c: (row(b, h, c), vblk0 + h)),
        pl.BlockSpec((lm, dv), lambda b, h, c: (row(b, h, c), vblk0 + heads + h)),
        pl.BlockSpec((lm, V7X_LANES), lambda b, h, c: (row(b, h, c), gate_blk)),
        pl.BlockSpec((kw, dk), lambda b, h, c: (0, h)),
        pl.BlockSpec((kw, dk), lambda b, h, c: (0, heads + h)),
        pl.BlockSpec((1, dk), lambda b, h, c: (0, h)),
        pl.BlockSpec((1, dk), lambda b, h, c: (0, heads + h)),
        pl.BlockSpec((1, dv), lambda b, h, c: (0, h)),
    ]
    b2 = b_conv.reshape(1, -1)
    return pl.pallas_call(
        functools.partial(_mlstm_kernel, lm=lm, dk=dk, heads=heads, kw=kw),
        grid=(bsz, heads, nc),
        in_specs=in_specs,
        out_specs=pl.BlockSpec((lm, dv), lambda b, h, c: (row(b, h, c), h)),
        out_shape=jax.ShapeDtypeStruct((n_rows, d), BF16),
        scratch_shapes=[pltpu.VMEM((dk, dv), F32), pltpu.VMEM((1, dk), F32), pltpu.VMEM((1, 1), F32),
                        pltpu.VMEM((lm + V7X_SUBLANES, dk), F32), pltpu.VMEM((lm + V7X_SUBLANES, dk), F32)],
        compiler_params=_cparams(3),
        name="mlstm_core",
    )(z, z, z, z, z, w_conv, w_conv, b2, b2, mh_g.reshape(1, d))


def _gmlp_kernel(u_ref, v_ref, g_ref, be_ref, w_ref, bsp_ref, o_ref, *, lg, groups, dg, nsub):
    rows = lax.broadcasted_iota(jnp.int32, (lg, lg), 0)
    cols = lax.broadcasted_iota(jnp.int32, (lg, lg), 1)
    keep = rows >= cols
    wm = [jnp.where(keep, w_ref[g], 0.0).astype(BF16) for g in range(groups)]
    for s in range(nsub):
        r = slice(s * lg, (s + 1) * lg)
        vn = _layer_norm_rows(v_ref[r, :], g_ref[...], be_ref[...]).astype(BF16)
        for g in range(groups):
            cs = slice(g * dg, (g + 1) * dg)
            sv = _bdot(wm[g], vn[:, cs]) + bsp_ref[:, g:g + 1]
            o_ref[r, cs] = (u_ref[r, cs] * sv).astype(o_ref.dtype)


def _gmlp_core(zz, ln_g, ln_b, w_sp, b_sp, *, seq, tm=512):
    n, d2 = zz.shape
    d = d2 // 2
    groups, lg, _ = w_sp.shape
    tm = max(lg, _pick(seq, tm))
    return pl.pallas_call(
        functools.partial(_gmlp_kernel, lg=lg, groups=groups, dg=d // groups, nsub=tm // lg),
        grid=(n // tm,),
        in_specs=[pl.BlockSpec((tm, d), lambda i: (i, 0)),
                  pl.BlockSpec((tm, d), lambda i: (i, 1)),
                  pl.BlockSpec((1, d), lambda i: (0, 0)),
                  pl.BlockSpec((1, d), lambda i: (0, 0)),
                  pl.BlockSpec((groups, lg, lg), lambda i: (0, 0, 0)),
                  pl.BlockSpec((lg, groups), lambda i: (0, 0))],
        out_specs=pl.BlockSpec((tm, d), lambda i: (i, 0)),
        out_shape=jax.ShapeDtypeStruct((n, d), BF16),
        compiler_params=_cparams(1),
        name="gmlp_spatial",
    )(zz, zz, ln_g.reshape(1, d), ln_b.reshape(1, d), w_sp, b_sp.T)


def _router_kernel(h_ref, w_ref, o_ref):
    h = h_ref[...]
    w = w_ref[...]
    hh = h.astype(BF16)
    hl = (h - hh.astype(F32)).astype(BF16)
    wh = w.astype(BF16)
    wl = (w - wh.astype(F32)).astype(BF16)
    o_ref[...] = _bdot(hh, wh) + _bdot(hl, wh) + _bdot(hh, wl)


def _router_logits(hf, w_router, *, tm=1024):
    n, d = hf.shape
    e = w_router.shape[1]
    wpad = jnp.pad(w_router, ((0, 0), (0, V7X_LANES - e)))
    tm = _pick(n, tm)
    out = pl.pallas_call(
        _router_kernel,
        grid=(n // tm,),
        in_specs=[pl.BlockSpec((tm, d), lambda i: (i, 0)),
                  pl.BlockSpec((d, V7X_LANES), lambda i: (0, 0))],
        out_specs=pl.BlockSpec((tm, V7X_LANES), lambda i: (i, 0)),
        out_shape=jax.ShapeDtypeStruct((n, V7X_LANES), F32),
        compiler_params=_cparams(1),
        name="moe_router",
    )(hf, wpad)
    return out[:, :e]


def _moe_ffn_kernel(te_ref, nu_ref, tok_ref, hf_hbm, w1_ref, w3_ref, w2_ref, o_ref, xg_ref, xb_ref, sem, *, tm):
    i = pl.program_id(0)
    f = pl.program_id(1)
    active = i < nu_ref[0]

    def row_copy(r, tok):
        return pltpu.make_async_copy(hf_hbm.at[pl.ds(tok, 1), :], xg_ref.at[pl.ds(r, 1), :], sem)

    @pl.when(jnp.logical_and(active, f == 0))
    def _():
        def issue(r, carry):
            row_copy(r, tok_ref[0, 0, r]).start()
            return carry

        lax.fori_loop(0, tm, issue, 0)

        def wait(r, carry):
            row_copy(r, tok_ref[0, 0, r]).wait()
            return carry

        lax.fori_loop(0, tm, wait, 0)
        xb_ref[...] = xg_ref[...].astype(BF16)

    @pl.when(active)
    def _():
        _swiglu_step(xb_ref[...], w1_ref.at[0], w3_ref.at[0], w2_ref.at[0], o_ref, f)

    @pl.when(jnp.logical_and(jnp.logical_not(active), f == 0))
    def _():
        o_ref[...] = jnp.zeros(o_ref.shape, F32)


def _moe_ffn(hf, row_tok, tile_e, n_used, w1, w3, w2, *, tm, tf=256):
    n, d = hf.shape
    ff = w1.shape[2]
    tf = _pick(ff, tf)
    nf = ff // tf
    n_tiles = row_tok.shape[0]

    def f_eff(i, f, nu):
        return jnp.where(i < nu[0], f, nf - 1)

    grid_spec = pltpu.PrefetchScalarGridSpec(
        num_scalar_prefetch=2,
        grid=(n_tiles, nf),
        in_specs=[pl.BlockSpec((1, 1, tm), lambda i, f, te, nu: (i, 0, 0), memory_space=pltpu.SMEM),
                  pl.BlockSpec(memory_space=pl.ANY),
                  pl.BlockSpec((1, d, tf), lambda i, f, te, nu: (te[i], 0, f_eff(i, f, nu))),
                  pl.BlockSpec((1, d, tf), lambda i, f, te, nu: (te[i], 0, f_eff(i, f, nu))),
                  pl.BlockSpec((1, tf, d), lambda i, f, te, nu: (te[i], f_eff(i, f, nu), 0))],
        out_specs=pl.BlockSpec((tm, d), lambda i, f, te, nu: (i, 0)),
        scratch_shapes=[pltpu.VMEM((tm, d), F32), pltpu.VMEM((tm, d), BF16), pltpu.SemaphoreType.DMA(())],
    )
    return pl.pallas_call(
        functools.partial(_moe_ffn_kernel, tm=tm),
        grid_spec=grid_spec,
        out_shape=jax.ShapeDtypeStruct((n_tiles * tm, d), F32),
        compiler_params=_cparams(2),
        name="moe_ffn",
    )(tile_e, n_used, row_tok, hf, w1, w3, w2)


def _moe_combine_kernel(pos_ref, y_hbm, x_ref, gate_ref, g_ref, gam_ref, bet_ref, o_ref, yb_ref, sem, *, tm, alpha):
    def row_copy(k, r):
        return pltpu.make_async_copy(y_hbm.at[pl.ds(pos_ref[0, 0, k * tm + r], 1), :],
                                     yb_ref.at[k, pl.ds(r, 1), :], sem)

    def issue(r, carry):
        for k in range(TOP_K):
            row_copy(k, r).start()
        return carry

    lax.fori_loop(0, tm, issue, 0)

    def wait(r, carry):
        for k in range(TOP_K):
            row_copy(k, r).wait()
        return carry

    lax.fori_loop(0, tm, wait, 0)
    gate = gate_ref[...]
    y = gate[:, 0:1] * yb_ref[0]
    for k in range(1, TOP_K):
        y = y + gate[:, k:k + 1] * yb_ref[k]
    r = alpha * x_ref[...] + g_ref[0] * y
    o_ref[...] = _layer_norm_rows(r, gam_ref[...], bet_ref[...])


def _moe_combine_ln(y_rows, pos, gates, x, g, gamma, beta, *, alpha, rows_per_batch, tm=256):
    n, d = x.shape
    tm = _pick(rows_per_batch, tm)
    tpb = rows_per_batch // tm
    nt = n // tm
    pos3 = pos.reshape(nt, tm, TOP_K).transpose(0, 2, 1).reshape(nt, 1, TOP_K * tm)
    row = pl.BlockSpec((tm, d), lambda i: (i, 0))
    vec = pl.BlockSpec((1, d), lambda i: (0, 0))
    return pl.pallas_call(
        functools.partial(_moe_combine_kernel, tm=tm, alpha=alpha),
        grid=(nt,),
        in_specs=[pl.BlockSpec((1, 1, TOP_K * tm), lambda i: (i, 0, 0), memory_space=pltpu.SMEM),
                  pl.BlockSpec(memory_space=pl.ANY),
                  row,
                  pl.BlockSpec((tm, TOP_K), lambda i: (i, 0)),
                  pl.BlockSpec((1, 1, d), lambda i: (i // tpb, 0, 0)),
                  vec, vec],
        out_specs=row,
        out_shape=jax.ShapeDtypeStruct((n, d), F32),
        scratch_shapes=[pltpu.VMEM((TOP_K, tm, d), F32), pltpu.SemaphoreType.DMA(())],
        compiler_params=_cparams(1),
        name="moe_combine_ln",
    )(pos3, y_rows, x, gates, g, gamma.reshape(1, d), beta.reshape(1, d))


def _route(logits, tm):
    n, e = logits.shape
    top_v, top_e = lax.top_k(logits, TOP_K)
    gates = jax.nn.softmax(top_v, axis=-1)
    e_flat = top_e.reshape(-1).astype(jnp.int32)
    onehot = (e_flat[:, None] == jnp.arange(e, dtype=jnp.int32)[None, :]).astype(jnp.int32)
    cum = jnp.cumsum(onehot, axis=0)
    rank = jnp.take_along_axis(cum, e_flat[:, None], axis=1)[:, 0] - 1
    counts = cum[-1]
    padded = (counts + tm - 1) // tm * tm
    pend = jnp.cumsum(padded)
    pstart = pend - padded
    dest = (pstart[e_flat] + rank).astype(jnp.int32)
    n_tiles = (n * TOP_K) // tm + e
    tok_flat = jnp.repeat(jnp.arange(n, dtype=jnp.int32), TOP_K)
    row_tok = jnp.zeros((n_tiles * tm,), jnp.int32).at[dest].set(tok_flat)
    n_used = (pend[-1] // tm).astype(jnp.int32)
    tidx = jnp.arange(n_tiles, dtype=jnp.int32)
    tile_e = jnp.minimum(jnp.searchsorted(pend, tidx * tm, side='right'), e - 1).astype(jnp.int32)
    tile_e = jnp.where(tidx < n_used, tile_e, tile_e[jnp.maximum(n_used - 1, 0)])
    return gates, dest.reshape(n, TOP_K), row_tok.reshape(n_tiles, 1, tm), tile_e, n_used.reshape(1)


def _moe_layer(x, hf, g2, gamma, beta, w_router, w1, w3, w2, *, alpha, rows_per_batch, tm=1024):
    n = hf.shape[0]
    tm = _pick(n * TOP_K, tm)
    logits = _router_logits(hf, w_router)
    gates, pos, row_tok, tile_e, n_used = _route(logits, tm)
    y_rows = _moe_ffn(hf, row_tok, tile_e, n_used, w1, w3, w2, tm=tm)
    return _moe_combine_ln(y_rows, pos, gates, x, g2, gamma, beta, alpha=alpha, rows_per_batch=rows_per_batch)


def kernel(x, c, ada_w, ada_b, ln1_g, ln1_b, ln2_g, ln2_b, a_w_in, a_b_in, a_w_dw, a_b_dw, a_ln_g, a_ln_b, a_w_out, a_b_out, b_w_in, b_b_in, b_a_re, b_a_im, b_log_dt, b_b_re, b_b_im, b_c_re, b_c_im, b_d, b_w_glu, b_b_glu, c_w_in, c_b_in, c_w_conv, c_b_conv, c_mh_g, c_w_out, c_b_out, d_w_in, d_b_in, d_ln_g, d_ln_b, d_w_sp, d_b_sp, d_w_out, d_b_out, f_w1, f_w3, f_w2, m_router, m_w1, m_w3, m_w2):
    bsz, seq, d = x.shape
    depth = ada_w.shape[0]
    alpha = (2.0 * depth) ** 0.25
    n = bsz * seq
    mm = functools.partial(_matmul, rows_per_batch=seq)

    mods = _ada_mods(c, ada_w, ada_b)
    xf = x.reshape(n, d)
    ia = ib = ic = idd = 0
    i_dense = i_moe = 0
    for layer in range(depth):
        sh1, sc1, g1, sh2, sc2, g2 = [mods[layer, :, k * d:(k + 1) * d].reshape(bsz, 1, d) for k in range(6)]
        kind = layer % N_MIXERS
        if kind == 0:
            glu = mm(xf, a_w_in[ia], a_b_in[ia], mod=(sc1, sh1), act="glu")
            u = _conv_ln_swish(glu, a_w_dw[ia], a_b_dw[ia], a_ln_g[ia], a_ln_b[ia], bsz=bsz, seq=seq)
            y = mm(u, a_w_out[ia], a_b_out[ia])
            ia += 1
        elif kind == 1:
            u = mm(xf, b_w_in[ib], b_b_in[ib], mod=(sc1, sh1))
            tables = _s5_tables(b_a_re[ib], b_a_im[ib], b_log_dt[ib], b_b_re[ib], b_b_im[ib],
                                b_c_re[ib], b_c_im[ib], b_d[ib])
            yg = _s5_core(u, tables, bsz=bsz, seq=seq)
            y = mm(yg, b_w_glu[ib], b_b_glu[ib], act="glu")
            ib += 1
        elif kind == 2:
            heads = ML_HEADS
            dv = d // heads
            n_main = 2 * heads * (dv // 2) + 2 * heads * dv
            w_in = jnp.pad(c_w_in[ic], ((0, 0), (0, V7X_LANES - 2 * heads)))
            b_in = jnp.pad(c_b_in[ic], ((0, V7X_LANES - 2 * heads),))
            assert w_in.shape[1] == n_main + V7X_LANES
            z = mm(xf, w_in, b_in, mod=(sc1, sh1), tn=896)
            hs = _mlstm_core(z, c_w_conv[ic], c_b_conv[ic], c_mh_g[ic], bsz=bsz, seq=seq, d=d)
            y = mm(hs, c_w_out[ic], c_b_out[ic])
            ic += 1
        else:
            zz = mm(xf, d_w_in[idd], d_b_in[idd], mod=(sc1, sh1), act="gelu")
            gated = _gmlp_core(zz, d_ln_g[idd], d_ln_b[idd], d_w_sp[idd], d_b_sp[idd], seq=seq)
            y = mm(gated, d_w_out[idd], d_b_out[idd])
            idd += 1
        is_moe = layer % 2 == 1
        xf, hf = _ln_res(xf, y, g1, ln1_g[layer], ln1_b[layer], alpha=alpha, rows_per_batch=seq,
                         mod=(sc2, sh2), h_dtype=F32 if is_moe else BF16)
        if not is_moe:
            y = _ffn(hf, f_w1[i_dense], f_w3[i_dense], f_w2[i_dense])
            xf = _ln_res(xf, y, g2, ln2_g[layer], ln2_b[layer], alpha=alpha, rows_per_batch=seq)
            i_dense += 1
        else:
            xf = _moe_layer(xf, hf, g2, ln2_g[layer], ln2_b[layer], m_router[i_moe], m_w1[i_moe],
                            m_w3[i_moe], m_w2[i_moe], alpha=alpha, rows_per_batch=seq)
            i_moe += 1
    return xf.reshape(bsz, seq, d)
```

```python
import functools
import math

import jax
import jax.numpy as jnp
from jax import lax
from jax.experimental import pallas as pl
from jax.experimental.pallas import tpu as pltpu

F32 = jnp.float32
BF16 = jnp.bfloat16

LN_EPS = 1e-5
N_MIXERS = 4
CONV_WIDTH = 31
S5_GROUP = 16
S5_STATE = 64
ML_HEADS = 8
ML_CONV = 4
GM_CHUNK = 128
GM_GROUPS = 8
N_EXPERTS = 8
TOP_K = 2

V7X_VMEM_BYTES = 64 * 1024 * 1024
V7X_LANES = 128
V7X_SUBLANES = 8
VMEM_LIMIT = V7X_VMEM_BYTES - 8 * 1024 * 1024

S5_CHUNK = 16
ML_CHUNK_ROWS = 256
CONV_HALO = 32


def _cparams(n_axes):
    return pltpu.CompilerParams(dimension_semantics=("arbitrary",) * n_axes,
                                vmem_limit_bytes=VMEM_LIMIT)


def _pick(n, pref):
    t = min(n, pref)
    while n % t:
        t //= 2
    return t


def _sigmoid(x):
    return 1.0 / (1.0 + jnp.exp(-x))


def _silu(x):
    return x * _sigmoid(x)


def _layer_norm_rows(v, gamma, beta):
    mu = jnp.mean(v, axis=-1, keepdims=True)
    d = v - mu
    var = jnp.mean(d * d, axis=-1, keepdims=True)
    y = d * lax.rsqrt(var + LN_EPS)
    if gamma is not None:
        y = y * gamma
    if beta is not None:
        y = y + beta
    return y


def _bdot(a, b):
    return jnp.dot(a, b, preferred_element_type=F32)


def _ada_kernel(c_ref, w_ref, b_ref, o_ref):
    cond = _silu(c_ref[...]).astype(BF16)
    o_ref[0] = _bdot(cond, w_ref[0].astype(BF16)) + b_ref[0]


def _ada_mods(c, ada_w, ada_b):
    depth, d, d6 = ada_w.shape
    bsz = c.shape[0]
    tn = _pick(d6, 1024)
    return pl.pallas_call(
        _ada_kernel,
        grid=(depth, d6 // tn),
        in_specs=[pl.BlockSpec((bsz, d), lambda l, j: (0, 0)),
                  pl.BlockSpec((1, d, tn), lambda l, j: (l, 0, j)),
                  pl.BlockSpec((1, 1, tn), lambda l, j: (l, 0, j))],
        out_specs=pl.BlockSpec((1, bsz, tn), lambda l, j: (l, 0, j)),
        out_shape=jax.ShapeDtypeStruct((depth, bsz, d6), F32),
        compiler_params=_cparams(2),
        name="ada_mods",
    )(c, ada_w, ada_b.reshape(depth, 1, d6))


def _mm_kernel(*refs, n_w, has_mod, use_scratch, act):
    it = iter(refs)
    x_ref = next(it)
    sc_ref = sh_ref = None
    if has_mod:
        sc_ref, sh_ref = next(it), next(it)
    w_refs = [next(it) for _ in range(n_w)]
    b_refs = [next(it) for _ in range(n_w)]
    o_ref = next(it)
    if use_scratch:
        xb_ref = next(it)

        @pl.when(pl.program_id(1) == 0)
        def _():
            xv = x_ref[...].astype(F32)
            if has_mod:
                xv = xv * (1.0 + sc_ref[0]) + sh_ref[0]
            xb_ref[...] = xv.astype(BF16)

        xb = xb_ref[...]
    else:
        xb = x_ref[...]
    z = [_bdot(xb, w_refs[k][...].astype(BF16)) + b_refs[k][...] for k in range(n_w)]
    if act == "glu":
        out = z[0] * _sigmoid(z[1])
    elif act == "gelu":
        out = jax.nn.gelu(z[0])
    else:
        out = z[0]
    o_ref[...] = out.astype(o_ref.dtype)


def _matmul(x, w, b, *, rows_per_batch, mod=None, act=None, out_dtype=F32, tm=1024, tn=512):
    n, k = x.shape
    nw = w.shape[1]
    n_out = nw // 2 if act == "glu" else nw
    tm = _pick(rows_per_batch, tm)
    tn = _pick(n_out, tn)
    tiles_per_batch = rows_per_batch // tm
    has_mod = mod is not None
    use_scratch = has_mod or x.dtype != BF16
    n_w = 2 if act == "glu" else 1
    half = n_out // tn

    in_specs = [pl.BlockSpec((tm, k), lambda i, j: (i, 0))]
    args = [x]
    if has_mod:
        mspec = pl.BlockSpec((1, 1, k), lambda i, j: (i // tiles_per_batch, 0, 0))
        in_specs += [mspec, mspec]
        args += [mod[0], mod[1]]
    b2 = b.reshape(1, nw)
    in_specs.append(pl.BlockSpec((k, tn), lambda i, j: (0, j)))
    args.append(w)
    if n_w == 2:
        in_specs.append(pl.BlockSpec((k, tn), lambda i, j: (0, j + half)))
        args.append(w)
    in_specs.append(pl.BlockSpec((1, tn), lambda i, j: (0, j)))
    args.append(b2)
    if n_w == 2:
        in_specs.append(pl.BlockSpec((1, tn), lambda i, j: (0, j + half)))
        args.append(b2)
    scratch = [pltpu.VMEM((tm, k), BF16)] if use_scratch else []
    return pl.pallas_call(
        functools.partial(_mm_kernel, n_w=n_w, has_mod=has_mod, use_scratch=use_scratch, act=act),
        grid=(n // tm, n_out // tn),
        in_specs=in_specs,
        out_specs=pl.BlockSpec((tm, tn), lambda i, j: (i, j)),
        out_shape=jax.ShapeDtypeStruct((n, n_out), out_dtype),
        scratch_shapes=scratch,
        compiler_params=_cparams(2),
        name="matmul_" + (act or "bias"),
    )(*args)


def _ln_res_kernel(*refs, alpha, emit_h):
    if emit_h:
        x_ref, y_ref, g_ref, gam_ref, bet_ref, sc_ref, sh_ref, o_ref, h_ref = refs
    else:
        x_ref, y_ref, g_ref, gam_ref, bet_ref, o_ref = refs
    r = alpha * x_ref[...] + g_ref[0] * y_ref[...].astype(F32)
    xn = _layer_norm_rows(r, gam_ref[...], bet_ref[...])
    o_ref[...] = xn
    if emit_h:
        h_ref[...] = (xn * (1.0 + sc_ref[0]) + sh_ref[0]).astype(h_ref.dtype)


def _ln_res(x, y, g, gamma, beta, *, alpha, rows_per_batch, mod=None, h_dtype=BF16, tm=512):
    n, d = x.shape
    tm = _pick(rows_per_batch, tm)
    tpb = rows_per_batch // tm
    row = pl.BlockSpec((tm, d), lambda i: (i, 0))
    per_b = pl.BlockSpec((1, 1, d), lambda i: (i // tpb, 0, 0))
    vec = pl.BlockSpec((1, d), lambda i: (0, 0))
    emit_h = mod is not None
    in_specs = [row, row, per_b, vec, vec]
    args = [x, y, g, gamma.reshape(1, d), beta.reshape(1, d)]
    out_shape = jax.ShapeDtypeStruct((n, d), F32)
    out_specs = row
    if emit_h:
        in_specs += [per_b, per_b]
        args += [mod[0], mod[1]]
        out_shape = (out_shape, jax.ShapeDtypeStruct((n, d), h_dtype))
        out_specs = (row, row)
    return pl.pallas_call(
        functools.partial(_ln_res_kernel, alpha=alpha, emit_h=emit_h),
        grid=(n // tm,),
        in_specs=in_specs,
        out_specs=out_specs,
        out_shape=out_shape,
        compiler_params=_cparams(1),
        name="ln_res",
    )(*args)


def _group_state(tg_ref, nu_ref):
    i = pl.program_id(1)
    active = i < nu_ref[0]
    changed = jnp.logical_or(i == 0, tg_ref[i] != tg_ref[jnp.maximum(i - 1, 0)])
    return active, jnp.logical_and(active, changed)


def _ffn_up_kernel(tg_ref, nu_ref, x_ref, w1_ref, w3_ref, h_ref, w1b_ref, w3b_ref):
    active, recast = _group_state(tg_ref, nu_ref)

    @pl.when(recast)
    def _():
        w1b_ref[...] = w1_ref[0].astype(BF16)
        w3b_ref[...] = w3_ref[0].astype(BF16)

    @pl.when(active)
    def _():
        x = x_ref[...]
        a = _bdot(x, w1b_ref[...])
        b = _bdot(x, w3b_ref[...])
        h_ref[...] = (_silu(a) * b).astype(h_ref.dtype)

    @pl.when(jnp.logical_not(active))
    def _():
        h_ref[...] = jnp.zeros(h_ref.shape, h_ref.dtype)


def _ffn_down_kernel(tg_ref, nu_ref, h_ref, w2_ref, o_ref, w2b_ref):
    active, recast = _group_state(tg_ref, nu_ref)

    @pl.when(recast)
    def _():
        w2b_ref[...] = w2_ref[0].astype(BF16)

    @pl.when(active)
    def _():
        o_ref[...] = _bdot(h_ref[...], w2b_ref[...])

    @pl.when(jnp.logical_not(active))
    def _():
        o_ref[...] = jnp.zeros(o_ref.shape, o_ref.dtype)


def _grouped_ffn(xb, w1, w3, w2, tile_g, n_used, *, tm, tf=512, tm_down=512, tn=512):
    r, d = xb.shape
    ff = w1.shape[2]
    tf = _pick(ff, tf)
    tn = _pick(d, tn)
    tm_down = _pick(tm, tm_down)
    n_tiles = r // tm
    h = pl.pallas_call(
        _ffn_up_kernel,
        grid_spec=pltpu.PrefetchScalarGridSpec(
            num_scalar_prefetch=2,
            grid=(ff // tf, n_tiles),
            in_specs=[pl.BlockSpec((tm, d), lambda f, i, tg, nu: (i, 0)),
                      pl.BlockSpec((1, d, tf), lambda f, i, tg, nu: (tg[i], 0, f)),
                      pl.BlockSpec((1, d, tf), lambda f, i, tg, nu: (tg[i], 0, f))],
            out_specs=pl.BlockSpec((tm, tf), lambda f, i, tg, nu: (i, f)),
            scratch_shapes=[pltpu.VMEM((d, tf), BF16), pltpu.VMEM((d, tf), BF16)]),
        out_shape=jax.ShapeDtypeStruct((r, ff), BF16),
        compiler_params=_cparams(2),
        name="ffn_up",
    )(tile_g, n_used, xb, w1, w3)
    sub = tm // tm_down
    tile_g2 = jnp.repeat(tile_g, sub)
    n_used2 = n_used * sub
    return pl.pallas_call(
        _ffn_down_kernel,
        grid_spec=pltpu.PrefetchScalarGridSpec(
            num_scalar_prefetch=2,
            grid=(d // tn, n_tiles * sub),
            in_specs=[pl.BlockSpec((tm_down, ff), lambda n, i, tg, nu: (i, 0)),
                      pl.BlockSpec((1, ff, tn), lambda n, i, tg, nu: (tg[i], 0, n))],
            out_specs=pl.BlockSpec((tm_down, tn), lambda n, i, tg, nu: (i, n)),
            scratch_shapes=[pltpu.VMEM((ff, tn), BF16)]),
        out_shape=jax.ShapeDtypeStruct((r, d), F32),
        compiler_params=_cparams(2),
        name="ffn_down",
    )(tile_g2, n_used2, h, w2)


def _conv_kernel(x_ref, w_ref, b_ref, g_ref, be_ref, o_ref, xx_ref, u_ref, *, ts, kw, rc, cc):
    i = pl.program_id(1)
    d = x_ref.shape[1]
    halo = CONV_HALO

    @pl.when(i == 0)
    def _():
        xx_ref[0:halo, :] = jnp.zeros((halo, d), F32)

    @pl.when(i > 0)
    def _():
        xx_ref[0:halo, :] = xx_ref[ts:ts + halo, :]

    xx_ref[halo:halo + ts, :] = x_ref[...]
    off = halo - (kw - 1)

    for r0 in range(0, ts, rc):
        for c0 in range(0, d, cc):
            acc = jnp.broadcast_to(b_ref[:, c0:c0 + cc], (rc, cc))
            for j in range(kw):
                acc = acc + w_ref[j:j + 1, c0:c0 + cc] * xx_ref[r0 + off + j:r0 + off + j + rc, c0:c0 + cc]
            u_ref[r0:r0 + rc, c0:c0 + cc] = acc
    y = _layer_norm_rows(u_ref[...], g_ref[...], be_ref[...])
    o_ref[...] = _silu(y).astype(o_ref.dtype)


def _conv_ln_swish(glu, w_dw, b_dw, ln_g, ln_b, *, bsz, seq, ts=128):
    n, d = glu.shape
    kw = w_dw.shape[0]
    ts = _pick(seq, ts)
    nt = seq // ts
    rc = _pick(ts, 32)
    cc = _pick(d, 512)
    row = pl.BlockSpec((ts, d), lambda b, i: (b * nt + i, 0))
    vec = pl.BlockSpec((1, d), lambda b, i: (0, 0))
    return pl.pallas_call(
        functools.partial(_conv_kernel, ts=ts, kw=kw, rc=rc, cc=cc),
        grid=(bsz, nt),
        in_specs=[row, pl.BlockSpec((kw, d), lambda b, i: (0, 0)), vec, vec, vec],
        out_specs=row,
        out_shape=jax.ShapeDtypeStruct((n, d), BF16),
        scratch_shapes=[pltpu.VMEM((CONV_HALO + ts, d), F32), pltpu.VMEM((ts, d), F32)],
        compiler_params=_cparams(2),
        name="conv_ln_swish",
    )(glu, w_dw, b_dw.reshape(1, d), ln_g.reshape(1, d), ln_b.reshape(1, d))


def _s5_kernel(u_ref, t_ref, mr_ref, mi_ref, cr_ref, ci_ref, lr_ref, li_ref, d_ref, o_ref,
               vr_ref, vi_ref, pr_ref, pi_ref, *, nb, nc, half):
    u = u_ref[0]
    ub = u.astype(BF16)
    vr_ref[...] = _bdot(ub, mr_ref[0])
    vi_ref[...] = _bdot(ub, mi_ref[0])
    lr = lr_ref[0]
    li = li_ref[0]

    slab = V7X_SUBLANES
    spc = slab // nb
    n2 = vr_ref.shape[1]
    srow = lax.broadcasted_iota(jnp.int32, (slab, n2), 0)

    def step(k, carry):
        sr, si = carry
        r0 = pl.multiple_of(k * slab, slab)
        xr = vr_ref[pl.ds(r0, slab), :]
        xi = vi_ref[pl.ds(r0, slab), :]
        prev_r = sr
        prev_i = si
        for j in range(spc):
            if j > 0:
                keep = srow < j * nb
                prev_r = jnp.where(keep, prev_r, pltpu.roll(sr, j * nb, 0))
                prev_i = jnp.where(keep, prev_i, pltpu.roll(si, j * nb, 0))
                xrj = pltpu.roll(xr, slab - j * nb, 0)
                xij = pltpu.roll(xi, slab - j * nb, 0)
            else:
                xrj, xij = xr, xi
            sr, si = lr * sr - li * si + xrj, lr * si + li * sr + xij
        pr_ref[pl.ds(r0, slab), :] = prev_r
        pi_ref[pl.ds(r0, slab), :] = prev_i
        return sr, si

    zero = jnp.zeros((slab, n2), F32)
    lax.fori_loop(0, nc // spc, step, (zero, zero))
    y_inter = (_bdot(pr_ref[...].astype(BF16), cr_ref[0]) + _bdot(pi_ref[...].astype(BF16), ci_ref[0]))
    y_intra = jnp.concatenate([_bdot(ub[:, :half], t_ref[0, 0]), _bdot(ub[:, half:], t_ref[0, 1])], axis=1)
    y = y_intra + y_inter + d_ref[0] * u
    o_ref[0] = jax.nn.gelu(y).astype(o_ref.dtype)


def _s5_tables(a_re, a_im, log_dt, b_re, b_im, c_re, c_im, d_skip):
    g, n = a_re.shape
    p = b_re.shape[2]
    lc = S5_CHUNK
    ar, ai = a_re.astype(F32), a_im.astype(F32)
    dt = jnp.exp(log_dt.astype(F32))[:, None]
    decay = jnp.exp(ar * dt)
    lr, li = decay * jnp.cos(ai * dt), decay * jnp.sin(ai * dt)
    den = ar * ar + ai * ai
    zr = ((lr - 1.0) * ar + li * ai) / den
    zi = (li * ar - (lr - 1.0) * ai) / den
    br, bi = b_re.astype(F32), b_im.astype(F32)
    bbr = zr[..., None] * br - zi[..., None] * bi
    bbi = zr[..., None] * bi + zi[..., None] * br
    tau = jnp.arange(lc + 1, dtype=F32)[:, None, None]
    pdec = jnp.exp(tau * (ar * dt)[None])
    pr, pi = pdec * jnp.cos(tau * (ai * dt)[None]), pdec * jnp.sin(tau * (ai * dt)[None])
    cr, ci = c_re.astype(F32), c_im.astype(F32)
    hp = lax.Precision.HIGHEST
    lbr = pr[..., None] * bbr[None] - pi[..., None] * bbi[None]
    lbi = pr[..., None] * bbi[None] + pi[..., None] * bbr[None]
    ktau = (jnp.einsum('gpn,tgnq->tgpq', cr, lbr[:lc], precision=hp)
            - jnp.einsum('gpn,tgnq->tgpq', ci, lbi[:lc], precision=hp))
    s_idx = jnp.arange(lc)[:, None]
    t_idx = jnp.arange(lc)[None, :]
    diff = t_idx - s_idx
    kg = ktau[jnp.clip(diff, 0, lc - 1)]
    kg = jnp.where((diff >= 0)[:, :, None, None, None], kg, 0.0)
    toep = kg.transpose(2, 0, 4, 1, 3).reshape(g, lc * p, lc * p)
    mr = lbr[lc - 1 - jnp.arange(lc)].transpose(1, 0, 3, 2).reshape(g, lc * p, n)
    mi = lbi[lc - 1 - jnp.arange(lc)].transpose(1, 0, 3, 2).reshape(g, lc * p, n)
    pr1, pi1 = pr[1:], pi[1:]
    cmr = (cr[None] * pr1[:, :, None, :] - ci[None] * pi1[:, :, None, :])
    cmi = -(cr[None] * pi1[:, :, None, :] + ci[None] * pr1[:, :, None, :])
    cmr = cmr.transpose(1, 3, 0, 2).reshape(g, n, lc * p)
    cmi = cmi.transpose(1, 3, 0, 2).reshape(g, n, lc * p)
    g2 = g // 2
    w = lc * p

    def pair_in(m):
        m = m.reshape(g2, 2, w, n)
        z = jnp.zeros((g2, w, n), F32)
        top = jnp.concatenate([m[:, 0], z], axis=2)
        bot = jnp.concatenate([z, m[:, 1]], axis=2)
        return jnp.concatenate([top, bot], axis=1)

    def pair_out(m):
        m = m.reshape(g2, 2, n, w)
        z = jnp.zeros((g2, n, w), F32)
        top = jnp.concatenate([m[:, 0], z], axis=2)
        bot = jnp.concatenate([z, m[:, 1]], axis=2)
        return jnp.concatenate([top, bot], axis=1)

    lam_r = pr[lc].reshape(g2, 1, 2 * n)
    lam_i = pi[lc].reshape(g2, 1, 2 * n)
    dvec = jnp.broadcast_to(d_skip.astype(F32).reshape(g2, 2, 1, p), (g2, 2, lc, p)).reshape(g2, 1, 2 * w)
    return (toep.reshape(g2, 2, w, w).astype(BF16), pair_in(mr).astype(BF16), pair_in(mi).astype(BF16),
            pair_out(cmr).astype(BF16), pair_out(cmi).astype(BF16), lam_r, lam_i, dvec)


def _s5_core(u, tables, *, bsz, seq):
    n_rows, d = u.shape
    toep, mr, mi, cmr, cmi, lam_r, lam_i, dvec = tables
    g2 = toep.shape[0]
    lc = S5_CHUNK
    p = S5_GROUP
    w2 = 2 * lc * p
    nc = seq // lc
    m = nc * bsz
    n2 = lam_r.shape[2]
    ut = u.reshape(bsz, nc, lc, g2, 2, p).transpose(3, 1, 0, 4, 2, 5).reshape(g2, m, w2)
    blk = lambda *shape: pl.BlockSpec((1,) + shape, lambda i: (i,) + (0,) * len(shape))
    yt = pl.pallas_call(
        functools.partial(_s5_kernel, nb=bsz, nc=nc, half=w2 // 2),
        grid=(g2,),
        in_specs=[blk(m, w2), blk(2, w2 // 2, w2 // 2), blk(w2, n2), blk(w2, n2), blk(n2, w2), blk(n2, w2),
                  blk(1, n2), blk(1, n2), blk(1, w2)],
        out_specs=blk(m, w2),
        out_shape=jax.ShapeDtypeStruct((g2, m, w2), BF16),
        scratch_shapes=[pltpu.VMEM((m, n2), F32)] * 4,
        compiler_params=_cparams(1),
        name="s5_core",
    )(ut, toep, mr, mi, cmr, cmi, lam_r, lam_i, dvec)
    return yt.reshape(g2, nc, bsz, 2, lc, p).transpose(2, 1, 4, 0, 3, 5).reshape(n_rows, d)


def _log_sigmoid(x):
    return jnp.minimum(x, 0.0) - jnp.log(1.0 + jnp.exp(-jnp.abs(x)))


def _mlstm_kernel(q_ref, k_ref, v_ref, o_ref, g_ref, wq_ref, wk_ref, bq_ref, bk_ref, mh_ref, out_ref,
                  c_st, n_st, m_st, qx_ref, kx_ref, *, lm, dk, heads, kw):
    h = pl.program_id(1)
    c = pl.program_id(2)
    pad = V7X_SUBLANES

    @pl.when(c == 0)
    def _():
        c_st[...] = jnp.zeros(c_st.shape, F32)
        n_st[...] = jnp.zeros(n_st.shape, F32)
        m_st[...] = jnp.zeros(m_st.shape, F32)
        qx_ref[0:pad, :] = jnp.zeros((pad, dk), F32)
        kx_ref[0:pad, :] = jnp.zeros((pad, dk), F32)

    def conv_swish(x_ref, xx_ref, w_ref, b_ref):
        xx_ref[pad:pad + lm, :] = x_ref[...]
        acc = jnp.broadcast_to(b_ref[...], (lm, dk))
        for j in range(kw):
            acc = acc + w_ref[j:j + 1, :] * xx_ref[pad - (kw - 1) + j:pad - (kw - 1) + j + lm, :]
        xx_ref[0:pad, :] = xx_ref[lm:lm + pad, :]
        return _silu(acc)

    q = conv_swish(q_ref, qx_ref, wq_ref, bq_ref)
    k = conv_swish(k_ref, kx_ref, wk_ref, bk_ref) * (dk ** -0.5)
    v = v_ref[...]

    gts = g_ref[...]
    nl = gts.shape[1]
    lane = lax.broadcasted_iota(jnp.int32, (lm, nl), 1)
    lf = _log_sigmoid(gts)
    rows = lax.broadcasted_iota(jnp.int32, (lm, lm), 0)
    cols = lax.broadcasted_iota(jnp.int32, (lm, lm), 1)
    causal = rows >= cols
    tri = causal.astype(BF16)
    lf_hi = lf.astype(BF16)
    r1 = lf - lf_hi.astype(F32)
    lf_mid = r1.astype(BF16)
    lf_lo = (r1 - lf_mid.astype(F32)).astype(BF16)
    bcum = _bdot(tri, lf_hi) + _bdot(tri, lf_mid) + _bdot(tri, lf_lo)

    def col_of(mat, idx):
        return jnp.sum(jnp.where(lane == idx, mat, 0.0), axis=1, keepdims=True)

    def row_of(mat_t, idx):
        sub = lax.broadcasted_iota(jnp.int32, mat_t.shape, 0)
        return jnp.sum(jnp.where(sub == idx, mat_t, 0.0), axis=0, keepdims=True)

    b_col = col_of(bcum, heads + h)
    li_col = col_of(gts, h)
    b_row = row_of(bcum.T, heads + h)
    li_row = row_of(gts.T, h)

    m_prev = m_st[...]
    dmat = jnp.where(causal, b_col - b_row + li_row, -jnp.inf)
    inter = b_col + m_prev
    m_j = jnp.maximum(inter, jnp.max(dmat, axis=1, keepdims=True))
    w_intra = jnp.exp(dmat - m_j)
    w_inter = jnp.exp(inter - m_j)
    qb, kb, vb = q.astype(BF16), k.astype(BF16), v.astype(BF16)
    sc = lax.dot_general(qb, kb, (((1,), (1,)), ((), ())), preferred_element_type=F32) * w_intra
    num = w_inter * _bdot(qb, c_st[...].astype(BF16)) + _bdot(sc.astype(BF16), vb)
    den = w_inter * jnp.sum(q * n_st[...], axis=1, keepdims=True) + jnp.sum(sc, axis=1, keepdims=True)
    hb = num / jnp.maximum(jnp.abs(den), jnp.exp(-m_j))

    bl = b_col[lm - 1:lm, :]
    gsum = bl - b_col + li_col
    m_new = jnp.maximum(bl + m_prev, jnp.max(gsum, axis=0, keepdims=True))
    wc = jnp.exp(bl + m_prev - m_new)
    wk = jnp.exp(gsum - m_new)
    kwt = wk * k
    c_st[...] = wc * c_st[...] + _bdot(kwt.T.astype(BF16), vb)
    n_st[...] = wc * n_st[...] + jnp.sum(kwt, axis=0, keepdims=True)
    m_st[...] = m_new

    hs = _sigmoid(o_ref[...]) * hb
    out_ref[...] = (_layer_norm_rows(hs, None, None) * mh_ref[...]).astype(out_ref.dtype)


def _mlstm_core(z, w_conv, b_conv, mh_g, *, bsz, seq, d):
    n_rows = z.shape[0]
    heads = ML_HEADS
    dv = d // heads
    dk = dv // 2
    lm = _pick(seq, ML_CHUNK_ROWS)
    nc = seq // lm
    kw = w_conv.shape[0]
    gate_blk = (2 * heads * dk + 2 * heads * dv) // V7X_LANES
    vblk0 = (2 * heads * dk) // dv
    row = lambda b, h, c: b * nc + c
    in_specs = [
        pl.BlockSpec((lm, dk), lambda b, h, c: (row(b, h, c), h)),
        pl.BlockSpec((lm, dk), lambda b, h, c: (row(b, h, c), heads + h)),
        pl.BlockSpec((lm, dv), lambda b, h, c: (row(b, h, c), vblk0 + h)),
        pl.BlockSpec((lm, dv), lambda b, h, c: (row(b, h, c), vblk0 + heads + h)),
        pl.BlockSpec((lm, V7X_LANES), lambda b, h, c: (row(b, h, c), gate_blk)),
        pl.BlockSpec((kw, dk), lambda b, h, c: (0, h)),
        pl.BlockSpec((kw, dk), lambda b, h, c: (0, heads + h)),
        pl.BlockSpec((1, dk), lambda b, h, c: (0, h)),
        pl.BlockSpec((1, dk), lambda b, h, c: (0, heads + h)),
        pl.BlockSpec((1, dv), lambda b, h, c: (0, h)),
    ]
    b2 = b_conv.reshape(1, -1)
    return pl.pallas_call(
        functools.partial(_mlstm_kernel, lm=lm, dk=dk, heads=heads, kw=kw),
        grid=(bsz, heads, nc),
        in_specs=in_specs,
        out_specs=pl.BlockSpec((lm, dv), lambda b, h, c: (row(b, h, c), h)),
        out_shape=jax.ShapeDtypeStruct((n_rows, d), BF16),
        scratch_shapes=[pltpu.VMEM((dk, dv), F32), pltpu.VMEM((1, dk), F32), pltpu.VMEM((1, 1), F32),
                        pltpu.VMEM((lm + V7X_SUBLANES, dk), F32), pltpu.VMEM((lm + V7X_SUBLANES, dk), F32)],
        compiler_params=_cparams(3),
        name="mlstm_core",
    )(z, z, z, z, z, w_conv, w_conv, b2, b2, mh_g.reshape(1, d))


def _gmlp_kernel(u_ref, v_ref, g_ref, be_ref, w_ref, bsp_ref, o_ref, *, lg, groups, dg, nsub):
    rows = lax.broadcasted_iota(jnp.int32, (lg, lg), 0)
    cols = lax.broadcasted_iota(jnp.int32, (lg, lg), 1)
    keep = rows >= cols
    wm = [jnp.where(keep, w_ref[g], 0.0).astype(BF16) for g in range(groups)]
    for s in range(nsub):
        r = slice(s * lg, (s + 1) * lg)
        vn = _layer_norm_rows(v_ref[r, :], g_ref[...], be_ref[...]).astype(BF16)
        for g in range(groups):
            cs = slice(g * dg, (g + 1) * dg)
            sv = _bdot(wm[g], vn[:, cs]) + bsp_ref[:, g:g + 1]
            o_ref[r, cs] = (u_ref[r, cs] * sv).astype(o_ref.dtype)


def _gmlp_core(zz, ln_g, ln_b, w_sp, b_sp, *, seq, tm=512):
    n, d2 = zz.shape
    d = d2 // 2
    groups, lg, _ = w_sp.shape
    tm = max(lg, _pick(seq, tm))
    return pl.pallas_call(
        functools.partial(_gmlp_kernel, lg=lg, groups=groups, dg=d // groups, nsub=tm // lg),
        grid=(n // tm,),
        in_specs=[pl.BlockSpec((tm, d), lambda i: (i, 0)),
                  pl.BlockSpec((tm, d), lambda i: (i, 1)),
                  pl.BlockSpec((1, d), lambda i: (0, 0)),
                  pl.BlockSpec((1, d), lambda i: (0, 0)),
                  pl.BlockSpec((groups, lg, lg), lambda i: (0, 0, 0)),
                  pl.BlockSpec((lg, groups), lambda i: (0, 0))],
        out_specs=pl.BlockSpec((tm, d), lambda i: (i, 0)),
        out_shape=jax.ShapeDtypeStruct((n, d), BF16),
        compiler_params=_cparams(1),
        name="gmlp_spatial",
    )(zz, zz, ln_g.reshape(1, d), ln_b.reshape(1, d), w_sp, b_sp.T)


def _router_kernel(h_ref, w_ref, o_ref):
    h = h_ref[...]
    w = w_ref[...]
    hh = h.astype(BF16)
    hl = (h - hh.astype(F32)).astype(BF16)
    wh = w.astype(BF16)
    wl = (w - wh.astype(F32)).astype(BF16)
    o_ref[...] = _bdot(hh, wh) + _bdot(hl, wh) + _bdot(hh, wl)


def _router_logits(hf, w_router, *, tm=1024):
    n, d = hf.shape
    e = w_router.shape[1]
    wpad = jnp.pad(w_router, ((0, 0), (0, V7X_LANES - e)))
    tm = _pick(n, tm)
    out = pl.pallas_call(
        _router_kernel,
        grid=(n // tm,),
        in_specs=[pl.BlockSpec((tm, d), lambda i: (i, 0)),
                  pl.BlockSpec((d, V7X_LANES), lambda i: (0, 0))],
        out_specs=pl.BlockSpec((tm, V7X_LANES), lambda i: (i, 0)),
        out_shape=jax.ShapeDtypeStruct((n, V7X_LANES), F32),
        compiler_params=_cparams(1),
        name="moe_router",
    )(hf, wpad)
    return out[:, :e]


def _moe_gather_kernel(nu_ref, tok_ref, tokn_ref, hf_hbm, o_ref, xg_ref, sem, *, tm):
    i = pl.program_id(0)
    nu = nu_ref[0]
    slot = lax.rem(i, 2)

    def row_copy(tref, r, s):
        return pltpu.make_async_copy(hf_hbm.at[pl.ds(tref[0, 0, r], 1), :], xg_ref.at[s, pl.ds(r, 1), :], sem.at[s])

    def start_tile(tref, s):
        def issue(r, carry):
            row_copy(tref, r, s).start()
            return carry

        lax.fori_loop(0, tm, issue, 0)

    @pl.when(jnp.logical_and(i == 0, nu > 0))
    def _():
        start_tile(tok_ref, 0)

    @pl.when(i + 1 < nu)
    def _():
        start_tile(tokn_ref, 1 - slot)

    @pl.when(i < nu)
    def _():
        def wait(r, carry):
            row_copy(tok_ref, r, slot).wait()
            return carry

        lax.fori_loop(0, tm, wait, 0)
        o_ref[...] = xg_ref[slot].astype(o_ref.dtype)

    @pl.when(i >= nu)
    def _():
        o_ref[...] = jnp.zeros(o_ref.shape, o_ref.dtype)


def _moe_gather(hf, row_tok, n_used, *, tm):
    n, d = hf.shape
    n_tiles = row_tok.shape[0]
    tok_spec = lambda fn: pl.BlockSpec((1, 1, tm), fn, memory_space=pltpu.SMEM)
    return pl.pallas_call(
        functools.partial(_moe_gather_kernel, tm=tm),
        grid_spec=pltpu.PrefetchScalarGridSpec(
            num_scalar_prefetch=1,
            grid=(n_tiles,),
            in_specs=[tok_spec(lambda i, nu: (i, 0, 0)),
                      tok_spec(lambda i, nu: (jnp.minimum(i + 1, n_tiles - 1), 0, 0)),
                      pl.BlockSpec(memory_space=pl.ANY)],
            out_specs=pl.BlockSpec((tm, d), lambda i, nu: (i, 0)),
            scratch_shapes=[pltpu.VMEM((2, tm, d), hf.dtype), pltpu.SemaphoreType.DMA((2,))]),
        out_shape=jax.ShapeDtypeStruct((n_tiles * tm, d), BF16),
        compiler_params=_cparams(1),
        name="moe_gather",
    )(n_used, row_tok, row_tok, hf)


def _moe_combine_kernel(pos_ref, y_hbm, x_ref, gate_ref, g_ref, gam_ref, bet_ref, o_ref, yb_ref, sem, *, tm, alpha):
    def row_copy(k, r):
        return pltpu.make_async_copy(y_hbm.at[pl.ds(pos_ref[0, 0, k * tm + r], 1), :],
                                     yb_ref.at[k, pl.ds(r, 1), :], sem)

    def issue(r, carry):
        for k in range(TOP_K):
            row_copy(k, r).start()
        return carry

    lax.fori_loop(0, tm, issue, 0)

    def wait(r, carry):
        for k in range(TOP_K):
            row_copy(k, r).wait()
        return carry

    lax.fori_loop(0, tm, wait, 0)
    gate = gate_ref[...]
    y = gate[:, 0:1] * yb_ref[0]
    for k in range(1, TOP_K):
        y = y + gate[:, k:k + 1] * yb_ref[k]
    r = alpha * x_ref[...] + g_ref[0] * y
    o_ref[...] = _layer_norm_rows(r, gam_ref[...], bet_ref[...])


def _moe_combine_ln(y_rows, pos, gates, x, g, gamma, beta, *, alpha, rows_per_batch, tm=256):
    n, d = x.shape
    tm = _pick(rows_per_batch, tm)
    tpb = rows_per_batch // tm
    nt = n // tm
    pos3 = pos.reshape(nt, tm, TOP_K).transpose(0, 2, 1).reshape(nt, 1, TOP_K * tm)
    row = pl.BlockSpec((tm, d), lambda i: (i, 0))
    vec = pl.BlockSpec((1, d), lambda i: (0, 0))
    return pl.pallas_call(
        functools.partial(_moe_combine_kernel, tm=tm, alpha=alpha),
        grid=(nt,),
        in_specs=[pl.BlockSpec((1, 1, TOP_K * tm), lambda i: (i, 0, 0), memory_space=pltpu.SMEM),
                  pl.BlockSpec(memory_space=pl.ANY),
                  row,
                  pl.BlockSpec((tm, TOP_K), lambda i: (i, 0)),
                  pl.BlockSpec((1, 1, d), lambda i: (i // tpb, 0, 0)),
                  vec, vec],
        out_specs=row,
        out_shape=jax.ShapeDtypeStruct((n, d), F32),
        scratch_shapes=[pltpu.VMEM((TOP_K, tm, d), F32), pltpu.SemaphoreType.DMA(())],
        compiler_params=_cparams(1),
        name="moe_combine_ln",
    )(pos3, y_rows, x, gates, g, gamma.reshape(1, d), beta.reshape(1, d))


def _route(logits, tm):
    n, e = logits.shape
    top_v, top_e = lax.top_k(logits, TOP_K)
    gates = jax.nn.softmax(top_v, axis=-1)
    e_flat = top_e.reshape(-1).astype(jnp.int32)
    onehot = (e_flat[:, None] == jnp.arange(e, dtype=jnp.int32)[None, :]).astype(jnp.int32)
    cum = jnp.cumsum(onehot, axis=0)
    rank = jnp.take_along_axis(cum, e_flat[:, None], axis=1)[:, 0] - 1
    counts = cum[-1]
    padded = (counts + tm - 1) // tm * tm
    pend = jnp.cumsum(padded)
    pstart = pend - padded
    dest = (pstart[e_flat] + rank).astype(jnp.int32)
    n_tiles = (n * TOP_K) // tm + e
    tok_flat = jnp.repeat(jnp.arange(n, dtype=jnp.int32), TOP_K)
    row_tok = jnp.zeros((n_tiles * tm,), jnp.int32).at[dest].set(tok_flat)
    n_used = (pend[-1] // tm).astype(jnp.int32)
    tidx = jnp.arange(n_tiles, dtype=jnp.int32)
    tile_e = jnp.minimum(jnp.searchsorted(pend, tidx * tm, side='right'), e - 1).astype(jnp.int32)
    tile_e = jnp.where(tidx < n_used, tile_e, tile_e[jnp.maximum(n_used - 1, 0)])
    return gates, dest.reshape(n, TOP_K), row_tok.reshape(n_tiles, 1, tm), tile_e, n_used.reshape(1)


def _moe_layer(x, hf, g2, gamma, beta, w_router, w1, w3, w2, layer_idx, *, alpha, rows_per_batch, tm=1024):
    n = hf.shape[0]
    tm = _pick(n * TOP_K, tm)
    logits = _router_logits(hf, w_router)
    gates, pos, row_tok, tile_e, n_used = _route(logits, tm)
    xs = _moe_gather(hf, row_tok, n_used, tm=tm)
    y_rows = _grouped_ffn(xs, w1, w3, w2, tile_e + layer_idx * N_EXPERTS, n_used, tm=tm)
    return _moe_combine_ln(y_rows, pos, gates, x, g2, gamma, beta, alpha=alpha, rows_per_batch=rows_per_batch)


def kernel(x, c, ada_w, ada_b, ln1_g, ln1_b, ln2_g, ln2_b, a_w_in, a_b_in, a_w_dw, a_b_dw, a_ln_g, a_ln_b, a_w_out, a_b_out, b_w_in, b_b_in, b_a_re, b_a_im, b_log_dt, b_b_re, b_b_im, b_c_re, b_c_im, b_d, b_w_glu, b_b_glu, c_w_in, c_b_in, c_w_conv, c_b_conv, c_mh_g, c_w_out, c_b_out, d_w_in, d_b_in, d_ln_g, d_ln_b, d_w_sp, d_b_sp, d_w_out, d_b_out, f_w1, f_w3, f_w2, m_router, m_w1, m_w3, m_w2):
    bsz, seq, d = x.shape
    depth = ada_w.shape[0]
    alpha = (2.0 * depth) ** 0.25
    n = bsz * seq
    mm = functools.partial(_matmul, rows_per_batch=seq)

    mods = _ada_mods(c, ada_w, ada_b)
    xf = x.reshape(n, d)
    ia = ib = ic = idd = 0
    i_dense = i_moe = 0
    for layer in range(depth):
        sh1, sc1, g1, sh2, sc2, g2 = [mods[layer, :, k * d:(k + 1) * d].reshape(bsz, 1, d) for k in range(6)]
        kind = layer % N_MIXERS
        if kind == 0:
            glu = mm(xf, a_w_in[ia], a_b_in[ia], mod=(sc1, sh1), act="glu")
            u = _conv_ln_swish(glu, a_w_dw[ia], a_b_dw[ia], a_ln_g[ia], a_ln_b[ia], bsz=bsz, seq=seq)
            y = mm(u, a_w_out[ia], a_b_out[ia])
            ia += 1
        elif kind == 1:
            u = mm(xf, b_w_in[ib], b_b_in[ib], mod=(sc1, sh1))
            tables = _s5_tables(b_a_re[ib], b_a_im[ib], b_log_dt[ib], b_b_re[ib], b_b_im[ib],
                                b_c_re[ib], b_c_im[ib], b_d[ib])
            yg = _s5_core(u, tables, bsz=bsz, seq=seq)
            y = mm(yg, b_w_glu[ib], b_b_glu[ib], act="glu")
            ib += 1
        elif kind == 2:
            heads = ML_HEADS
            dv = d // heads
            n_main = 2 * heads * (dv // 2) + 2 * heads * dv
            w_in = jnp.pad(c_w_in[ic], ((0, 0), (0, V7X_LANES - 2 * heads)))
            b_in = jnp.pad(c_b_in[ic], ((0, V7X_LANES - 2 * heads),))
            assert w_in.shape[1] == n_main + V7X_LANES
            z = mm(xf, w_in, b_in, mod=(sc1, sh1), tn=896)
            hs = _mlstm_core(z, c_w_conv[ic], c_b_conv[ic], c_mh_g[ic], bsz=bsz, seq=seq, d=d)
            y = mm(hs, c_w_out[ic], c_b_out[ic])
            ic += 1
        else:
            zz = mm(xf, d_w_in[idd], d_b_in[idd], mod=(sc1, sh1), act="gelu")
            gated = _gmlp_core(zz, d_ln_g[idd], d_ln_b[idd], d_w_sp[idd], d_b_sp[idd], seq=seq)
            y = mm(gated, d_w_out[idd], d_b_out[idd])
            idd += 1
        is_moe = layer % 2 == 1
        xf, hf = _ln_res(xf, y, g1, ln1_g[layer], ln1_b[layer], alpha=alpha, rows_per_batch=seq,
                         mod=(sc2, sh2), h_dtype=F32 if is_moe else BF16)
        if not is_moe:
            tm_ffn = _pick(n, 1024)
            tile_g = jnp.full((n // tm_ffn,), i_dense, jnp.int32)
            n_used = jnp.full((1,), n // tm_ffn, jnp.int32)
            y = _grouped_ffn(hf, f_w1, f_w3, f_w2, tile_g, n_used, tm=tm_ffn)
            xf = _ln_res(xf, y, g2, ln2_g[layer], ln2_b[layer], alpha=alpha, rows_per_batch=seq)
            i_dense += 1
        else:
            merge = lambda w: w.reshape((w.shape[0] * w.shape[1],) + w.shape[2:])
            xf = _moe_layer(xf, hf, g2, ln2_g[layer], ln2_b[layer], m_router[i_moe], merge(m_w1),
                            merge(m_w3), merge(m_w2), i_moe, alpha=alpha, rows_per_batch=seq)
            i_moe += 1
    return xf.reshape(bsz, seq, d)
```

```python
import functools
import math

import jax
import jax.numpy as jnp
from jax import lax
from jax.experimental import pallas as pl
from jax.experimental.pallas import tpu as pltpu

F32 = jnp.float32
BF16 = jnp.bfloat16

LN_EPS = 1e-5
N_MIXERS = 4
CONV_WIDTH = 31
S5_GROUP = 16
S5_STATE = 64
ML_HEADS = 8
ML_CONV = 4
GM_CHUNK = 128
GM_GROUPS = 8
N_EXPERTS = 8
TOP_K = 2

V7X_VMEM_BYTES = 64 * 1024 * 1024
V7X_LANES = 128
V7X_SUBLANES = 8
VMEM_LIMIT = V7X_VMEM_BYTES - 8 * 1024 * 1024

S5_CHUNK = 16
ML_CHUNK_ROWS = 256
CONV_HALO = 32
DMA_UNROLL = 8


def _cparams(n_axes):
    return pltpu.CompilerParams(dimension_semantics=("arbitrary",) * n_axes,
                                vmem_limit_bytes=VMEM_LIMIT)


def _pick(n, pref):
    t = min(n, pref)
    while n % t:
        t //= 2
    return t


def _sigmoid(x):
    return 1.0 / (1.0 + jnp.exp(-x))


def _silu(x):
    return x * _sigmoid(x)


def _layer_norm_rows(v, gamma, beta):
    mu = jnp.mean(v, axis=-1, keepdims=True)
    d = v - mu
    var = jnp.mean(d * d, axis=-1, keepdims=True)
    y = d * lax.rsqrt(var + LN_EPS)
    if gamma is not None:
        y = y * gamma
    if beta is not None:
        y = y + beta
    return y


def _bdot(a, b):
    return jnp.dot(a, b, preferred_element_type=F32)


def _ada_kernel(c_ref, w_ref, b_ref, o_ref):
    cond = _silu(c_ref[...]).astype(BF16)
    o_ref[0] = _bdot(cond, w_ref[0].astype(BF16)) + b_ref[0]


def _ada_mods(c, ada_w, ada_b):
    depth, d, d6 = ada_w.shape
    bsz = c.shape[0]
    tn = _pick(d6, 1024)
    return pl.pallas_call(
        _ada_kernel,
        grid=(depth, d6 // tn),
        in_specs=[pl.BlockSpec((bsz, d), lambda l, j: (0, 0)),
                  pl.BlockSpec((1, d, tn), lambda l, j: (l, 0, j)),
                  pl.BlockSpec((1, 1, tn), lambda l, j: (l, 0, j))],
        out_specs=pl.BlockSpec((1, bsz, tn), lambda l, j: (l, 0, j)),
        out_shape=jax.ShapeDtypeStruct((depth, bsz, d6), F32),
        compiler_params=_cparams(2),
        name="ada_mods",
    )(c, ada_w, ada_b.reshape(depth, 1, d6))


def _mm_kernel(*refs, n_w, has_mod, use_scratch, act):
    it = iter(refs)
    x_ref = next(it)
    sc_ref = sh_ref = None
    if has_mod:
        sc_ref, sh_ref = next(it), next(it)
    w_refs = [next(it) for _ in range(n_w)]
    b_refs = [next(it) for _ in range(n_w)]
    o_ref = next(it)
    if use_scratch:
        xb_ref = next(it)

        @pl.when(pl.program_id(1) == 0)
        def _():
            xv = x_ref[...].astype(F32)
            if has_mod:
                xv = xv * (1.0 + sc_ref[0]) + sh_ref[0]
            xb_ref[...] = xv.astype(BF16)

        xb = xb_ref[...]
    else:
        xb = x_ref[...]
    z = [_bdot(xb, w_refs[k][...].astype(BF16)) + b_refs[k][...] for k in range(n_w)]
    if act == "glu":
        out = z[0] * _sigmoid(z[1])
    elif act == "gelu":
        out = jax.nn.gelu(z[0])
    else:
        out = z[0]
    o_ref[...] = out.astype(o_ref.dtype)


def _matmul(x, w, b, *, rows_per_batch, mod=None, act=None, out_dtype=F32, tm=1024, tn=512):
    n, k = x.shape
    nw = w.shape[1]
    n_out = nw // 2 if act == "glu" else nw
    tm = _pick(rows_per_batch, tm)
    tn = _pick(n_out, tn)
    tiles_per_batch = rows_per_batch // tm
    has_mod = mod is not None
    use_scratch = has_mod or x.dtype != BF16
    n_w = 2 if act == "glu" else 1
    half = n_out // tn

    in_specs = [pl.BlockSpec((tm, k), lambda i, j: (i, 0))]
    args = [x]
    if has_mod:
        mspec = pl.BlockSpec((1, 1, k), lambda i, j: (i // tiles_per_batch, 0, 0))
        in_specs += [mspec, mspec]
        args += [mod[0], mod[1]]
    b2 = b.reshape(1, nw)
    in_specs.append(pl.BlockSpec((k, tn), lambda i, j: (0, j)))
    args.append(w)
    if n_w == 2:
        in_specs.append(pl.BlockSpec((k, tn), lambda i, j: (0, j + half)))
        args.append(w)
    in_specs.append(pl.BlockSpec((1, tn), lambda i, j: (0, j)))
    args.append(b2)
    if n_w == 2:
        in_specs.append(pl.BlockSpec((1, tn), lambda i, j: (0, j + half)))
        args.append(b2)
    scratch = [pltpu.VMEM((tm, k), BF16)] if use_scratch else []
    return pl.pallas_call(
        functools.partial(_mm_kernel, n_w=n_w, has_mod=has_mod, use_scratch=use_scratch, act=act),
        grid=(n // tm, n_out // tn),
        in_specs=in_specs,
        out_specs=pl.BlockSpec((tm, tn), lambda i, j: (i, j)),
        out_shape=jax.ShapeDtypeStruct((n, n_out), out_dtype),
        scratch_shapes=scratch,
        compiler_params=_cparams(2),
        name="matmul_" + (act or "bias"),
    )(*args)


def _ln_res_kernel(*refs, alpha, emit_h):
    if emit_h:
        x_ref, y_ref, g_ref, gam_ref, bet_ref, sc_ref, sh_ref, o_ref, h_ref = refs
    else:
        x_ref, y_ref, g_ref, gam_ref, bet_ref, o_ref = refs
    r = alpha * x_ref[...] + g_ref[0] * y_ref[...].astype(F32)
    xn = _layer_norm_rows(r, gam_ref[...], bet_ref[...])
    o_ref[...] = xn
    if emit_h:
        h_ref[...] = (xn * (1.0 + sc_ref[0]) + sh_ref[0]).astype(h_ref.dtype)


def _ln_res(x, y, g, gamma, beta, *, alpha, rows_per_batch, mod=None, h_dtype=BF16, tm=512):
    n, d = x.shape
    tm = _pick(rows_per_batch, tm)
    tpb = rows_per_batch // tm
    row = pl.BlockSpec((tm, d), lambda i: (i, 0))
    per_b = pl.BlockSpec((1, 1, d), lambda i: (i // tpb, 0, 0))
    vec = pl.BlockSpec((1, d), lambda i: (0, 0))
    emit_h = mod is not None
    in_specs = [row, row, per_b, vec, vec]
    args = [x, y, g, gamma.reshape(1, d), beta.reshape(1, d)]
    out_shape = jax.ShapeDtypeStruct((n, d), F32)
    out_specs = row
    if emit_h:
        in_specs += [per_b, per_b]
        args += [mod[0], mod[1]]
        out_shape = (out_shape, jax.ShapeDtypeStruct((n, d), h_dtype))
        out_specs = (row, row)
    return pl.pallas_call(
        functools.partial(_ln_res_kernel, alpha=alpha, emit_h=emit_h),
        grid=(n // tm,),
        in_specs=in_specs,
        out_specs=out_specs,
        out_shape=out_shape,
        compiler_params=_cparams(1),
        name="ln_res",
    )(*args)


def _group_state(tg_ref, nu_ref):
    i = pl.program_id(1)
    active = i < nu_ref[0]
    changed = jnp.logical_or(i == 0, tg_ref[i] != tg_ref[jnp.maximum(i - 1, 0)])
    return active, jnp.logical_and(active, changed)


def _ffn_up_kernel(tg_ref, nu_ref, x_ref, w1_ref, w3_ref, h_ref, w1b_ref, w3b_ref):
    active, recast = _group_state(tg_ref, nu_ref)

    @pl.when(recast)
    def _():
        w1b_ref[...] = w1_ref[0].astype(BF16)
        w3b_ref[...] = w3_ref[0].astype(BF16)

    @pl.when(active)
    def _():
        x = x_ref[...]
        a = _bdot(x, w1b_ref[...])
        b = _bdot(x, w3b_ref[...])
        h_ref[...] = (_silu(a) * b).astype(h_ref.dtype)

    @pl.when(jnp.logical_not(active))
    def _():
        h_ref[...] = jnp.zeros(h_ref.shape, h_ref.dtype)


def _ffn_down_kernel(tg_ref, nu_ref, h_ref, w2_ref, o_ref, w2b_ref):
    active, recast = _group_state(tg_ref, nu_ref)

    @pl.when(recast)
    def _():
        w2b_ref[...] = w2_ref[0].astype(BF16)

    @pl.when(active)
    def _():
        o_ref[...] = _bdot(h_ref[...], w2b_ref[...])

    @pl.when(jnp.logical_not(active))
    def _():
        o_ref[...] = jnp.zeros(o_ref.shape, o_ref.dtype)


def _grouped_ffn(xb, w1, w3, w2, tile_g, n_used, *, tm, tf=512, tm_down=512, tn=512):
    r, d = xb.shape
    ff = w1.shape[2]
    tf = _pick(ff, tf)
    tn = _pick(d, tn)
    tm_down = _pick(tm, tm_down)
    n_tiles = r // tm
    h = pl.pallas_call(
        _ffn_up_kernel,
        grid_spec=pltpu.PrefetchScalarGridSpec(
            num_scalar_prefetch=2,
            grid=(ff // tf, n_tiles),
            in_specs=[pl.BlockSpec((tm, d), lambda f, i, tg, nu: (i, 0)),
                      pl.BlockSpec((1, d, tf), lambda f, i, tg, nu: (tg[i], 0, f)),
                      pl.BlockSpec((1, d, tf), lambda f, i, tg, nu: (tg[i], 0, f))],
            out_specs=pl.BlockSpec((tm, tf), lambda f, i, tg, nu: (i, f)),
            scratch_shapes=[pltpu.VMEM((d, tf), BF16), pltpu.VMEM((d, tf), BF16)]),
        out_shape=jax.ShapeDtypeStruct((r, ff), BF16),
        compiler_params=_cparams(2),
        name="ffn_up",
    )(tile_g, n_used, xb, w1, w3)
    sub = tm // tm_down
    tile_g2 = jnp.repeat(tile_g, sub)
    n_used2 = n_used * sub
    return pl.pallas_call(
        _ffn_down_kernel,
        grid_spec=pltpu.PrefetchScalarGridSpec(
            num_scalar_prefetch=2,
            grid=(d // tn, n_tiles * sub),
            in_specs=[pl.BlockSpec((tm_down, ff), lambda n, i, tg, nu: (i, 0)),
                      pl.BlockSpec((1, ff, tn), lambda n, i, tg, nu: (tg[i], 0, n))],
            out_specs=pl.BlockSpec((tm_down, tn), lambda n, i, tg, nu: (i, n)),
            scratch_shapes=[pltpu.VMEM((ff, tn), BF16)]),
        out_shape=jax.ShapeDtypeStruct((r, d), F32),
        compiler_params=_cparams(2),
        name="ffn_down",
    )(tile_g2, n_used2, h, w2)


def _conv_kernel(x_ref, w_ref, b_ref, g_ref, be_ref, o_ref, xx_ref, u_ref, *, ts, kw, rc, cc):
    i = pl.program_id(1)
    d = x_ref.shape[1]
    halo = CONV_HALO

    @pl.when(i == 0)
    def _():
        xx_ref[0:halo, :] = jnp.zeros((halo, d), F32)

    @pl.when(i > 0)
    def _():
        xx_ref[0:halo, :] = xx_ref[ts:ts + halo, :]

    xx_ref[halo:halo + ts, :] = x_ref[...]
    off = halo - (kw - 1)

    for r0 in range(0, ts, rc):
        for c0 in range(0, d, cc):
            acc = jnp.broadcast_to(b_ref[:, c0:c0 + cc], (rc, cc))
            for j in range(kw):
                acc = acc + w_ref[j:j + 1, c0:c0 + cc] * xx_ref[r0 + off + j:r0 + off + j + rc, c0:c0 + cc]
            u_ref[r0:r0 + rc, c0:c0 + cc] = acc
    y = _layer_norm_rows(u_ref[...], g_ref[...], be_ref[...])
    o_ref[...] = _silu(y).astype(o_ref.dtype)


def _conv_ln_swish(glu, w_dw, b_dw, ln_g, ln_b, *, bsz, seq, ts=128):
    n, d = glu.shape
    kw = w_dw.shape[0]
    ts = _pick(seq, ts)
    nt = seq // ts
    rc = _pick(ts, 32)
    cc = _pick(d, 512)
    row = pl.BlockSpec((ts, d), lambda b, i: (b * nt + i, 0))
    vec = pl.BlockSpec((1, d), lambda b, i: (0, 0))
    return pl.pallas_call(
        functools.partial(_conv_kernel, ts=ts, kw=kw, rc=rc, cc=cc),
        grid=(bsz, nt),
        in_specs=[row, pl.BlockSpec((kw, d), lambda b, i: (0, 0)), vec, vec, vec],
        out_specs=row,
        out_shape=jax.ShapeDtypeStruct((n, d), BF16),
        scratch_shapes=[pltpu.VMEM((CONV_HALO + ts, d), F32), pltpu.VMEM((ts, d), F32)],
        compiler_params=_cparams(2),
        name="conv_ln_swish",
    )(glu, w_dw, b_dw.reshape(1, d), ln_g.reshape(1, d), ln_b.reshape(1, d))


def _s5_kernel(u_ref, t_ref, m_ref, c_ref, lr_ref, li_ref, d_ref, o_ref, a_ref, v_ref, p_ref, *, lc, nc):
    lanes = u_ref.shape[2]
    for t in range(lc):
        a_ref[:, t * lanes:(t + 1) * lanes] = u_ref[0, pl.ds(t, nc, stride=lc), :]
    a = a_ref[...]
    ab = a.astype(BF16)
    v_ref[...] = _bdot(ab, m_ref[0])
    lr = lr_ref[0]
    li = li_ref[0]
    ns = lr.shape[1]

    slab = V7X_SUBLANES
    srow = lax.broadcasted_iota(jnp.int32, (slab, ns), 0)

    def step(k, carry):
        sr, si = carry
        r0 = pl.multiple_of(k * slab, slab)
        xr = v_ref[pl.ds(r0, slab), 0:ns]
        xi = v_ref[pl.ds(r0, slab), ns:2 * ns]
        prev_r = sr
        prev_i = si
        for j in range(slab):
            if j > 0:
                keep = srow < j
                prev_r = jnp.where(keep, prev_r, pltpu.roll(sr, j, 0))
                prev_i = jnp.where(keep, prev_i, pltpu.roll(si, j, 0))
                xrj = pltpu.roll(xr, slab - j, 0)
                xij = pltpu.roll(xi, slab - j, 0)
            else:
                xrj, xij = xr, xi
            sr, si = lr * sr - li * si + xrj, lr * si + li * sr + xij
        p_ref[pl.ds(r0, slab), 0:ns] = prev_r
        p_ref[pl.ds(r0, slab), ns:2 * ns] = prev_i
        return sr, si

    zero = jnp.zeros((slab, ns), F32)
    lax.fori_loop(0, nc // slab, step, (zero, zero))
    y = _bdot(ab, t_ref[0]) + _bdot(p_ref[...].astype(BF16), c_ref[0]) + d_ref[0] * a
    y = jax.nn.gelu(y)
    for t in range(lc):
        o_ref[0, pl.ds(t, nc, stride=lc), :] = y[:, t * lanes:(t + 1) * lanes]


def _s5_tables(a_re, a_im, log_dt, b_re, b_im, c_re, c_im, d_skip):
    g, n = a_re.shape
    p = b_re.shape[2]
    lc = S5_CHUNK
    ar, ai = a_re.astype(F32), a_im.astype(F32)
    dt = jnp.exp(log_dt.astype(F32))[:, None]
    decay = jnp.exp(ar * dt)
    lr, li = decay * jnp.cos(ai * dt), decay * jnp.sin(ai * dt)
    den = ar * ar + ai * ai
    zr = ((lr - 1.0) * ar + li * ai) / den
    zi = (li * ar - (lr - 1.0) * ai) / den
    br, bi = b_re.astype(F32), b_im.astype(F32)
    bbr = zr[..., None] * br - zi[..., None] * bi
    bbi = zr[..., None] * bi + zi[..., None] * br
    tau = jnp.arange(lc + 1, dtype=F32)[:, None, None]
    pdec = jnp.exp(tau * (ar * dt)[None])
    pr, pi = pdec * jnp.cos(tau * (ai * dt)[None]), pdec * jnp.sin(tau * (ai * dt)[None])
    cr, ci = c_re.astype(F32), c_im.astype(F32)
    hp = lax.Precision.HIGHEST
    lbr = pr[..., None] * bbr[None] - pi[..., None] * bbi[None]
    lbi = pr[..., None] * bbi[None] + pi[..., None] * bbr[None]
    ktau = (jnp.einsum('gpn,tgnq->tgpq', cr, lbr[:lc], precision=hp)
            - jnp.einsum('gpn,tgnq->tgpq', ci, lbi[:lc], precision=hp))
    s_idx = jnp.arange(lc)[:, None]
    t_idx = jnp.arange(lc)[None, :]
    diff = t_idx - s_idx
    kg = ktau[jnp.clip(diff, 0, lc - 1)]
    kg = jnp.where((diff >= 0)[:, :, None, None, None], kg, 0.0)
    gb = V7X_LANES // p
    nj = g // gb
    eye = jnp.eye(gb, dtype=F32)
    wk = lc * gb * p
    toep = jnp.einsum('stjgpq,gh->jsgqthp', kg.reshape(lc, lc, nj, gb, p, p), eye).reshape(nj, wk, wk)
    rev = lc - 1 - jnp.arange(lc)
    m_re = jnp.einsum('sjgnq,gh->jsgqhn', lbr[rev].reshape(lc, nj, gb, n, p), eye)
    m_im = jnp.einsum('sjgnq,gh->jsgqhn', lbi[rev].reshape(lc, nj, gb, n, p), eye)
    m_in = jnp.stack([m_re, m_im], axis=4).reshape(nj, wk, 2 * gb * n)
    pr1, pi1 = pr[1:], pi[1:]
    cmr = (cr[None] * pr1[:, :, None, :] - ci[None] * pi1[:, :, None, :])
    cmi = -(cr[None] * pi1[:, :, None, :] + ci[None] * pr1[:, :, None, :])
    c_re_t = jnp.einsum('tjgpn,gh->jgnthp', cmr.reshape(lc, nj, gb, p, n), eye)
    c_im_t = jnp.einsum('tjgpn,gh->jgnthp', cmi.reshape(lc, nj, gb, p, n), eye)
    c_out = jnp.stack([c_re_t, c_im_t], axis=1).reshape(nj, 2 * gb * n, wk)
    lam_r = pr[lc].reshape(nj, 1, gb * n)
    lam_i = pi[lc].reshape(nj, 1, gb * n)
    dvec = jnp.broadcast_to(d_skip.astype(F32).reshape(nj, 1, 1, gb * p), (nj, 1, lc, gb * p)).reshape(nj, 1, wk)
    return toep.astype(BF16), m_in.astype(BF16), c_out.astype(BF16), lam_r, lam_i, dvec


def _s5_core(u, tables, *, bsz, seq):
    n_rows, d = u.shape
    toep, m_in, c_out, lam_r, lam_i, dvec = tables
    nj, wk, ns2 = m_in.shape
    lc = S5_CHUNK
    nc = seq // lc
    lanes = V7X_LANES
    assert nc % V7X_SUBLANES == 0 and wk == lc * lanes and nj * lanes == d
    tab = lambda *shape: pl.BlockSpec((1,) + shape, lambda j, b: (j,) + (0,) * len(shape))
    seq_blk = pl.BlockSpec((1, seq, lanes), lambda j, b: (b, 0, j))
    y = pl.pallas_call(
        functools.partial(_s5_kernel, lc=lc, nc=nc),
        grid=(nj, bsz),
        in_specs=[seq_blk, tab(wk, wk), tab(wk, ns2), tab(ns2, wk), tab(1, ns2 // 2), tab(1, ns2 // 2), tab(1, wk)],
        out_specs=seq_blk,
        out_shape=jax.ShapeDtypeStruct((bsz, seq, d), F32),
        scratch_shapes=[pltpu.VMEM((nc, wk), F32), pltpu.VMEM((nc, ns2), F32), pltpu.VMEM((nc, ns2), F32)],
        compiler_params=_cparams(2),
        name="s5_core",
    )(u.reshape(bsz, seq, d), toep, m_in, c_out, lam_r, lam_i, dvec)
    return y.reshape(n_rows, d)


def _log_sigmoid(x):
    return jnp.minimum(x, 0.0) - jnp.log(1.0 + jnp.exp(-jnp.abs(x)))


def _mlstm_kernel(q_ref, k_ref, v_ref, o_ref, g_ref, wq_ref, wk_ref, bq_ref, bk_ref, mh_ref, out_ref,
                  c_st, n_st, m_st, qx_ref, kx_ref, *, lm, dk, heads, kw):
    h = pl.program_id(1)
    c = pl.program_id(2)
    pad = V7X_SUBLANES

    @pl.when(c == 0)
    def _():
        c_st[...] = jnp.zeros(c_st.shape, F32)
        n_st[...] = jnp.zeros(n_st.shape, F32)
        m_st[...] = jnp.zeros(m_st.shape, F32)
        qx_ref[0:pad, :] = jnp.zeros((pad, dk), F32)
        kx_ref[0:pad, :] = jnp.zeros((pad, dk), F32)

    def conv_swish(x_ref, xx_ref, w_ref, b_ref):
        xx_ref[pad:pad + lm, :] = x_ref[...]
        acc = jnp.broadcast_to(b_ref[...], (lm, dk))
        for j in range(kw):
            acc = acc + w_ref[j:j + 1, :] * xx_ref[pad - (kw - 1) + j:pad - (kw - 1) + j + lm, :]
        xx_ref[0:pad, :] = xx_ref[lm:lm + pad, :]
        return _silu(acc)

    q = conv_swish(q_ref, qx_ref, wq_ref, bq_ref)
    k = conv_swish(k_ref, kx_ref, wk_ref, bk_ref) * (dk ** -0.5)
    v = v_ref[...]

    gts = g_ref[...]
    nl = gts.shape[1]
    lane = lax.broadcasted_iota(jnp.int32, (lm, nl), 1)
    lf = _log_sigmoid(gts)
    rows = lax.broadcasted_iota(jnp.int32, (lm, lm), 0)
    cols = lax.broadcasted_iota(jnp.int32, (lm, lm), 1)
    causal = rows >= cols
    tri = causal.astype(BF16)
    lf_hi = lf.astype(BF16)
    r1 = lf - lf_hi.astype(F32)
    lf_mid = r1.astype(BF16)
    lf_lo = (r1 - lf_mid.astype(F32)).astype(BF16)
    bcum = _bdot(tri, lf_hi) + _bdot(tri, lf_mid) + _bdot(tri, lf_lo)

    def col_of(mat, idx):
        return jnp.sum(jnp.where(lane == idx, mat, 0.0), axis=1, keepdims=True)

    def row_of(mat_t, idx):
        sub = lax.broadcasted_iota(jnp.int32, mat_t.shape, 0)
        return jnp.sum(jnp.where(sub == idx, mat_t, 0.0), axis=0, keepdims=True)

    b_col = col_of(bcum, heads + h)
    li_col = col_of(gts, h)
    b_row = row_of(bcum.T, heads + h)
    li_row = row_of(gts.T, h)

    m_prev = m_st[...]
    dmat = jnp.where(causal, b_col - b_row + li_row, -jnp.inf)
    inter = b_col + m_prev
    m_j = jnp.maximum(inter, jnp.max(dmat, axis=1, keepdims=True))
    w_intra = jnp.exp(dmat - m_j)
    w_inter = jnp.exp(inter - m_j)
    qb, kb, vb = q.astype(BF16), k.astype(BF16), v.astype(BF16)
    sc = lax.dot_general(qb, kb, (((1,), (1,)), ((), ())), preferred_element_type=F32) * w_intra
    num = w_inter * _bdot(qb, c_st[...].astype(BF16)) + _bdot(sc.astype(BF16), vb)
    den = w_inter * jnp.sum(q * n_st[...], axis=1, keepdims=True) + jnp.sum(sc, axis=1, keepdims=True)
    hb = num / jnp.maximum(jnp.abs(den), jnp.exp(-m_j))

    bl = b_col[lm - 1:lm, :]
    gsum = bl - b_col + li_col
    m_new = jnp.maximum(bl + m_prev, jnp.max(gsum, axis=0, keepdims=True))
    wc = jnp.exp(bl + m_prev - m_new)
    wk = jnp.exp(gsum - m_new)
    kwt = wk * k
    c_st[...] = wc * c_st[...] + _bdot(kwt.T.astype(BF16), vb)
    n_st[...] = wc * n_st[...] + jnp.sum(kwt, axis=0, keepdims=True)
    m_st[...] = m_new

    hs = _sigmoid(o_ref[...]) * hb
    out_ref[...] = (_layer_norm_rows(hs, None, None) * mh_ref[...]).astype(out_ref.dtype)


def _mlstm_core(z, w_conv, b_conv, mh_g, *, bsz, seq, d):
    n_rows = z.shape[0]
    heads = ML_HEADS
    dv = d // heads
    dk = dv // 2
    lm = _pick(seq, ML_CHUNK_ROWS)
    nc = seq // lm
    kw = w_conv.shape[0]
    gate_blk = (2 * heads * dk + 2 * heads * dv) // V7X_LANES
    vblk0 = (2 * heads * dk) // dv
    row = lambda b, h, c: b * nc + c
    in_specs = [
        pl.BlockSpec((lm, dk), lambda b, h, c: (row(b, h, c), h)),
        pl.BlockSpec((lm, dk), lambda b, h, c: (row(b, h, c), heads + h)),
        pl.BlockSpec((lm, dv), lambda b, h, c: (row(b, h, c), vblk0 + h)),
        pl.BlockSpec((lm, dv), lambda b, h, c: (row(b, h, c), vblk0 + heads + h)),
        pl.BlockSpec((lm, V7X_LANES), lambda b, h, c: (row(b, h, c), gate_blk)),
        pl.BlockSpec((kw, dk), lambda b, h, c: (0, h)),
        pl.BlockSpec((kw, dk), lambda b, h, c: (0, heads + h)),
        pl.BlockSpec((1, dk), lambda b, h, c: (0, h)),
        pl.BlockSpec((1, dk), lambda b, h, c: (0, heads + h)),
        pl.BlockSpec((1, dv), lambda b, h, c: (0, h)),
    ]
    b2 = b_conv.reshape(1, -1)
    return pl.pallas_call(
        functools.partial(_mlstm_kernel, lm=lm, dk=dk, heads=heads, kw=kw),
        grid=(bsz, heads, nc),
        in_specs=in_specs,
        out_specs=pl.BlockSpec((lm, dv), lambda b, h, c: (row(b, h, c), h)),
        out_shape=jax.ShapeDtypeStruct((n_rows, d), BF16),
        scratch_shapes=[pltpu.VMEM((dk, dv), F32), pltpu.VMEM((1, dk), F32), pltpu.VMEM((1, 1), F32),
                        pltpu.VMEM((lm + V7X_SUBLANES, dk), F32), pltpu.VMEM((lm + V7X_SUBLANES, dk), F32)],
        compiler_params=_cparams(3),
        name="mlstm_core",
    )(z, z, z, z, z, w_conv, w_conv, b2, b2, mh_g.reshape(1, d))


def _gmlp_kernel(u_ref, v_ref, g_ref, be_ref, w_ref, bsp_ref, o_ref, *, lg, groups, dg, nsub):
    rows = lax.broadcasted_iota(jnp.int32, (lg, lg), 0)
    cols = lax.broadcasted_iota(jnp.int32, (lg, lg), 1)
    keep = rows >= cols
    wm = [jnp.where(keep, w_ref[g], 0.0).astype(BF16) for g in range(groups)]
    for s in range(nsub):
        r = slice(s * lg, (s + 1) * lg)
        vn = _layer_norm_rows(v_ref[r, :], g_ref[...], be_ref[...]).astype(BF16)
        for g in range(groups):
            cs = slice(g * dg, (g + 1) * dg)
            sv = _bdot(wm[g], vn[:, cs]) + bsp_ref[:, g:g + 1]
            o_ref[r, cs] = (u_ref[r, cs] * sv).astype(o_ref.dtype)


def _gmlp_core(zz, ln_g, ln_b, w_sp, b_sp, *, seq, tm=512):
    n, d2 = zz.shape
    d = d2 // 2
    groups, lg, _ = w_sp.shape
    tm = max(lg, _pick(seq, tm))
    return pl.pallas_call(
        functools.partial(_gmlp_kernel, lg=lg, groups=groups, dg=d // groups, nsub=tm // lg),
        grid=(n // tm,),
        in_specs=[pl.BlockSpec((tm, d), lambda i: (i, 0)),
                  pl.BlockSpec((tm, d), lambda i: (i, 1)),
                  pl.BlockSpec((1, d), lambda i: (0, 0)),
                  pl.BlockSpec((1, d), lambda i: (0, 0)),
                  pl.BlockSpec((groups, lg, lg), lambda i: (0, 0, 0)),
                  pl.BlockSpec((lg, groups), lambda i: (0, 0))],
        out_specs=pl.BlockSpec((tm, d), lambda i: (i, 0)),
        out_shape=jax.ShapeDtypeStruct((n, d), BF16),
        compiler_params=_cparams(1),
        name="gmlp_spatial",
    )(zz, zz, ln_g.reshape(1, d), ln_b.reshape(1, d), w_sp, b_sp.T)


def _router_kernel(h_ref, w_ref, o_ref):
    h = h_ref[...]
    w = w_ref[...]
    hh = h.astype(BF16)
    hl = (h - hh.astype(F32)).astype(BF16)
    wh = w.astype(BF16)
    wl = (w - wh.astype(F32)).astype(BF16)
    o_ref[...] = _bdot(hh, wh) + _bdot(hl, wh) + _bdot(hh, wl)


def _router_logits(hf, w_router, *, tm=1024):
    n, d = hf.shape
    e = w_router.shape[1]
    wpad = jnp.pad(w_router, ((0, 0), (0, V7X_LANES - e)))
    tm = _pick(n, tm)
    out = pl.pallas_call(
        _router_kernel,
        grid=(n // tm,),
        in_specs=[pl.BlockSpec((tm, d), lambda i: (i, 0)),
                  pl.BlockSpec((d, V7X_LANES), lambda i: (0, 0))],
        out_specs=pl.BlockSpec((tm, V7X_LANES), lambda i: (i, 0)),
        out_shape=jax.ShapeDtypeStruct((n, V7X_LANES), F32),
        compiler_params=_cparams(1),
        name="moe_router",
    )(hf, wpad)
    return out[:, :e]


def _moe_gather_kernel(nu_ref, tok_ref, tokn_ref, hf_hbm, o_ref, xg_ref, sem, *, tm):
    i = pl.program_id(0)
    nu = nu_ref[0]
    slot = lax.rem(i, 2)

    def row_copy(tref, r, s):
        return pltpu.make_async_copy(hf_hbm.at[pl.ds(tref[0, 0, r], 1), :], xg_ref.at[s, pl.ds(r, 1), :], sem.at[s])

    def start_tile(tref, s):
        def issue(r, carry):
            row_copy(tref, r, s).start()
            return carry

        lax.fori_loop(0, tm, issue, 0, unroll=DMA_UNROLL)

    @pl.when(jnp.logical_and(i == 0, nu > 0))
    def _():
        start_tile(tok_ref, 0)

    @pl.when(i + 1 < nu)
    def _():
        start_tile(tokn_ref, 1 - slot)

    @pl.when(i < nu)
    def _():
        def wait(r, carry):
            row_copy(tok_ref, r, slot).wait()
            return carry

        lax.fori_loop(0, tm, wait, 0, unroll=DMA_UNROLL)
        o_ref[...] = xg_ref[slot].astype(o_ref.dtype)

    @pl.when(i >= nu)
    def _():
        o_ref[...] = jnp.zeros(o_ref.shape, o_ref.dtype)


def _moe_gather(hf, row_tok, n_used, *, tm):
    n, d = hf.shape
    n_tiles = row_tok.shape[0]
    tok_spec = lambda fn: pl.BlockSpec((1, 1, tm), fn, memory_space=pltpu.SMEM)
    return pl.pallas_call(
        functools.partial(_moe_gather_kernel, tm=tm),
        grid_spec=pltpu.PrefetchScalarGridSpec(
            num_scalar_prefetch=1,
            grid=(n_tiles,),
            in_specs=[tok_spec(lambda i, nu: (i, 0, 0)),
                      tok_spec(lambda i, nu: (jnp.minimum(i + 1, n_tiles - 1), 0, 0)),
                      pl.BlockSpec(memory_space=pl.ANY)],
            out_specs=pl.BlockSpec((tm, d), lambda i, nu: (i, 0)),
            scratch_shapes=[pltpu.VMEM((2, tm, d), hf.dtype), pltpu.SemaphoreType.DMA((2,))]),
        out_shape=jax.ShapeDtypeStruct((n_tiles * tm, d), BF16),
        compiler_params=_cparams(1),
        name="moe_gather",
    )(n_used, row_tok, row_tok, hf)


def _moe_combine_kernel(pos_ref, posn_ref, y_hbm, x_ref, gate_ref, g_ref, gam_ref, bet_ref, o_ref, yb_ref, sem,
                        *, tm, alpha):
    i = pl.program_id(0)
    nt = pl.num_programs(0)
    slot = lax.rem(i, 2)

    def row_copy(pref, k, r, s):
        return pltpu.make_async_copy(y_hbm.at[pl.ds(pref[0, 0, k * tm + r], 1), :],
                                     yb_ref.at[s, k, pl.ds(r, 1), :], sem.at[s])

    def start_tile(pref, s):
        def issue(r, carry):
            for k in range(TOP_K):
                row_copy(pref, k, r, s).start()
            return carry

        lax.fori_loop(0, tm, issue, 0, unroll=DMA_UNROLL)

    @pl.when(i == 0)
    def _():
        start_tile(pos_ref, 0)

    @pl.when(i + 1 < nt)
    def _():
        start_tile(posn_ref, 1 - slot)

    def wait(r, carry):
        for k in range(TOP_K):
            row_copy(pos_ref, k, r, slot).wait()
        return carry

    lax.fori_loop(0, tm, wait, 0, unroll=DMA_UNROLL)
    gate = gate_ref[...]
    y = gate[:, 0:1] * yb_ref[slot, 0]
    for k in range(1, TOP_K):
        y = y + gate[:, k:k + 1] * yb_ref[slot, k]
    r = alpha * x_ref[...] + g_ref[0] * y
    o_ref[...] = _layer_norm_rows(r, gam_ref[...], bet_ref[...])


def _moe_combine_ln(y_rows, pos, gates, x, g, gamma, beta, *, alpha, rows_per_batch, tm=256):
    n, d = x.shape
    tm = _pick(rows_per_batch, tm)
    tpb = rows_per_batch // tm
    nt = n // tm
    pos3 = pos.reshape(nt, tm, TOP_K).transpose(0, 2, 1).reshape(nt, 1, TOP_K * tm)
    row = pl.BlockSpec((tm, d), lambda i: (i, 0))
    vec = pl.BlockSpec((1, d), lambda i: (0, 0))
    return pl.pallas_call(
        functools.partial(_moe_combine_kernel, tm=tm, alpha=alpha),
        grid=(nt,),
        in_specs=[pl.BlockSpec((1, 1, TOP_K * tm), lambda i: (i, 0, 0), memory_space=pltpu.SMEM),
                  pl.BlockSpec((1, 1, TOP_K * tm), lambda i: (jnp.minimum(i + 1, nt - 1), 0, 0),
                               memory_space=pltpu.SMEM),
                  pl.BlockSpec(memory_space=pl.ANY),
                  row,
                  pl.BlockSpec((tm, TOP_K), lambda i: (i, 0)),
                  pl.BlockSpec((1, 1, d), lambda i: (i // tpb, 0, 0)),
                  vec, vec],
        out_specs=row,
        out_shape=jax.ShapeDtypeStruct((n, d), F32),
        scratch_shapes=[pltpu.VMEM((2, TOP_K, tm, d), F32), pltpu.SemaphoreType.DMA((2,))],
        compiler_params=_cparams(1),
        name="moe_combine_ln",
    )(pos3, pos3, y_rows, x, gates, g, gamma.reshape(1, d), beta.reshape(1, d))


def _route(logits, tm):
    n, e = logits.shape
    top_v, top_e = lax.top_k(logits, TOP_K)
    gates = jax.nn.softmax(top_v, axis=-1)
    e_flat = top_e.reshape(-1).astype(jnp.int32)
    onehot = (e_flat[:, None] == jnp.arange(e, dtype=jnp.int32)[None, :]).astype(jnp.int32)
    cum = jnp.cumsum(onehot, axis=0)
    rank = jnp.take_along_axis(cum, e_flat[:, None], axis=1)[:, 0] - 1
    counts = cum[-1]
    padded = (counts + tm - 1) // tm * tm
    pend = jnp.cumsum(padded)
    pstart = pend - padded
    dest = (pstart[e_flat] + rank).astype(jnp.int32)
    n_tiles = (n * TOP_K) // tm + e
    tok_flat = jnp.repeat(jnp.arange(n, dtype=jnp.int32), TOP_K)
    row_tok = jnp.zeros((n_tiles * tm,), jnp.int32).at[dest].set(tok_flat)
    n_used = (pend[-1] // tm).astype(jnp.int32)
    tidx = jnp.arange(n_tiles, dtype=jnp.int32)
    tile_e = jnp.minimum(jnp.searchsorted(pend, tidx * tm, side='right'), e - 1).astype(jnp.int32)
    tile_e = jnp.where(tidx < n_used, tile_e, tile_e[jnp.maximum(n_used - 1, 0)])
    return gates, dest.reshape(n, TOP_K), row_tok.reshape(n_tiles, 1, tm), tile_e, n_used.reshape(1)


def _moe_layer(x, hf, g2, gamma, beta, w_router, w1, w3, w2, layer_idx, *, alpha, rows_per_batch, tm=1024):
    n = hf.shape[0]
    tm = _pick(n * TOP_K, tm)
    logits = _router_logits(hf, w_router)
    gates, pos, row_tok, tile_e, n_used = _route(logits, tm)
    xs = _moe_gather(hf, row_tok, n_used, tm=tm)
    y_rows = _grouped_ffn(xs, w1, w3, w2, tile_e + layer_idx * N_EXPERTS, n_used, tm=tm)
    return _moe_combine_ln(y_rows, pos, gates, x, g2, gamma, beta, alpha=alpha, rows_per_batch=rows_per_batch)


def kernel(x, c, ada_w, ada_b, ln1_g, ln1_b, ln2_g, ln2_b, a_w_in, a_b_in, a_w_dw, a_b_dw, a_ln_g, a_ln_b, a_w_out, a_b_out, b_w_in, b_b_in, b_a_re, b_a_im, b_log_dt, b_b_re, b_b_im, b_c_re, b_c_im, b_d, b_w_glu, b_b_glu, c_w_in, c_b_in, c_w_conv, c_b_conv, c_mh_g, c_w_out, c_b_out, d_w_in, d_b_in, d_ln_g, d_ln_b, d_w_sp, d_b_sp, d_w_out, d_b_out, f_w1, f_w3, f_w2, m_router, m_w1, m_w3, m_w2):
    bsz, seq, d = x.shape
    depth = ada_w.shape[0]
    alpha = (2.0 * depth) ** 0.25
    n = bsz * seq
    mm = functools.partial(_matmul, rows_per_batch=seq)

    mods = _ada_mods(c, ada_w, ada_b)
    xf = x.reshape(n, d)
    ia = ib = ic = idd = 0
    i_dense = i_moe = 0
    for layer in range(depth):
        sh1, sc1, g1, sh2, sc2, g2 = [mods[layer, :, k * d:(k + 1) * d].reshape(bsz, 1, d) for k in range(6)]
        kind = layer % N_MIXERS
        if kind == 0:
            glu = mm(xf, a_w_in[ia], a_b_in[ia], mod=(sc1, sh1), act="glu")
            u = _conv_ln_swish(glu, a_w_dw[ia], a_b_dw[ia], a_ln_g[ia], a_ln_b[ia], bsz=bsz, seq=seq)
            y = mm(u, a_w_out[ia], a_b_out[ia])
            ia += 1
        elif kind == 1:
            u = mm(xf, b_w_in[ib], b_b_in[ib], mod=(sc1, sh1))
            tables = _s5_tables(b_a_re[ib], b_a_im[ib], b_log_dt[ib], b_b_re[ib], b_b_im[ib],
                                b_c_re[ib], b_c_im[ib], b_d[ib])
            yg = _s5_core(u, tables, bsz=bsz, seq=seq)
            y = mm(yg, b_w_glu[ib], b_b_glu[ib], act="glu")
            ib += 1
        elif kind == 2:
            heads = ML_HEADS
            dv = d // heads
            n_main = 2 * heads * (dv // 2) + 2 * heads * dv
            w_in = jnp.pad(c_w_in[ic], ((0, 0), (0, V7X_LANES - 2 * heads)))
            b_in = jnp.pad(c_b_in[ic], ((0, V7X_LANES - 2 * heads),))
            assert w_in.shape[1] == n_main + V7X_LANES
            z = mm(xf, w_in, b_in, mod=(sc1, sh1), tn=896)
            hs = _mlstm_core(z, c_w_conv[ic], c_b_conv[ic], c_mh_g[ic], bsz=bsz, seq=seq, d=d)
            y = mm(hs, c_w_out[ic], c_b_out[ic])
            ic += 1
        else:
            zz = mm(xf, d_w_in[idd], d_b_in[idd], mod=(sc1, sh1), act="gelu")
            gated = _gmlp_core(zz, d_ln_g[idd], d_ln_b[idd], d_w_sp[idd], d_b_sp[idd], seq=seq)
            y = mm(gated, d_w_out[idd], d_b_out[idd])
            idd += 1
        is_moe = layer % 2 == 1
        xf, hf = _ln_res(xf, y, g1, ln1_g[layer], ln1_b[layer], alpha=alpha, rows_per_batch=seq,
                         mod=(sc2, sh2), h_dtype=F32 if is_moe else BF16)
        if not is_moe:
            tm_ffn = _pick(n, 1024)
            tile_g = jnp.full((n // tm_ffn,), i_dense, jnp.int32)
            n_used = jnp.full((1,), n // tm_ffn, jnp.int32)
            y = _grouped_ffn(hf, f_w1, f_w3, f_w2, tile_g, n_used, tm=tm_ffn)
            xf = _ln_res(xf, y, g2, ln2_g[layer], ln2_b[layer], alpha=alpha, rows_per_batch=seq)
            i_dense += 1
        else:
            merge = lambda w: w.reshape((w.shape[0] * w.shape[1],) + w.shape[2:])
            xf = _moe_layer(xf, hf, g2, ln2_g[layer], ln2_b[layer], m_router[i_moe], merge(m_w1),
                            merge(m_w3), merge(m_w2), i_moe, alpha=alpha, rows_per_batch=seq)
            i_moe += 1
    return xf.reshape(bsz, seq, d)
```

```python
import functools
import math

import jax
import jax.numpy as jnp
from jax import lax
from jax.experimental import pallas as pl
from jax.experimental.pallas import tpu as pltpu

F32 = jnp.float32
BF16 = jnp.bfloat16

LN_EPS = 1e-5
N_MIXERS = 4
CONV_WIDTH = 31
S5_GROUP = 16
S5_STATE = 64
ML_HEADS = 8
ML_CONV = 4
GM_CHUNK = 128
GM_GROUPS = 8
N_EXPERTS = 8
TOP_K = 2

V7X_VMEM_BYTES = 64 * 1024 * 1024
V7X_LANES = 128
V7X_SUBLANES = 8
VMEM_LIMIT = V7X_VMEM_BYTES - 8 * 1024 * 1024

S5_CHUNK = 16
ML_CHUNK_ROWS = 256
CONV_HALO = 32
DMA_UNROLL = 8


def _cparams(n_axes):
    return pltpu.CompilerParams(dimension_semantics=("arbitrary",) * n_axes,
                                vmem_limit_bytes=VMEM_LIMIT)


def _pick(n, pref):
    t = min(n, pref)
    while n % t:
        t //= 2
    return t


def _sigmoid(x):
    return 1.0 / (1.0 + jnp.exp(-x))


def _silu(x):
    return x * _sigmoid(x)


def _layer_norm_rows(v, gamma, beta):
    mu = jnp.mean(v, axis=-1, keepdims=True)
    d = v - mu
    var = jnp.mean(d * d, axis=-1, keepdims=True)
    y = d * lax.rsqrt(var + LN_EPS)
    if gamma is not None:
        y = y * gamma
    if beta is not None:
        y = y + beta
    return y


def _bdot(a, b):
    return jnp.dot(a, b, preferred_element_type=F32)


def _ada_kernel(c_ref, w_ref, b_ref, o_ref):
    cond = _silu(c_ref[...]).astype(BF16)
    o_ref[0] = _bdot(cond, w_ref[0].astype(BF16)) + b_ref[0]


def _ada_mods(c, ada_w, ada_b):
    depth, d, d6 = ada_w.shape
    bsz = c.shape[0]
    tn = _pick(d6, 1024)
    return pl.pallas_call(
        _ada_kernel,
        grid=(depth, d6 // tn),
        in_specs=[pl.BlockSpec((bsz, d), lambda l, j: (0, 0)),
                  pl.BlockSpec((1, d, tn), lambda l, j: (l, 0, j)),
                  pl.BlockSpec((1, 1, tn), lambda l, j: (l, 0, j))],
        out_specs=pl.BlockSpec((1, bsz, tn), lambda l, j: (l, 0, j)),
        out_shape=jax.ShapeDtypeStruct((depth, bsz, d6), F32),
        compiler_params=_cparams(2),
        name="ada_mods",
    )(c, ada_w, ada_b.reshape(depth, 1, d6))


def _mm_kernel(*refs, n_w, has_mod, use_scratch, act):
    it = iter(refs)
    x_ref = next(it)
    sc_ref = sh_ref = None
    if has_mod:
        sc_ref, sh_ref = next(it), next(it)
    w_refs = [next(it) for _ in range(n_w)]
    b_refs = [next(it) for _ in range(n_w)]
    o_ref = next(it)
    if use_scratch:
        xb_ref = next(it)

        @pl.when(pl.program_id(1) == 0)
        def _():
            xv = x_ref[...].astype(F32)
            if has_mod:
                xv = xv * (1.0 + sc_ref[0]) + sh_ref[0]
            xb_ref[...] = xv.astype(BF16)

        xb = xb_ref[...]
    else:
        xb = x_ref[...]
    z = [_bdot(xb, w_refs[k][...].astype(BF16)) + b_refs[k][...] for k in range(n_w)]
    if act == "glu":
        out = z[0] * _sigmoid(z[1])
    elif act == "gelu":
        out = jax.nn.gelu(z[0])
    else:
        out = z[0]
    o_ref[...] = out.astype(o_ref.dtype)


def _matmul(x, w, b, *, rows_per_batch, mod=None, act=None, out_dtype=F32, tm=1024, tn=512):
    n, k = x.shape
    nw = w.shape[1]
    n_out = nw // 2 if act == "glu" else nw
    tm = _pick(rows_per_batch, tm)
    tn = _pick(n_out, tn)
    tiles_per_batch = rows_per_batch // tm
    has_mod = mod is not None
    use_scratch = has_mod or x.dtype != BF16
    n_w = 2 if act == "glu" else 1
    half = n_out // tn

    in_specs = [pl.BlockSpec((tm, k), lambda i, j: (i, 0))]
    args = [x]
    if has_mod:
        mspec = pl.BlockSpec((1, 1, k), lambda i, j: (i // tiles_per_batch, 0, 0))
        in_specs += [mspec, mspec]
        args += [mod[0], mod[1]]
    b2 = b.reshape(1, nw)
    in_specs.append(pl.BlockSpec((k, tn), lambda i, j: (0, j)))
    args.append(w)
    if n_w == 2:
        in_specs.append(pl.BlockSpec((k, tn), lambda i, j: (0, j + half)))
        args.append(w)
    in_specs.append(pl.BlockSpec((1, tn), lambda i, j: (0, j)))
    args.append(b2)
    if n_w == 2:
        in_specs.append(pl.BlockSpec((1, tn), lambda i, j: (0, j + half)))
        args.append(b2)
    scratch = [pltpu.VMEM((tm, k), BF16)] if use_scratch else []
    return pl.pallas_call(
        functools.partial(_mm_kernel, n_w=n_w, has_mod=has_mod, use_scratch=use_scratch, act=act),
        grid=(n // tm, n_out // tn),
        in_specs=in_specs,
        out_specs=pl.BlockSpec((tm, tn), lambda i, j: (i, j)),
        out_shape=jax.ShapeDtypeStruct((n, n_out), out_dtype),
        scratch_shapes=scratch,
        compiler_params=_cparams(2),
        name="matmul_" + (act or "bias"),
    )(*args)


def _ln_res_kernel(*refs, alpha, emit_h):
    if emit_h:
        x_ref, y_ref, g_ref, gam_ref, bet_ref, sc_ref, sh_ref, o_ref, h_ref = refs
    else:
        x_ref, y_ref, g_ref, gam_ref, bet_ref, o_ref = refs
    r = alpha * x_ref[...] + g_ref[0] * y_ref[...].astype(F32)
    xn = _layer_norm_rows(r, gam_ref[...], bet_ref[...])
    o_ref[...] = xn
    if emit_h:
        h_ref[...] = (xn * (1.0 + sc_ref[0]) + sh_ref[0]).astype(h_ref.dtype)


def _ln_res(x, y, g, gamma, beta, *, alpha, rows_per_batch, mod=None, h_dtype=BF16, tm=512):
    n, d = x.shape
    tm = _pick(rows_per_batch, tm)
    tpb = rows_per_batch // tm
    row = pl.BlockSpec((tm, d), lambda i: (i, 0))
    per_b = pl.BlockSpec((1, 1, d), lambda i: (i // tpb, 0, 0))
    vec = pl.BlockSpec((1, d), lambda i: (0, 0))
    emit_h = mod is not None
    in_specs = [row, row, per_b, vec, vec]
    args = [x, y, g, gamma.reshape(1, d), beta.reshape(1, d)]
    out_shape = jax.ShapeDtypeStruct((n, d), F32)
    out_specs = row
    if emit_h:
        in_specs += [per_b, per_b]
        args += [mod[0], mod[1]]
        out_shape = (out_shape, jax.ShapeDtypeStruct((n, d), h_dtype))
        out_specs = (row, row)
    return pl.pallas_call(
        functools.partial(_ln_res_kernel, alpha=alpha, emit_h=emit_h),
        grid=(n // tm,),
        in_specs=in_specs,
        out_specs=out_specs,
        out_shape=out_shape,
        compiler_params=_cparams(1),
        name="ln_res",
    )(*args)


def _proj_ln_kernel(h_ref, w_ref, b_ref, x_ref, g_ref, gam_ref, bet_ref, sc_ref, sh_ref, o_ref, h2_ref, wb_ref,
                    *, alpha):
    @pl.when(pl.program_id(0) == 0)
    def _():
        wb_ref[...] = w_ref[...].astype(BF16)

    y = _bdot(h_ref[...], wb_ref[...]) + b_ref[...]
    xn = _layer_norm_rows(alpha * x_ref[...] + g_ref[0] * y, gam_ref[...], bet_ref[...])
    o_ref[...] = xn
    h2_ref[...] = (xn * (1.0 + sc_ref[0]) + sh_ref[0]).astype(h2_ref.dtype)


def _proj_ln(h, w, b, x, g, gamma, beta, mod, *, alpha, rows_per_batch, h_dtype, tm=256):
    n, k = h.shape
    d = w.shape[1]
    tm = _pick(rows_per_batch, tm)
    tpb = rows_per_batch // tm
    per_b = pl.BlockSpec((1, 1, d), lambda i: (i // tpb, 0, 0))
    vec = pl.BlockSpec((1, d), lambda i: (0, 0))
    row = pl.BlockSpec((tm, d), lambda i: (i, 0))
    return pl.pallas_call(
        functools.partial(_proj_ln_kernel, alpha=alpha),
        grid=(n // tm,),
        in_specs=[pl.BlockSpec((tm, k), lambda i: (i, 0)),
                  pl.BlockSpec((k, d), lambda i: (0, 0), pipeline_mode=pl.Buffered(1)),
                  vec, row, per_b, vec, vec, per_b, per_b],
        out_specs=(row, row),
        out_shape=(jax.ShapeDtypeStruct((n, d), F32), jax.ShapeDtypeStruct((n, d), h_dtype)),
        scratch_shapes=[pltpu.VMEM((k, d), BF16)],
        compiler_params=_cparams(1),
        name="proj_ln",
    )(h, w, b.reshape(1, d), x, g, gamma.reshape(1, d), beta.reshape(1, d), mod[0], mod[1])


def _group_state(tg_ref, nu_ref):
    i = pl.program_id(1)
    active = i < nu_ref[0]
    changed = jnp.logical_or(i == 0, tg_ref[i] != tg_ref[jnp.maximum(i - 1, 0)])
    return active, jnp.logical_and(active, changed)


def _ffn_up_kernel(tg_ref, nu_ref, x_ref, w1_ref, w3_ref, h_ref, w1b_ref, w3b_ref):
    active, recast = _group_state(tg_ref, nu_ref)

    @pl.when(recast)
    def _():
        w1b_ref[...] = w1_ref[0].astype(BF16)
        w3b_ref[...] = w3_ref[0].astype(BF16)

    @pl.when(active)
    def _():
        x = x_ref[...]
        a = _bdot(x, w1b_ref[...])
        b = _bdot(x, w3b_ref[...])
        h_ref[...] = (_silu(a) * b).astype(h_ref.dtype)

    @pl.when(jnp.logical_not(active))
    def _():
        h_ref[...] = jnp.zeros(h_ref.shape, h_ref.dtype)


def _ffn_down_kernel(tg_ref, nu_ref, h_ref, w2_ref, o_ref, w2b_ref):
    active, recast = _group_state(tg_ref, nu_ref)

    @pl.when(recast)
    def _():
        w2b_ref[...] = w2_ref[0].astype(BF16)

    @pl.when(active)
    def _():
        o_ref[...] = _bdot(h_ref[...], w2b_ref[...])

    @pl.when(jnp.logical_not(active))
    def _():
        o_ref[...] = jnp.zeros(o_ref.shape, o_ref.dtype)


def _grouped_ffn(xb, w1, w3, w2, tile_g, n_used, *, tm, tf=512, tm_down=512, tn=512):
    r, d = xb.shape
    ff = w1.shape[2]
    tf = _pick(ff, tf)
    tn = _pick(d, tn)
    tm_down = _pick(tm, tm_down)
    n_tiles = r // tm
    h = pl.pallas_call(
        _ffn_up_kernel,
        grid_spec=pltpu.PrefetchScalarGridSpec(
            num_scalar_prefetch=2,
            grid=(ff // tf, n_tiles),
            in_specs=[pl.BlockSpec((tm, d), lambda f, i, tg, nu: (i, 0)),
                      pl.BlockSpec((1, d, tf), lambda f, i, tg, nu: (tg[i], 0, f)),
                      pl.BlockSpec((1, d, tf), lambda f, i, tg, nu: (tg[i], 0, f))],
            out_specs=pl.BlockSpec((tm, tf), lambda f, i, tg, nu: (i, f)),
            scratch_shapes=[pltpu.VMEM((d, tf), BF16), pltpu.VMEM((d, tf), BF16)]),
        out_shape=jax.ShapeDtypeStruct((r, ff), BF16),
        compiler_params=_cparams(2),
        name="ffn_up",
    )(tile_g, n_used, xb, w1, w3)
    sub = tm // tm_down
    tile_g2 = jnp.repeat(tile_g, sub)
    n_used2 = n_used * sub
    return pl.pallas_call(
        _ffn_down_kernel,
        grid_spec=pltpu.PrefetchScalarGridSpec(
            num_scalar_prefetch=2,
            grid=(d // tn, n_tiles * sub),
            in_specs=[pl.BlockSpec((tm_down, ff), lambda n, i, tg, nu: (i, 0)),
                      pl.BlockSpec((1, ff, tn), lambda n, i, tg, nu: (tg[i], 0, n))],
            out_specs=pl.BlockSpec((tm_down, tn), lambda n, i, tg, nu: (i, n)),
            scratch_shapes=[pltpu.VMEM((ff, tn), BF16)]),
        out_shape=jax.ShapeDtypeStruct((r, d), F32),
        compiler_params=_cparams(2),
        name="ffn_down",
    )(tile_g2, n_used2, h, w2)


def _conv_kernel(x_ref, w_ref, b_ref, g_ref, be_ref, o_ref, xx_ref, u_ref, *, ts, kw, rc, cc):
    i = pl.program_id(1)
    d = x_ref.shape[1]
    halo = CONV_HALO

    @pl.when(i == 0)
    def _():
        xx_ref[0:halo, :] = jnp.zeros((halo, d), F32)

    @pl.when(i > 0)
    def _():
        xx_ref[0:halo, :] = xx_ref[ts:ts + halo, :]

    xx_ref[halo:halo + ts, :] = x_ref[...]
    off = halo - (kw - 1)

    sub = V7X_SUBLANES
    for r0 in range(0, ts, rc):
        for c0 in range(0, d, cc):
            acc = jnp.broadcast_to(b_ref[:, c0:c0 + cc], (rc, cc))
            for rho in range(sub):
                taps = [j for j in range(kw) if (off + j) % sub == rho]
                if not taps:
                    continue
                base = r0 + off + taps[0]
                win = xx_ref[base:base + rc + taps[-1] - taps[0], c0:c0 + cc]
                part = None
                for j in taps:
                    o = j - taps[0]
                    term = w_ref[j:j + 1, c0:c0 + cc] * win[o:o + rc, :]
                    part = term if part is None else part + term
                acc = acc + part
            u_ref[r0:r0 + rc, c0:c0 + cc] = acc
    y = _layer_norm_rows(u_ref[...], g_ref[...], be_ref[...])
    o_ref[...] = _silu(y).astype(o_ref.dtype)


def _conv_ln_swish(glu, w_dw, b_dw, ln_g, ln_b, *, bsz, seq, ts=128):
    n, d = glu.shape
    kw = w_dw.shape[0]
    ts = _pick(seq, ts)
    nt = seq // ts
    rc = _pick(ts, 64)
    cc = _pick(d, 256)
    row = pl.BlockSpec((ts, d), lambda b, i: (b * nt + i, 0))
    vec = pl.BlockSpec((1, d), lambda b, i: (0, 0))
    return pl.pallas_call(
        functools.partial(_conv_kernel, ts=ts, kw=kw, rc=rc, cc=cc),
        grid=(bsz, nt),
        in_specs=[row, pl.BlockSpec((kw, d), lambda b, i: (0, 0)), vec, vec, vec],
        out_specs=row,
        out_shape=jax.ShapeDtypeStruct((n, d), BF16),
        scratch_shapes=[pltpu.VMEM((CONV_HALO + ts, d), F32), pltpu.VMEM((ts, d), F32)],
        compiler_params=_cparams(2),
        name="conv_ln_swish",
    )(glu, w_dw, b_dw.reshape(1, d), ln_g.reshape(1, d), ln_b.reshape(1, d))


def _iota_div(x, k):
    assert k & (k - 1) == 0
    return lax.shift_right_logical(x, k.bit_length() - 1)


def _iota_mod(x, k):
    assert k & (k - 1) == 0
    return lax.bitwise_and(x, k - 1)


def _s5_expand_tables(kq_ref, mq_ref, cq_ref, t_scr, m_scr, c_scr, *, lc, p):
    lanes = t_scr.shape[0] // lc
    ns2 = m_scr.shape[1]
    ns = ns2 // 2
    nst = ns // (lanes // p)
    iota = lambda shape, dim: lax.broadcasted_iota(jnp.int32, shape, dim)
    r, c = iota((lanes, lanes), 0), iota((lanes, lanes), 1)
    same_g = _iota_div(r, p) == _iota_div(c, p)
    e_t = (_iota_mod(iota((p, lanes), 1), p) == iota((p, lanes), 0)).astype(BF16)
    tiles = [jnp.where(same_g, _bdot(kq_ref[0, tau].astype(BF16), e_t), 0.0).astype(BF16) for tau in range(lc)]
    zero_tile = jnp.zeros((lanes, lanes), BF16)
    for s in range(lc):
        for t in range(lc):
            t_scr[s * lanes:(s + 1) * lanes, t * lanes:(t + 1) * lanes] = tiles[t - s] if t >= s else zero_tile
    r, c = iota((lanes, ns2), 0), iota((lanes, ns2), 1)
    e_m = (r == _iota_div(c, ns) * nst + _iota_mod(c, nst)).astype(BF16)
    mask_m = _iota_div(r, p) == _iota_div(_iota_mod(c, ns), nst)
    for s in range(lc):
        m_scr[s * lanes:(s + 1) * lanes, :] = jnp.where(
            mask_m, _bdot(mq_ref[0, s].astype(BF16), e_m), 0.0).astype(BF16)
    r, c = iota((ns2, lanes), 0), iota((ns2, lanes), 1)
    e_c = (c == _iota_div(r, ns) * nst + _iota_mod(r, nst)).astype(BF16)
    mask_c = _iota_div(_iota_mod(r, ns), nst) == _iota_div(c, p)
    for t in range(lc):
        c_scr[:, t * lanes:(t + 1) * lanes] = jnp.where(
            mask_c, _bdot(e_c, cq_ref[0, t].astype(BF16)), 0.0).astype(BF16)


def _s5_kernel(u_ref, kq_ref, mq_ref, cq_ref, lr_ref, li_ref, d_ref, o_ref,
               t_scr, m_scr, c_scr, a_ref, v_ref, p_ref, *, lc, nc, p):
    @pl.when(pl.program_id(1) == 0)
    def _():
        _s5_expand_tables(kq_ref, mq_ref, cq_ref, t_scr, m_scr, c_scr, lc=lc, p=p)

    lanes = u_ref.shape[2]
    for t in range(lc):
        a_ref[:, t * lanes:(t + 1) * lanes] = u_ref[0, pl.ds(t, nc, stride=lc), :]
    a = a_ref[...]
    ab = a.astype(BF16)
    v_ref[...] = _bdot(ab, m_scr[...])
    lr = lr_ref[0]
    li = li_ref[0]
    ns = lr.shape[1]

    slab = V7X_SUBLANES
    srow = lax.broadcasted_iota(jnp.int32, (slab, ns), 0)

    def step(k, carry):
        sr, si = carry
        r0 = pl.multiple_of(k * slab, slab)
        xr = v_ref[pl.ds(r0, slab), 0:ns]
        xi = v_ref[pl.ds(r0, slab), ns:2 * ns]
        prev_r = sr
        prev_i = si
        for j in range(slab):
            if j > 0:
                keep = srow < j
                prev_r = jnp.where(keep, prev_r, pltpu.roll(sr, j, 0))
                prev_i = jnp.where(keep, prev_i, pltpu.roll(si, j, 0))
                xrj = pltpu.roll(xr, slab - j, 0)
                xij = pltpu.roll(xi, slab - j, 0)
            else:
                xrj, xij = xr, xi
            sr, si = lr * sr - li * si + xrj, lr * si + li * sr + xij
        p_ref[pl.ds(r0, slab), 0:ns] = prev_r
        p_ref[pl.ds(r0, slab), ns:2 * ns] = prev_i
        return sr, si

    zero = jnp.zeros((slab, ns), F32)
    lax.fori_loop(0, nc // slab, step, (zero, zero))
    y = _bdot(ab, t_scr[...]) + _bdot(p_ref[...].astype(BF16), c_scr[...]) + d_ref[0] * a
    y = jax.nn.gelu(y)
    for t in range(lc):
        o_ref[0, pl.ds(t, nc, stride=lc), :] = y[:, t * lanes:(t + 1) * lanes]


def _s5_tables(a_re, a_im, log_dt, b_re, b_im, c_re, c_im, d_skip):
    g, n = a_re.shape
    p = b_re.shape[2]
    lc = S5_CHUNK
    ar, ai = a_re.astype(F32), a_im.astype(F32)
    dt = jnp.exp(log_dt.astype(F32))[:, None]
    decay = jnp.exp(ar * dt)
    lr, li = decay * jnp.cos(ai * dt), decay * jnp.sin(ai * dt)
    den = ar * ar + ai * ai
    zr = ((lr - 1.0) * ar + li * ai) / den
    zi = (li * ar - (lr - 1.0) * ai) / den
    br, bi = b_re.astype(F32), b_im.astype(F32)
    bbr = zr[..., None] * br - zi[..., None] * bi
    bbi = zr[..., None] * bi + zi[..., None] * br
    tau = jnp.arange(lc + 1, dtype=F32)[:, None, None]
    pdec = jnp.exp(tau * (ar * dt)[None])
    pr, pi = pdec * jnp.cos(tau * (ai * dt)[None]), pdec * jnp.sin(tau * (ai * dt)[None])
    cr, ci = c_re.astype(F32), c_im.astype(F32)
    hp = lax.Precision.HIGHEST
    lbr = pr[..., None] * bbr[None] - pi[..., None] * bbi[None]
    lbi = pr[..., None] * bbi[None] + pi[..., None] * bbr[None]
    ktau = (jnp.einsum('gpn,tgnq->tgpq', cr, lbr[:lc], precision=hp)
            - jnp.einsum('gpn,tgnq->tgpq', ci, lbi[:lc], precision=hp))
    gb = V7X_LANES // p
    nj = g // gb
    kq = ktau.reshape(lc, nj, gb, p, p).transpose(1, 0, 2, 4, 3).reshape(nj, lc, gb * p, p)
    rev = lc - 1 - jnp.arange(lc)
    mq = jnp.stack([lbr[rev], lbi[rev]], axis=2).reshape(lc, nj, gb, 2, n, p)
    mq = mq.transpose(1, 0, 2, 5, 3, 4).reshape(nj, lc, gb * p, 2 * n)
    pr1, pi1 = pr[1:], pi[1:]
    cmr = (cr[None] * pr1[:, :, None, :] - ci[None] * pi1[:, :, None, :])
    cmi = -(cr[None] * pi1[:, :, None, :] + ci[None] * pr1[:, :, None, :])
    cq = jnp.stack([cmr, cmi], axis=0).reshape(2, lc, nj, gb, p, n)
    cq = cq.transpose(2, 1, 0, 5, 3, 4).reshape(nj, lc, 2 * n, gb * p)
    lam_r = pr[lc].reshape(nj, 1, gb * n)
    lam_i = pi[lc].reshape(nj, 1, gb * n)
    dvec = jnp.broadcast_to(d_skip.astype(F32).reshape(nj, 1, 1, gb * p), (nj, 1, lc, gb * p))
    return kq, mq, cq, lam_r, lam_i, dvec.reshape(nj, 1, lc * gb * p)


def _s5_core(u, tables, *, bsz, seq):
    n_rows, d = u.shape
    kq, mq, cq, lam_r, lam_i, dvec = tables
    nj, lc, lanes, p = kq.shape
    ns = lam_r.shape[2]
    nc = seq // lc
    wk = lc * lanes
    assert lanes == V7X_LANES and nc % V7X_SUBLANES == 0 and nj * lanes == d
    tab = lambda *shape: pl.BlockSpec((1,) + shape, lambda j, b: (j,) + (0,) * len(shape))
    seq_blk = pl.BlockSpec((1, seq, lanes), lambda j, b: (b, 0, j))
    y = pl.pallas_call(
        functools.partial(_s5_kernel, lc=lc, nc=nc, p=p),
        grid=(nj, bsz),
        in_specs=[seq_blk, tab(lc, lanes, p), tab(lc, lanes, lanes), tab(lc, lanes, lanes),
                  tab(1, ns), tab(1, ns), tab(1, wk)],
        out_specs=seq_blk,
        out_shape=jax.ShapeDtypeStruct((bsz, seq, d), F32),
        scratch_shapes=[pltpu.VMEM((wk, wk), BF16), pltpu.VMEM((wk, 2 * ns), BF16), pltpu.VMEM((2 * ns, wk), BF16),
                        pltpu.VMEM((nc, wk), F32), pltpu.VMEM((nc, 2 * ns), F32), pltpu.VMEM((nc, 2 * ns), F32)],
        compiler_params=_cparams(2),
        name="s5_core",
    )(u.reshape(bsz, seq, d), kq, mq, cq, lam_r, lam_i, dvec)
    return y.reshape(n_rows, d)


def _log_sigmoid(x):
    return jnp.minimum(x, 0.0) - jnp.log(1.0 + jnp.exp(-jnp.abs(x)))


def _mlstm_kernel(q_ref, k_ref, v_ref, o_ref, g_ref, wq_ref, wk_ref, bq_ref, bk_ref, mh_ref, out_ref,
                  c_st, n_st, m_st, qx_ref, kx_ref, *, lm, dk, heads, kw):
    h = pl.program_id(1)
    c = pl.program_id(2)
    pad = V7X_SUBLANES

    @pl.when(c == 0)
    def _():
        c_st[...] = jnp.zeros(c_st.shape, F32)
        n_st[...] = jnp.zeros(n_st.shape, F32)
        m_st[...] = jnp.zeros(m_st.shape, F32)
        qx_ref[0:pad, :] = jnp.zeros((pad, dk), F32)
        kx_ref[0:pad, :] = jnp.zeros((pad, dk), F32)

    def conv_swish(x_ref, xx_ref, w_ref, b_ref):
        xx_ref[pad:pad + lm, :] = x_ref[...]
        acc = jnp.broadcast_to(b_ref[...], (lm, dk))
        for j in range(kw):
            acc = acc + w_ref[j:j + 1, :] * xx_ref[pad - (kw - 1) + j:pad - (kw - 1) + j + lm, :]
        xx_ref[0:pad, :] = xx_ref[lm:lm + pad, :]
        return _silu(acc)

    q = conv_swish(q_ref, qx_ref, wq_ref, bq_ref)
    k = conv_swish(k_ref, kx_ref, wk_ref, bk_ref) * (dk ** -0.5)
    v = v_ref[...]

    gts = g_ref[...]
    nl = gts.shape[1]
    lane = lax.broadcasted_iota(jnp.int32, (lm, nl), 1)
    lf = _log_sigmoid(gts)
    rows = lax.broadcasted_iota(jnp.int32, (lm, lm), 0)
    cols = lax.broadcasted_iota(jnp.int32, (lm, lm), 1)
    causal = rows >= cols
    tri = causal.astype(BF16)
    lf_hi = lf.astype(BF16)
    r1 = lf - lf_hi.astype(F32)
    lf_mid = r1.astype(BF16)
    lf_lo = (r1 - lf_mid.astype(F32)).astype(BF16)
    bcum = _bdot(tri, lf_hi) + _bdot(tri, lf_mid) + _bdot(tri, lf_lo)

    def col_of(mat, idx):
        return jnp.sum(jnp.where(lane == idx, mat, 0.0), axis=1, keepdims=True)

    def row_of(mat_t, idx):
        sub = lax.broadcasted_iota(jnp.int32, mat_t.shape, 0)
        return jnp.sum(jnp.where(sub == idx, mat_t, 0.0), axis=0, keepdims=True)

    b_col = col_of(bcum, heads + h)
    li_col = col_of(gts, h)
    b_row = row_of(bcum.T, heads + h)
    li_row = row_of(gts.T, h)

    m_prev = m_st[...]
    dmat = jnp.where(causal, b_col - b_row + li_row, -jnp.inf)
    inter = b_col + m_prev
    m_j = jnp.maximum(inter, jnp.max(dmat, axis=1, keepdims=True))
    w_intra = jnp.exp(dmat - m_j)
    w_inter = jnp.exp(inter - m_j)
    qb, kb, vb = q.astype(BF16), k.astype(BF16), v.astype(BF16)
    sc = lax.dot_general(qb, kb, (((1,), (1,)), ((), ())), preferred_element_type=F32) * w_intra
    num = w_inter * _bdot(qb, c_st[...].astype(BF16)) + _bdot(sc.astype(BF16), vb)
    den = w_inter * jnp.sum(q * n_st[...], axis=1, keepdims=True) + jnp.sum(sc, axis=1, keepdims=True)
    hb = num / jnp.maximum(jnp.abs(den), jnp.exp(-m_j))

    bl = b_col[lm - 1:lm, :]
    gsum = bl - b_col + li_col
    m_new = jnp.maximum(bl + m_prev, jnp.max(gsum, axis=0, keepdims=True))
    wc = jnp.exp(bl + m_prev - m_new)
    wk = jnp.exp(gsum - m_new)
    kwt = wk * k
    c_st[...] = wc * c_st[...] + _bdot(kwt.T.astype(BF16), vb)
    n_st[...] = wc * n_st[...] + jnp.sum(kwt, axis=0, keepdims=True)
    m_st[...] = m_new

    hs = _sigmoid(o_ref[...]) * hb
    out_ref[...] = (_layer_norm_rows(hs, None, None) * mh_ref[...]).astype(out_ref.dtype)


def _mlstm_core(z, w_conv, b_conv, mh_g, *, bsz, seq, d):
    n_rows = z.shape[0]
    heads = ML_HEADS
    dv = d // heads
    dk = dv // 2
    lm = _pick(seq, ML_CHUNK_ROWS)
    nc = seq // lm
    kw = w_conv.shape[0]
    gate_blk = (2 * heads * dk + 2 * heads * dv) // V7X_LANES
    vblk0 = (2 * heads * dk) // dv
    row = lambda b, h, c: b * nc + c
    in_specs = [
        pl.BlockSpec((lm, dk), lambda b, h, c: (row(b, h, c), h)),
        pl.BlockSpec((lm, dk), lambda b, h, c: (row(b, h, c), heads + h)),
        pl.BlockSpec((lm, dv), lambda b, h, c: (row(b, h, c), vblk0 + h)),
        pl.BlockSpec((lm, dv), lambda b, h, c: (row(b, h, c), vblk0 + heads + h)),
        pl.BlockSpec((lm, V7X_LANES), lambda b, h, c: (row(b, h, c), gate_blk)),
        pl.BlockSpec((kw, dk), lambda b, h, c: (0, h)),
        pl.BlockSpec((kw, dk), lambda b, h, c: (0, heads + h)),
        pl.BlockSpec((1, dk), lambda b, h, c: (0, h)),
        pl.BlockSpec((1, dk), lambda b, h, c: (0, heads + h)),
        pl.BlockSpec((1, dv), lambda b, h, c: (0, h)),
    ]
    b2 = b_conv.reshape(1, -1)
    return pl.pallas_call(
        functools.partial(_mlstm_kernel, lm=lm, dk=dk, heads=heads, kw=kw),
        grid=(bsz, heads, nc),
        in_specs=in_specs,
        out_specs=pl.BlockSpec((lm, dv), lambda b, h, c: (row(b, h, c), h)),
        out_shape=jax.ShapeDtypeStruct((n_rows, d), BF16),
        scratch_shapes=[pltpu.VMEM((dk, dv), F32), pltpu.VMEM((1, dk), F32), pltpu.VMEM((1, 1), F32),
                        pltpu.VMEM((lm + V7X_SUBLANES, dk), F32), pltpu.VMEM((lm + V7X_SUBLANES, dk), F32)],
        compiler_params=_cparams(3),
        name="mlstm_core",
    )(z, z, z, z, z, w_conv, w_conv, b2, b2, mh_g.reshape(1, d))


def _gmlp_kernel(u_ref, v_ref, g_ref, be_ref, w_ref, bsp_ref, o_ref, *, lg, groups, dg, nsub):
    rows = lax.broadcasted_iota(jnp.int32, (lg, lg), 0)
    cols = lax.broadcasted_iota(jnp.int32, (lg, lg), 1)
    keep = rows >= cols
    wm = [jnp.where(keep, w_ref[g], 0.0).astype(BF16) for g in range(groups)]
    for s in range(nsub):
        r = slice(s * lg, (s + 1) * lg)
        vn = _layer_norm_rows(v_ref[r, :], g_ref[...], be_ref[...]).astype(BF16)
        for g in range(groups):
            cs = slice(g * dg, (g + 1) * dg)
            sv = _bdot(wm[g], vn[:, cs]) + bsp_ref[:, g:g + 1]
            o_ref[r, cs] = (u_ref[r, cs] * sv).astype(o_ref.dtype)


def _gmlp_core(zz, ln_g, ln_b, w_sp, b_sp, *, seq, tm=512):
    n, d2 = zz.shape
    d = d2 // 2
    groups, lg, _ = w_sp.shape
    tm = max(lg, _pick(seq, tm))
    return pl.pallas_call(
        functools.partial(_gmlp_kernel, lg=lg, groups=groups, dg=d // groups, nsub=tm // lg),
        grid=(n // tm,),
        in_specs=[pl.BlockSpec((tm, d), lambda i: (i, 0)),
                  pl.BlockSpec((tm, d), lambda i: (i, 1)),
                  pl.BlockSpec((1, d), lambda i: (0, 0)),
                  pl.BlockSpec((1, d), lambda i: (0, 0)),
                  pl.BlockSpec((groups, lg, lg), lambda i: (0, 0, 0)),
                  pl.BlockSpec((lg, groups), lambda i: (0, 0))],
        out_specs=pl.BlockSpec((tm, d), lambda i: (i, 0)),
        out_shape=jax.ShapeDtypeStruct((n, d), BF16),
        compiler_params=_cparams(1),
        name="gmlp_spatial",
    )(zz, zz, ln_g.reshape(1, d), ln_b.reshape(1, d), w_sp, b_sp.T)


def _router_kernel(h_ref, w_ref, o_ref):
    h = h_ref[...]
    w = w_ref[...]
    hh = h.astype(BF16)
    hl = (h - hh.astype(F32)).astype(BF16)
    wh = w.astype(BF16)
    wl = (w - wh.astype(F32)).astype(BF16)
    o_ref[...] = _bdot(hh, wh) + _bdot(hl, wh) + _bdot(hh, wl)


def _router_logits(hf, w_router, *, tm=1024):
    n, d = hf.shape
    e = w_router.shape[1]
    wpad = jnp.pad(w_router, ((0, 0), (0, V7X_LANES - e)))
    tm = _pick(n, tm)
    out = pl.pallas_call(
        _router_kernel,
        grid=(n // tm,),
        in_specs=[pl.BlockSpec((tm, d), lambda i: (i, 0)),
                  pl.BlockSpec((d, V7X_LANES), lambda i: (0, 0))],
        out_specs=pl.BlockSpec((tm, V7X_LANES), lambda i: (i, 0)),
        out_shape=jax.ShapeDtypeStruct((n, V7X_LANES), F32),
        compiler_params=_cparams(1),
        name="moe_router",
    )(hf, wpad)
    return out[:, :e]


def _moe_gather_kernel(nu_ref, tok_ref, tokn_ref, hf_hbm, o_ref, xg_ref, sem, *, tm):
    i = pl.program_id(0)
    nu = nu_ref[0]
    slot = lax.rem(i, 2)

    def row_copy(tref, r, s):
        return pltpu.make_async_copy(hf_hbm.at[pl.ds(tref[0, 0, r], 1), :], xg_ref.at[s, pl.ds(r, 1), :], sem.at[s])

    def start_tile(tref, s):
        def issue(r, carry):
            row_copy(tref, r, s).start()
            return carry

        lax.fori_loop(0, tm, issue, 0, unroll=DMA_UNROLL)

    @pl.when(jnp.logical_and(i == 0, nu > 0))
    def _():
        start_tile(tok_ref, 0)

    @pl.when(i + 1 < nu)
    def _():
        start_tile(tokn_ref, 1 - slot)

    @pl.when(i < nu)
    def _():
        def wait(r, carry):
            row_copy(tok_ref, r, slot).wait()
            return carry

        lax.fori_loop(0, tm, wait, 0, unroll=DMA_UNROLL)
        o_ref[...] = xg_ref[slot].astype(o_ref.dtype)

    @pl.when(i >= nu)
    def _():
        o_ref[...] = jnp.zeros(o_ref.shape, o_ref.dtype)


def _moe_gather(hf, row_tok, n_used, *, tm):
    n, d = hf.shape
    n_tiles = row_tok.shape[0]
    tok_spec = lambda fn: pl.BlockSpec((1, 1, tm), fn, memory_space=pltpu.SMEM)
    return pl.pallas_call(
        functools.partial(_moe_gather_kernel, tm=tm),
        grid_spec=pltpu.PrefetchScalarGridSpec(
            num_scalar_prefetch=1,
            grid=(n_tiles,),
            in_specs=[tok_spec(lambda i, nu: (i, 0, 0)),
                      tok_spec(lambda i, nu: (jnp.minimum(i + 1, n_tiles - 1), 0, 0)),
                      pl.BlockSpec(memory_space=pl.ANY)],
            out_specs=pl.BlockSpec((tm, d), lambda i, nu: (i, 0)),
            scratch_shapes=[pltpu.VMEM((2, tm, d), hf.dtype), pltpu.SemaphoreType.DMA((2,))]),
        out_shape=jax.ShapeDtypeStruct((n_tiles * tm, d), BF16),
        compiler_params=_cparams(1),
        name="moe_gather",
    )(n_used, row_tok, row_tok, hf)


def _moe_combine_kernel(pos_ref, posn_ref, y_hbm, x_ref, gate_ref, g_ref, gam_ref, bet_ref, o_ref, yb_ref, sem,
                        *, tm, alpha):
    i = pl.program_id(0)
    nt = pl.num_programs(0)
    slot = lax.rem(i, 2)

    def row_copy(pref, k, r, s):
        return pltpu.make_async_copy(y_hbm.at[pl.ds(pref[0, 0, k * tm + r], 1), :],
                                     yb_ref.at[s, k, pl.ds(r, 1), :], sem.at[s])

    def start_tile(pref, s):
        def issue(r, carry):
            for k in range(TOP_K):
                row_copy(pref, k, r, s).start()
            return carry

        lax.fori_loop(0, tm, issue, 0, unroll=DMA_UNROLL)

    @pl.when(i == 0)
    def _():
        start_tile(pos_ref, 0)

    @pl.when(i + 1 < nt)
    def _():
        start_tile(posn_ref, 1 - slot)

    def wait(r, carry):
        for k in range(TOP_K):
            row_copy(pos_ref, k, r, slot).wait()
        return carry

    lax.fori_loop(0, tm, wait, 0, unroll=DMA_UNROLL)
    gate = gate_ref[...]
    y = gate[:, 0:1] * yb_ref[slot, 0]
    for k in range(1, TOP_K):
        y = y + gate[:, k:k + 1] * yb_ref[slot, k]
    r = alpha * x_ref[...] + g_ref[0] * y
    o_ref[...] = _layer_norm_rows(r, gam_ref[...], bet_ref[...])


def _moe_combine_ln(y_rows, pos, gates, x, g, gamma, beta, *, alpha, rows_per_batch, tm=256):
    n, d = x.shape
    tm = _pick(rows_per_batch, tm)
    tpb = rows_per_batch // tm
    nt = n // tm
    pos3 = pos.reshape(nt, tm, TOP_K).transpose(0, 2, 1).reshape(nt, 1, TOP_K * tm)
    row = pl.BlockSpec((tm, d), lambda i: (i, 0))
    vec = pl.BlockSpec((1, d), lambda i: (0, 0))
    return pl.pallas_call(
        functools.partial(_moe_combine_kernel, tm=tm, alpha=alpha),
        grid=(nt,),
        in_specs=[pl.BlockSpec((1, 1, TOP_K * tm), lambda i: (i, 0, 0), memory_space=pltpu.SMEM),
                  pl.BlockSpec((1, 1, TOP_K * tm), lambda i: (jnp.minimum(i + 1, nt - 1), 0, 0),
                               memory_space=pltpu.SMEM),
                  pl.BlockSpec(memory_space=pl.ANY),
                  row,
                  pl.BlockSpec((tm, TOP_K), lambda i: (i, 0)),
                  pl.BlockSpec((1, 1, d), lambda i: (i // tpb, 0, 0)),
                  vec, vec],
        out_specs=row,
        out_shape=jax.ShapeDtypeStruct((n, d), F32),
        scratch_shapes=[pltpu.VMEM((2, TOP_K, tm, d), F32), pltpu.SemaphoreType.DMA((2,))],
        compiler_params=_cparams(1),
        name="moe_combine_ln",
    )(pos3, pos3, y_rows, x, gates, g, gamma.reshape(1, d), beta.reshape(1, d))


def _route(logits, tm):
    n, e = logits.shape
    top_v, top_e = lax.top_k(logits, TOP_K)
    gates = jax.nn.softmax(top_v, axis=-1)
    e_flat = top_e.reshape(-1).astype(jnp.int32)
    onehot = (e_flat[:, None] == jnp.arange(e, dtype=jnp.int32)[None, :]).astype(jnp.int32)
    cum = jnp.cumsum(onehot, axis=0)
    rank = jnp.take_along_axis(cum, e_flat[:, None], axis=1)[:, 0] - 1
    counts = cum[-1]
    padded = (counts + tm - 1) // tm * tm
    pend = jnp.cumsum(padded)
    pstart = pend - padded
    dest = (pstart[e_flat] + rank).astype(jnp.int32)
    n_tiles = (n * TOP_K) // tm + e
    tok_flat = jnp.repeat(jnp.arange(n, dtype=jnp.int32), TOP_K)
    row_tok = jnp.zeros((n_tiles * tm,), jnp.int32).at[dest].set(tok_flat)
    n_used = (pend[-1] // tm).astype(jnp.int32)
    tidx = jnp.arange(n_tiles, dtype=jnp.int32)
    tile_e = jnp.minimum(jnp.searchsorted(pend, tidx * tm, side='right'), e - 1).astype(jnp.int32)
    tile_e = jnp.where(tidx < n_used, tile_e, tile_e[jnp.maximum(n_used - 1, 0)])
    return gates, dest.reshape(n, TOP_K), row_tok.reshape(n_tiles, 1, tm), tile_e, n_used.reshape(1)


def _moe_layer(x, hf, g2, gamma, beta, w_router, w1, w3, w2, layer_idx, *, alpha, rows_per_batch, tm=1024):
    n = hf.shape[0]
    tm = _pick(n * TOP_K, tm)
    logits = _router_logits(hf, w_router)
    gates, pos, row_tok, tile_e, n_used = _route(logits, tm)
    xs = _moe_gather(hf, row_tok, n_used, tm=tm)
    y_rows = _grouped_ffn(xs, w1, w3, w2, tile_e + layer_idx * N_EXPERTS, n_used, tm=tm)
    return _moe_combine_ln(y_rows, pos, gates, x, g2, gamma, beta, alpha=alpha, rows_per_batch=rows_per_batch)


def kernel(x, c, ada_w, ada_b, ln1_g, ln1_b, ln2_g, ln2_b, a_w_in, a_b_in, a_w_dw, a_b_dw, a_ln_g, a_ln_b, a_w_out, a_b_out, b_w_in, b_b_in, b_a_re, b_a_im, b_log_dt, b_b_re, b_b_im, b_c_re, b_c_im, b_d, b_w_glu, b_b_glu, c_w_in, c_b_in, c_w_conv, c_b_conv, c_mh_g, c_w_out, c_b_out, d_w_in, d_b_in, d_ln_g, d_ln_b, d_w_sp, d_b_sp, d_w_out, d_b_out, f_w1, f_w3, f_w2, m_router, m_w1, m_w3, m_w2):
    bsz, seq, d = x.shape
    depth = ada_w.shape[0]
    alpha = (2.0 * depth) ** 0.25
    n = bsz * seq
    mm = functools.partial(_matmul, rows_per_batch=seq)

    mods = _ada_mods(c, ada_w, ada_b)
    xf = x.reshape(n, d)
    ia = ib = ic = idd = 0
    i_dense = i_moe = 0
    for layer in range(depth):
        sh1, sc1, g1, sh2, sc2, g2 = [mods[layer, :, k * d:(k + 1) * d].reshape(bsz, 1, d) for k in range(6)]
        kind = layer % N_MIXERS
        is_moe = layer % 2 == 1
        h_dtype = F32 if is_moe else BF16
        proj_ln = functools.partial(_proj_ln, x=xf, g=g1, gamma=ln1_g[layer], beta=ln1_b[layer], mod=(sc2, sh2),
                                    alpha=alpha, rows_per_batch=seq, h_dtype=h_dtype)
        if kind == 0:
            glu = mm(xf, a_w_in[ia], a_b_in[ia], mod=(sc1, sh1), act="glu")
            u = _conv_ln_swish(glu, a_w_dw[ia], a_b_dw[ia], a_ln_g[ia], a_ln_b[ia], bsz=bsz, seq=seq)
            xf, hf = proj_ln(u, a_w_out[ia], a_b_out[ia])
            ia += 1
        elif kind == 1:
            u = mm(xf, b_w_in[ib], b_b_in[ib], mod=(sc1, sh1))
            tables = _s5_tables(b_a_re[ib], b_a_im[ib], b_log_dt[ib], b_b_re[ib], b_b_im[ib],
                                b_c_re[ib], b_c_im[ib], b_d[ib])
            yg = _s5_core(u, tables, bsz=bsz, seq=seq)
            y = mm(yg, b_w_glu[ib], b_b_glu[ib], act="glu")
            xf, hf = _ln_res(xf, y, g1, ln1_g[layer], ln1_b[layer], alpha=alpha, rows_per_batch=seq,
                             mod=(sc2, sh2), h_dtype=h_dtype)
            ib += 1
        elif kind == 2:
            heads = ML_HEADS
            dv = d // heads
            n_main = 2 * heads * (dv // 2) + 2 * heads * dv
            w_in = jnp.pad(c_w_in[ic], ((0, 0), (0, V7X_LANES - 2 * heads)))
            b_in = jnp.pad(c_b_in[ic], ((0, V7X_LANES - 2 * heads),))
            assert w_in.shape[1] == n_main + V7X_LANES
            z = mm(xf, w_in, b_in, mod=(sc1, sh1), tn=896)
            hs = _mlstm_core(z, c_w_conv[ic], c_b_conv[ic], c_mh_g[ic], bsz=bsz, seq=seq, d=d)
            xf, hf = proj_ln(hs, c_w_out[ic], c_b_out[ic])
            ic += 1
        else:
            zz = mm(xf, d_w_in[idd], d_b_in[idd], mod=(sc1, sh1), act="gelu")
            gated = _gmlp_core(zz, d_ln_g[idd], d_ln_b[idd], d_w_sp[idd], d_b_sp[idd], seq=seq)
            xf, hf = proj_ln(gated, d_w_out[idd], d_b_out[idd])
            idd += 1
        if not is_moe:
            tm_ffn = _pick(n, 1024)
            tile_g = jnp.full((n // tm_ffn,), i_dense, jnp.int32)
            n_used = jnp.full((1,), n // tm_ffn, jnp.int32)
            y = _grouped_ffn(hf, f_w1, f_w3, f_w2, tile_g, n_used, tm=tm_ffn)
            xf = _ln_res(xf, y, g2, ln2_g[layer], ln2_b[layer], alpha=alpha, rows_per_batch=seq)
            i_dense += 1
        else:
            merge = lambda w: w.reshape((w.shape[0] * w.shape[1],) + w.shape[2:])
            xf = _moe_layer(xf, hf, g2, ln2_g[layer], ln2_b[layer], m_router[i_moe], merge(m_w1),
                            merge(m_w3), merge(m_w2), i_moe, alpha=alpha, rows_per_batch=seq)
            i_moe += 1
    return xf.reshape(bsz, seq, d)
```

```python
import functools
import math

import jax
import jax.numpy as jnp
from jax import lax
from jax.experimental import pallas as pl
from jax.experimental.pallas import tpu as pltpu

F32 = jnp.float32
BF16 = jnp.bfloat16

LN_EPS = 1e-5
N_MIXERS = 4
CONV_WIDTH = 31
S5_GROUP = 16
S5_STATE = 64
ML_HEADS = 8
ML_CONV = 4
GM_CHUNK = 128
GM_GROUPS = 8
N_EXPERTS = 8
TOP_K = 2

V7X_VMEM_BYTES = 64 * 1024 * 1024
V7X_LANES = 128
V7X_SUBLANES = 8
VMEM_LIMIT = V7X_VMEM_BYTES - 8 * 1024 * 1024

S5_CHUNK = 16
ML_CHUNK_ROWS = 256
CONV_HALO = 32
DMA_UNROLL = 8


def _cparams(n_axes):
    return pltpu.CompilerParams(dimension_semantics=("arbitrary",) * n_axes,
                                vmem_limit_bytes=VMEM_LIMIT)


def _pick(n, pref):
    t = min(n, pref)
    while n % t:
        t //= 2
    return t


def _sigmoid(x):
    return 1.0 / (1.0 + jnp.exp(-x))


def _silu(x):
    return x * _sigmoid(x)


def _layer_norm_rows(v, gamma, beta):
    mu = jnp.mean(v, axis=-1, keepdims=True)
    d = v - mu
    var = jnp.mean(d * d, axis=-1, keepdims=True)
    y = d * lax.rsqrt(var + LN_EPS)
    if gamma is not None:
        y = y * gamma
    if beta is not None:
        y = y + beta
    return y


def _bdot(a, b):
    return jnp.dot(a, b, preferred_element_type=F32)


def _ada_kernel(c_ref, w_ref, b_ref, o_ref):
    cond = _silu(c_ref[...]).astype(BF16)
    o_ref[0] = _bdot(cond, w_ref[0].astype(BF16)) + b_ref[0]


def _ada_mods(c, ada_w, ada_b):
    depth, d, d6 = ada_w.shape
    bsz = c.shape[0]
    tn = _pick(d6, 1024)
    return pl.pallas_call(
        _ada_kernel,
        grid=(depth, d6 // tn),
        in_specs=[pl.BlockSpec((bsz, d), lambda l, j: (0, 0)),
                  pl.BlockSpec((1, d, tn), lambda l, j: (l, 0, j)),
                  pl.BlockSpec((1, 1, tn), lambda l, j: (l, 0, j))],
        out_specs=pl.BlockSpec((1, bsz, tn), lambda l, j: (l, 0, j)),
        out_shape=jax.ShapeDtypeStruct((depth, bsz, d6), F32),
        compiler_params=_cparams(2),
        name="ada_mods",
    )(c, ada_w, ada_b.reshape(depth, 1, d6))


def _mm_kernel(*refs, n_w, has_mod, use_scratch, act):
    it = iter(refs)
    x_ref = next(it)
    sc_ref = sh_ref = None
    if has_mod:
        sc_ref, sh_ref = next(it), next(it)
    w_refs = [next(it) for _ in range(n_w)]
    b_refs = [next(it) for _ in range(n_w)]
    o_ref = next(it)
    if use_scratch:
        xb_ref = next(it)

        @pl.when(pl.program_id(1) == 0)
        def _():
            xv = x_ref[...].astype(F32)
            if has_mod:
                xv = xv * (1.0 + sc_ref[0]) + sh_ref[0]
            xb_ref[...] = xv.astype(BF16)

        xb = xb_ref[...]
    else:
        xb = x_ref[...]
    z = [_bdot(xb, w_refs[k][...].astype(BF16)) + b_refs[k][...] for k in range(n_w)]
    if act == "glu":
        out = z[0] * _sigmoid(z[1])
    elif act == "gelu":
        out = jax.nn.gelu(z[0])
    else:
        out = z[0]
    o_ref[...] = out.astype(o_ref.dtype)


def _matmul(x, w, b, *, rows_per_batch, mod=None, act=None, out_dtype=F32, tm=1024, tn=512):
    n, k = x.shape
    nw = w.shape[1]
    n_out = nw // 2 if act == "glu" else nw
    tm = _pick(rows_per_batch, tm)
    tn = _pick(n_out, tn)
    tiles_per_batch = rows_per_batch // tm
    has_mod = mod is not None
    use_scratch = has_mod or x.dtype != BF16
    n_w = 2 if act == "glu" else 1
    half = n_out // tn

    in_specs = [pl.BlockSpec((tm, k), lambda i, j: (i, 0))]
    args = [x]
    if has_mod:
        mspec = pl.BlockSpec((1, 1, k), lambda i, j: (i // tiles_per_batch, 0, 0))
        in_specs += [mspec, mspec]
        args += [mod[0], mod[1]]
    b2 = b.reshape(1, nw)
    in_specs.append(pl.BlockSpec((k, tn), lambda i, j: (0, j)))
    args.append(w)
    if n_w == 2:
        in_specs.append(pl.BlockSpec((k, tn), lambda i, j: (0, j + half)))
        args.append(w)
    in_specs.append(pl.BlockSpec((1, tn), lambda i, j: (0, j)))
    args.append(b2)
    if n_w == 2:
        in_specs.append(pl.BlockSpec((1, tn), lambda i, j: (0, j + half)))
        args.append(b2)
    scratch = [pltpu.VMEM((tm, k), BF16)] if use_scratch else []
    return pl.pallas_call(
        functools.partial(_mm_kernel, n_w=n_w, has_mod=has_mod, use_scratch=use_scratch, act=act),
        grid=(n // tm, n_out // tn),
        in_specs=in_specs,
        out_specs=pl.BlockSpec((tm, tn), lambda i, j: (i, j)),
        out_shape=jax.ShapeDtypeStruct((n, n_out), out_dtype),
        scratch_shapes=scratch,
        compiler_params=_cparams(2),
        name="matmul_" + (act or "bias"),
    )(*args)


def _ln_res_kernel(*refs, alpha, emit_h):
    if emit_h:
        x_ref, y_ref, g_ref, gam_ref, bet_ref, sc_ref, sh_ref, o_ref, h_ref = refs
    else:
        x_ref, y_ref, g_ref, gam_ref, bet_ref, o_ref = refs
    r = alpha * x_ref[...] + g_ref[0] * y_ref[...].astype(F32)
    xn = _layer_norm_rows(r, gam_ref[...], bet_ref[...])
    o_ref[...] = xn
    if emit_h:
        h_ref[...] = (xn * (1.0 + sc_ref[0]) + sh_ref[0]).astype(h_ref.dtype)


def _ln_res(x, y, g, gamma, beta, *, alpha, rows_per_batch, mod=None, h_dtype=BF16, tm=512):
    n, d = x.shape
    tm = _pick(rows_per_batch, tm)
    tpb = rows_per_batch // tm
    row = pl.BlockSpec((tm, d), lambda i: (i, 0))
    per_b = pl.BlockSpec((1, 1, d), lambda i: (i // tpb, 0, 0))
    vec = pl.BlockSpec((1, d), lambda i: (0, 0))
    emit_h = mod is not None
    in_specs = [row, row, per_b, vec, vec]
    args = [x, y, g, gamma.reshape(1, d), beta.reshape(1, d)]
    out_shape = jax.ShapeDtypeStruct((n, d), F32)
    out_specs = row
    if emit_h:
        in_specs += [per_b, per_b]
        args += [mod[0], mod[1]]
        out_shape = (out_shape, jax.ShapeDtypeStruct((n, d), h_dtype))
        out_specs = (row, row)
    return pl.pallas_call(
        functools.partial(_ln_res_kernel, alpha=alpha, emit_h=emit_h),
        grid=(n // tm,),
        in_specs=in_specs,
        out_specs=out_specs,
        out_shape=out_shape,
        compiler_params=_cparams(1),
        name="ln_res",
    )(*args)


def _proj_ln_kernel(h_ref, w_ref, b_ref, x_ref, g_ref, gam_ref, bet_ref, sc_ref, sh_ref, o_ref, h2_ref, wb_ref,
                    *, alpha):
    @pl.when(pl.program_id(0) == 0)
    def _():
        wb_ref[...] = w_ref[...].astype(BF16)

    tm = h_ref.shape[0]
    for r0 in range(0, tm, tm // 2):
        rows = slice(r0, r0 + tm // 2)
        y = _bdot(h_ref[rows, :], wb_ref[...]) + b_ref[...]
        xn = _layer_norm_rows(alpha * x_ref[rows, :] + g_ref[0] * y, gam_ref[...], bet_ref[...])
        o_ref[rows, :] = xn
        h2_ref[rows, :] = (xn * (1.0 + sc_ref[0]) + sh_ref[0]).astype(h2_ref.dtype)


def _proj_ln(h, w, b, x, g, gamma, beta, mod, *, alpha, rows_per_batch, h_dtype, tm=256):
    n, k = h.shape
    d = w.shape[1]
    tm = _pick(rows_per_batch, tm)
    tpb = rows_per_batch // tm
    per_b = pl.BlockSpec((1, 1, d), lambda i: (i // tpb, 0, 0))
    vec = pl.BlockSpec((1, d), lambda i: (0, 0))
    row = pl.BlockSpec((tm, d), lambda i: (i, 0))
    return pl.pallas_call(
        functools.partial(_proj_ln_kernel, alpha=alpha),
        grid=(n // tm,),
        in_specs=[pl.BlockSpec((tm, k), lambda i: (i, 0)),
                  pl.BlockSpec((k, d), lambda i: (0, 0), pipeline_mode=pl.Buffered(1)),
                  vec, row, per_b, vec, vec, per_b, per_b],
        out_specs=(row, row),
        out_shape=(jax.ShapeDtypeStruct((n, d), F32), jax.ShapeDtypeStruct((n, d), h_dtype)),
        scratch_shapes=[pltpu.VMEM((k, d), BF16)],
        compiler_params=_cparams(1),
        name="proj_ln",
    )(h, w, b.reshape(1, d), x, g, gamma.reshape(1, d), beta.reshape(1, d), mod[0], mod[1])


def _group_state(tg_ref, nu_ref):
    i = pl.program_id(1)
    active = i < nu_ref[0]
    changed = jnp.logical_or(i == 0, tg_ref[i] != tg_ref[jnp.maximum(i - 1, 0)])
    return active, jnp.logical_and(active, changed)


def _ffn_up_kernel(tg_ref, nu_ref, x_ref, w1_ref, w3_ref, h_ref, w1b_ref, w3b_ref):
    active, recast = _group_state(tg_ref, nu_ref)

    @pl.when(recast)
    def _():
        w1b_ref[...] = w1_ref[0].astype(BF16)
        w3b_ref[...] = w3_ref[0].astype(BF16)

    @pl.when(active)
    def _():
        x = x_ref[...]
        a = _bdot(x, w1b_ref[...])
        b = _bdot(x, w3b_ref[...])
        h_ref[...] = (_silu(a) * b).astype(h_ref.dtype)

    @pl.when(jnp.logical_not(active))
    def _():
        h_ref[...] = jnp.zeros(h_ref.shape, h_ref.dtype)


def _ffn_down_kernel(tg_ref, nu_ref, h_ref, w2_ref, o_ref, w2b_ref):
    active, recast = _group_state(tg_ref, nu_ref)

    @pl.when(recast)
    def _():
        w2b_ref[...] = w2_ref[0].astype(BF16)

    @pl.when(active)
    def _():
        o_ref[...] = _bdot(h_ref[...], w2b_ref[...])

    @pl.when(jnp.logical_not(active))
    def _():
        o_ref[...] = jnp.zeros(o_ref.shape, o_ref.dtype)


def _grouped_ffn(xb, w1, w3, w2, tile_g, n_used, *, tm, tf=512, tm_down=512, tn=512):
    r, d = xb.shape
    ff = w1.shape[2]
    tf = _pick(ff, tf)
    tn = _pick(d, tn)
    tm_down = _pick(tm, tm_down)
    n_tiles = r // tm
    h = pl.pallas_call(
        _ffn_up_kernel,
        grid_spec=pltpu.PrefetchScalarGridSpec(
            num_scalar_prefetch=2,
            grid=(ff // tf, n_tiles),
            in_specs=[pl.BlockSpec((tm, d), lambda f, i, tg, nu: (i, 0)),
                      pl.BlockSpec((1, d, tf), lambda f, i, tg, nu: (tg[i], 0, f)),
                      pl.BlockSpec((1, d, tf), lambda f, i, tg, nu: (tg[i], 0, f))],
            out_specs=pl.BlockSpec((tm, tf), lambda f, i, tg, nu: (i, f)),
            scratch_shapes=[pltpu.VMEM((d, tf), BF16), pltpu.VMEM((d, tf), BF16)]),
        out_shape=jax.ShapeDtypeStruct((r, ff), BF16),
        compiler_params=_cparams(2),
        name="ffn_up",
    )(tile_g, n_used, xb, w1, w3)
    sub = tm // tm_down
    tile_g2 = jnp.repeat(tile_g, sub)
    n_used2 = n_used * sub
    return pl.pallas_call(
        _ffn_down_kernel,
        grid_spec=pltpu.PrefetchScalarGridSpec(
            num_scalar_prefetch=2,
            grid=(d // tn, n_tiles * sub),
            in_specs=[pl.BlockSpec((tm_down, ff), lambda n, i, tg, nu: (i, 0)),
                      pl.BlockSpec((1, ff, tn), lambda n, i, tg, nu: (tg[i], 0, n))],
            out_specs=pl.BlockSpec((tm_down, tn), lambda n, i, tg, nu: (i, n)),
            scratch_shapes=[pltpu.VMEM((ff, tn), BF16)]),
        out_shape=jax.ShapeDtypeStruct((r, d), F32),
        compiler_params=_cparams(2),
        name="ffn_down",
    )(tile_g2, n_used2, h, w2)


def _conv_kernel(x_ref, w_ref, b_ref, g_ref, be_ref, o_ref, xx_ref, u_ref, *, ts, kw, rc, cc):
    i = pl.program_id(1)
    d = x_ref.shape[1]
    halo = CONV_HALO

    @pl.when(i == 0)
    def _():
        xx_ref[0:halo, :] = jnp.zeros((halo, d), F32)

    @pl.when(i > 0)
    def _():
        xx_ref[0:halo, :] = xx_ref[ts:ts + halo, :]

    xx_ref[halo:halo + ts, :] = x_ref[...]
    off = halo - (kw - 1)

    sub = V7X_SUBLANES
    for r0 in range(0, ts, rc):
        for c0 in range(0, d, cc):
            acc = jnp.broadcast_to(b_ref[:, c0:c0 + cc], (rc, cc))
            for rho in range(sub):
                taps = [j for j in range(kw) if (off + j) % sub == rho]
                if not taps:
                    continue
                base = r0 + off + taps[0]
                win = xx_ref[base:base + rc + taps[-1] - taps[0], c0:c0 + cc]
                part = None
                for j in taps:
                    o = j - taps[0]
                    term = w_ref[j:j + 1, c0:c0 + cc] * win[o:o + rc, :]
                    part = term if part is None else part + term
                acc = acc + part
            u_ref[r0:r0 + rc, c0:c0 + cc] = acc
    y = _layer_norm_rows(u_ref[...], g_ref[...], be_ref[...])
    o_ref[...] = _silu(y).astype(o_ref.dtype)


def _conv_ln_swish(glu, w_dw, b_dw, ln_g, ln_b, *, bsz, seq, ts=128):
    n, d = glu.shape
    kw = w_dw.shape[0]
    ts = _pick(seq, ts)
    nt = seq // ts
    rc = _pick(ts, 64)
    cc = _pick(d, 256)
    row = pl.BlockSpec((ts, d), lambda b, i: (b * nt + i, 0))
    vec = pl.BlockSpec((1, d), lambda b, i: (0, 0))
    return pl.pallas_call(
        functools.partial(_conv_kernel, ts=ts, kw=kw, rc=rc, cc=cc),
        grid=(bsz, nt),
        in_specs=[row, pl.BlockSpec((kw, d), lambda b, i: (0, 0)), vec, vec, vec],
        out_specs=row,
        out_shape=jax.ShapeDtypeStruct((n, d), BF16),
        scratch_shapes=[pltpu.VMEM((CONV_HALO + ts, d), F32), pltpu.VMEM((ts, d), F32)],
        compiler_params=_cparams(2),
        name="conv_ln_swish",
    )(glu, w_dw, b_dw.reshape(1, d), ln_g.reshape(1, d), ln_b.reshape(1, d))


def _iota_div(x, k):
    assert k & (k - 1) == 0
    return lax.shift_right_logical(x, k.bit_length() - 1)


def _iota_mod(x, k):
    assert k & (k - 1) == 0
    return lax.bitwise_and(x, k - 1)


def _s5_expand_tables(kq_ref, mq_ref, cq_ref, t_scr, m_scr, c_scr, *, lc, p):
    lanes = t_scr.shape[0] // lc
    ns2 = m_scr.shape[1]
    ns = ns2 // 2
    nst = ns // (lanes // p)
    iota = lambda shape, dim: lax.broadcasted_iota(jnp.int32, shape, dim)
    r, c = iota((lanes, lanes), 0), iota((lanes, lanes), 1)
    same_g = _iota_div(r, p) == _iota_div(c, p)
    e_t = (_iota_mod(iota((p, lanes), 1), p) == iota((p, lanes), 0)).astype(BF16)
    tiles = [jnp.where(same_g, _bdot(kq_ref[0, tau].astype(BF16), e_t), 0.0).astype(BF16) for tau in range(lc)]
    zero_tile = jnp.zeros((lanes, lanes), BF16)
    for s in range(lc):
        for t in range(lc):
            t_scr[s * lanes:(s + 1) * lanes, t * lanes:(t + 1) * lanes] = tiles[t - s] if t >= s else zero_tile
    r, c = iota((lanes, ns2), 0), iota((lanes, ns2), 1)
    e_m = (r == _iota_div(c, ns) * nst + _iota_mod(c, nst)).astype(BF16)
    mask_m = _iota_div(r, p) == _iota_div(_iota_mod(c, ns), nst)
    for s in range(lc):
        m_scr[s * lanes:(s + 1) * lanes, :] = jnp.where(
            mask_m, _bdot(mq_ref[0, s].astype(BF16), e_m), 0.0).astype(BF16)
    r, c = iota((ns2, lanes), 0), iota((ns2, lanes), 1)
    e_c = (c == _iota_div(r, ns) * nst + _iota_mod(r, nst)).astype(BF16)
    mask_c = _iota_div(_iota_mod(r, ns), nst) == _iota_div(c, p)
    for t in range(lc):
        c_scr[:, t * lanes:(t + 1) * lanes] = jnp.where(
            mask_c, _bdot(e_c, cq_ref[0, t].astype(BF16)), 0.0).astype(BF16)


def _s5_kernel(u_ref, kq_ref, mq_ref, cq_ref, lr_ref, li_ref, d_ref, o_ref,
               t_scr, m_scr, c_scr, a_ref, v_ref, p_ref, *, lc, nc, p):
    @pl.when(pl.program_id(1) == 0)
    def _():
        _s5_expand_tables(kq_ref, mq_ref, cq_ref, t_scr, m_scr, c_scr, lc=lc, p=p)

    lanes = u_ref.shape[2]
    for t in range(lc):
        a_ref[:, t * lanes:(t + 1) * lanes] = u_ref[0, pl.ds(t, nc, stride=lc), :]
    a = a_ref[...]
    ab = a.astype(BF16)
    v_ref[...] = _bdot(ab, m_scr[...])
    lr = lr_ref[0]
    li = li_ref[0]
    ns = lr.shape[1]

    slab = V7X_SUBLANES
    srow = lax.broadcasted_iota(jnp.int32, (slab, ns), 0)

    def step(k, carry):
        sr, si = carry
        r0 = pl.multiple_of(k * slab, slab)
        xr = v_ref[pl.ds(r0, slab), 0:ns]
        xi = v_ref[pl.ds(r0, slab), ns:2 * ns]
        prev_r = sr
        prev_i = si
        for j in range(slab):
            if j > 0:
                keep = srow < j
                prev_r = jnp.where(keep, prev_r, pltpu.roll(sr, j, 0))
                prev_i = jnp.where(keep, prev_i, pltpu.roll(si, j, 0))
                xrj = pltpu.roll(xr, slab - j, 0)
                xij = pltpu.roll(xi, slab - j, 0)
            else:
                xrj, xij = xr, xi
            sr, si = lr * sr - li * si + xrj, lr * si + li * sr + xij
        p_ref[pl.ds(r0, slab), 0:ns] = prev_r
        p_ref[pl.ds(r0, slab), ns:2 * ns] = prev_i
        return sr, si

    zero = jnp.zeros((slab, ns), F32)
    lax.fori_loop(0, nc // slab, step, (zero, zero))
    y = _bdot(ab, t_scr[...]) + _bdot(p_ref[...].astype(BF16), c_scr[...]) + d_ref[0] * a
    y = jax.nn.gelu(y)
    for t in range(lc):
        o_ref[0, pl.ds(t, nc, stride=lc), :] = y[:, t * lanes:(t + 1) * lanes]


def _s5_tables(a_re, a_im, log_dt, b_re, b_im, c_re, c_im, d_skip):
    g, n = a_re.shape
    p = b_re.shape[2]
    lc = S5_CHUNK
    ar, ai = a_re.astype(F32), a_im.astype(F32)
    dt = jnp.exp(log_dt.astype(F32))[:, None]
    decay = jnp.exp(ar * dt)
    lr, li = decay * jnp.cos(ai * dt), decay * jnp.sin(ai * dt)
    den = ar * ar + ai * ai
    zr = ((lr - 1.0) * ar + li * ai) / den
    zi = (li * ar - (lr - 1.0) * ai) / den
    br, bi = b_re.astype(F32), b_im.astype(F32)
    bbr = zr[..., None] * br - zi[..., None] * bi
    bbi = zr[..., None] * bi + zi[..., None] * br
    tau = jnp.arange(lc + 1, dtype=F32)[:, None, None]
    pdec = jnp.exp(tau * (ar * dt)[None])
    pr, pi = pdec * jnp.cos(tau * (ai * dt)[None]), pdec * jnp.sin(tau * (ai * dt)[None])
    cr, ci = c_re.astype(F32), c_im.astype(F32)
    hp = lax.Precision.HIGHEST
    lbr = pr[..., None] * bbr[None] - pi[..., None] * bbi[None]
    lbi = pr[..., None] * bbi[None] + pi[..., None] * bbr[None]
    ktau = (jnp.einsum('gpn,tgnq->tgpq', cr, lbr[:lc], precision=hp)
            - jnp.einsum('gpn,tgnq->tgpq', ci, lbi[:lc], precision=hp))
    gb = V7X_LANES // p
    nj = g // gb
    kq = ktau.reshape(lc, nj, gb, p, p).transpose(1, 0, 2, 4, 3).reshape(nj, lc, gb * p, p)
    rev = lc - 1 - jnp.arange(lc)
    mq = jnp.stack([lbr[rev], lbi[rev]], axis=2).reshape(lc, nj, gb, 2, n, p)
    mq = mq.transpose(1, 0, 2, 5, 3, 4).reshape(nj, lc, gb * p, 2 * n)
    pr1, pi1 = pr[1:], pi[1:]
    cmr = (cr[None] * pr1[:, :, None, :] - ci[None] * pi1[:, :, None, :])
    cmi = -(cr[None] * pi1[:, :, None, :] + ci[None] * pr1[:, :, None, :])
    cq = jnp.stack([cmr, cmi], axis=0).reshape(2, lc, nj, gb, p, n)
    cq = cq.transpose(2, 1, 0, 5, 3, 4).reshape(nj, lc, 2 * n, gb * p)
    lam_r = pr[lc].reshape(nj, 1, gb * n)
    lam_i = pi[lc].reshape(nj, 1, gb * n)
    dvec = jnp.broadcast_to(d_skip.astype(F32).reshape(nj, 1, 1, gb * p), (nj, 1, lc, gb * p))
    return kq, mq, cq, lam_r, lam_i, dvec.reshape(nj, 1, lc * gb * p)


def _s5_core(u, tables, *, bsz, seq):
    n_rows, d = u.shape
    kq, mq, cq, lam_r, lam_i, dvec = tables
    nj, lc, lanes, p = kq.shape
    ns = lam_r.shape[2]
    nc = seq // lc
    wk = lc * lanes
    assert lanes == V7X_LANES and nc % V7X_SUBLANES == 0 and nj * lanes == d
    tab = lambda *shape: pl.BlockSpec((1,) + shape, lambda j, b: (j,) + (0,) * len(shape))
    seq_blk = pl.BlockSpec((1, seq, lanes), lambda j, b: (b, 0, j))
    y = pl.pallas_call(
        functools.partial(_s5_kernel, lc=lc, nc=nc, p=p),
        grid=(nj, bsz),
        in_specs=[seq_blk, tab(lc, lanes, p), tab(lc, lanes, lanes), tab(lc, lanes, lanes),
                  tab(1, ns), tab(1, ns), tab(1, wk)],
        out_specs=seq_blk,
        out_shape=jax.ShapeDtypeStruct((bsz, seq, d), F32),
        scratch_shapes=[pltpu.VMEM((wk, wk), BF16), pltpu.VMEM((wk, 2 * ns), BF16), pltpu.VMEM((2 * ns, wk), BF16),
                        pltpu.VMEM((nc, wk), F32), pltpu.VMEM((nc, 2 * ns), F32), pltpu.VMEM((nc, 2 * ns), F32)],
        compiler_params=_cparams(2),
        name="s5_core",
    )(u.reshape(bsz, seq, d), kq, mq, cq, lam_r, lam_i, dvec)
    return y.reshape(n_rows, d)


def _log_sigmoid(x):
    return jnp.minimum(x, 0.0) - jnp.log(1.0 + jnp.exp(-jnp.abs(x)))


def _mlstm_kernel(q_ref, k_ref, v_ref, o_ref, g_ref, cw_ref, cb_ref, mh_ref, out_ref,
                  c_st, n_st, m_st, qx_ref, kx_ref, *, lm, dk, heads, kw):
    c = pl.program_id(1)
    pad = V7X_SUBLANES
    hdk = heads * dk
    dv = v_ref.shape[1] // heads

    @pl.when(c == 0)
    def _():
        c_st[...] = jnp.zeros(c_st.shape, F32)
        n_st[...] = jnp.zeros(n_st.shape, F32)
        m_st[...] = jnp.zeros(m_st.shape, F32)
        qx_ref[0:pad, :] = jnp.zeros((pad, hdk), F32)
        kx_ref[0:pad, :] = jnp.zeros((pad, hdk), F32)

    def conv_swish(x_ref, xx_ref, c0):
        xx_ref[pad:pad + lm, :] = x_ref[...]
        acc = jnp.broadcast_to(cb_ref[:, c0:c0 + hdk], (lm, hdk))
        for j in range(kw):
            acc = acc + cw_ref[j:j + 1, c0:c0 + hdk] * xx_ref[pad - (kw - 1) + j:pad - (kw - 1) + j + lm, :]
        xx_ref[0:pad, :] = xx_ref[lm:lm + pad, :]
        return _silu(acc)

    q_all = conv_swish(q_ref, qx_ref, 0)
    k_all = conv_swish(k_ref, kx_ref, hdk) * (dk ** -0.5)

    gts = g_ref[...]
    lf = _log_sigmoid(gts)
    rows = lax.broadcasted_iota(jnp.int32, (lm, lm), 0)
    cols = lax.broadcasted_iota(jnp.int32, (lm, lm), 1)
    causal = rows >= cols
    tri = causal.astype(BF16)
    lf_hi = lf.astype(BF16)
    r1 = lf - lf_hi.astype(F32)
    lf_mid = r1.astype(BF16)
    lf_lo = (r1 - lf_mid.astype(F32)).astype(BF16)
    bcum = _bdot(tri, lf_hi) + _bdot(tri, lf_mid) + _bdot(tri, lf_lo)

    bcum_t = bcum.T
    gts_t = gts.T
    for h in range(heads):
        q = q_all[:, h * dk:(h + 1) * dk]
        k = k_all[:, h * dk:(h + 1) * dk]
        vcols = slice(h * dv, (h + 1) * dv)
        b_col = bcum[:, heads + h:heads + h + 1]
        li_col = gts[:, h:h + 1]
        b_row = bcum_t[heads + h:heads + h + 1, :]
        li_row = gts_t[h:h + 1, :]

        m_prev = m_st[h]
        dmat = jnp.where(causal, b_col - b_row + li_row, -jnp.inf)
        inter = b_col + m_prev
        m_j = jnp.maximum(inter, jnp.max(dmat, axis=1, keepdims=True))
        w_intra = jnp.exp(dmat - m_j)
        w_inter = jnp.exp(inter - m_j)
        qb, kb, vb = q.astype(BF16), k.astype(BF16), v_ref[:, vcols].astype(BF16)
        sc = lax.dot_general(qb, kb, (((1,), (1,)), ((), ())), preferred_element_type=F32) * w_intra
        num = w_inter * _bdot(qb, c_st[h].astype(BF16)) + _bdot(sc.astype(BF16), vb)
        den = w_inter * jnp.sum(q * n_st[h], axis=1, keepdims=True) + jnp.sum(sc, axis=1, keepdims=True)
        hb = num / jnp.maximum(jnp.abs(den), jnp.exp(-m_j))

        bl = b_col[lm - 1:lm, :]
        gsum = bl - b_col + li_col
        m_new = jnp.maximum(bl + m_prev, jnp.max(gsum, axis=0, keepdims=True))
        wc = jnp.exp(bl + m_prev - m_new)
        wk = jnp.exp(gsum - m_new)
        kwt = wk * k
        c_st[h] = wc * c_st[h] + _bdot(kwt.T.astype(BF16), vb)
        n_st[h] = wc * n_st[h] + jnp.sum(kwt, axis=0, keepdims=True)
        m_st[h] = m_new

        hs = _sigmoid(o_ref[:, vcols]) * hb
        out_ref[:, vcols] = (_layer_norm_rows(hs, None, None) * mh_ref[:, vcols]).astype(out_ref.dtype)


def _mlstm_core(z, w_conv, b_conv, mh_g, *, bsz, seq, d):
    n_rows = z.shape[0]
    heads = ML_HEADS
    dv = d // heads
    dk = dv // 2
    lm = _pick(seq, ML_CHUNK_ROWS)
    nc = seq // lm
    kw = w_conv.shape[0]
    hdk = heads * dk
    hdv = heads * dv
    assert 2 * hdk == hdv
    gate_blk = (2 * hdk + 2 * hdv) // V7X_LANES
    row = lambda b, c: b * nc + c
    in_specs = [
        pl.BlockSpec((lm, hdk), lambda b, c: (row(b, c), 0)),
        pl.BlockSpec((lm, hdk), lambda b, c: (row(b, c), 1)),
        pl.BlockSpec((lm, hdv), lambda b, c: (row(b, c), 1)),
        pl.BlockSpec((lm, hdv), lambda b, c: (row(b, c), 2)),
        pl.BlockSpec((lm, V7X_LANES), lambda b, c: (row(b, c), gate_blk)),
        pl.BlockSpec((kw, 2 * hdk), lambda b, c: (0, 0)),
        pl.BlockSpec((1, 2 * hdk), lambda b, c: (0, 0)),
        pl.BlockSpec((1, d), lambda b, c: (0, 0)),
    ]
    return pl.pallas_call(
        functools.partial(_mlstm_kernel, lm=lm, dk=dk, heads=heads, kw=kw),
        grid=(bsz, nc),
        in_specs=in_specs,
        out_specs=pl.BlockSpec((lm, d), lambda b, c: (row(b, c), 0)),
        out_shape=jax.ShapeDtypeStruct((n_rows, d), BF16),
        scratch_shapes=[pltpu.VMEM((heads, dk, dv), F32), pltpu.VMEM((heads, 1, dk), F32),
                        pltpu.VMEM((heads, 1, 1), F32),
                        pltpu.VMEM((lm + V7X_SUBLANES, hdk), F32), pltpu.VMEM((lm + V7X_SUBLANES, hdk), F32)],
        compiler_params=_cparams(2),
        name="mlstm_core",
    )(z, z, z, z, z, w_conv, b_conv.reshape(1, -1), mh_g.reshape(1, d))


def _gmlp_kernel(u_ref, v_ref, g_ref, be_ref, w_ref, bsp_ref, o_ref, *, lg, groups, dg, nsub):
    rows = lax.broadcasted_iota(jnp.int32, (lg, lg), 0)
    cols = lax.broadcasted_iota(jnp.int32, (lg, lg), 1)
    keep = rows >= cols
    wm = [jnp.where(keep, w_ref[g], 0.0).astype(BF16) for g in range(groups)]
    for s in range(nsub):
        r = slice(s * lg, (s + 1) * lg)
        vn = _layer_norm_rows(v_ref[r, :], g_ref[...], be_ref[...]).astype(BF16)
        for g in range(groups):
            cs = slice(g * dg, (g + 1) * dg)
            sv = _bdot(wm[g], vn[:, cs]) + bsp_ref[:, g:g + 1]
            o_ref[r, cs] = (u_ref[r, cs] * sv).astype(o_ref.dtype)


def _gmlp_core(zz, ln_g, ln_b, w_sp, b_sp, *, seq, tm=512):
    n, d2 = zz.shape
    d = d2 // 2
    groups, lg, _ = w_sp.shape
    tm = max(lg, _pick(seq, tm))
    return pl.pallas_call(
        functools.partial(_gmlp_kernel, lg=lg, groups=groups, dg=d // groups, nsub=tm // lg),
        grid=(n // tm,),
        in_specs=[pl.BlockSpec((tm, d), lambda i: (i, 0)),
                  pl.BlockSpec((tm, d), lambda i: (i, 1)),
                  pl.BlockSpec((1, d), lambda i: (0, 0)),
                  pl.BlockSpec((1, d), lambda i: (0, 0)),
                  pl.BlockSpec((groups, lg, lg), lambda i: (0, 0, 0)),
                  pl.BlockSpec((lg, groups), lambda i: (0, 0))],
        out_specs=pl.BlockSpec((tm, d), lambda i: (i, 0)),
        out_shape=jax.ShapeDtypeStruct((n, d), BF16),
        compiler_params=_cparams(1),
        name="gmlp_spatial",
    )(zz, zz, ln_g.reshape(1, d), ln_b.reshape(1, d), w_sp, b_sp.T)


def _router_kernel(h_ref, w_ref, o_ref):
    h = h_ref[...]
    w = w_ref[...]
    hh = h.astype(BF16)
    hl = (h - hh.astype(F32)).astype(BF16)
    wh = w.astype(BF16)
    wl = (w - wh.astype(F32)).astype(BF16)
    o_ref[...] = _bdot(hh, wh) + _bdot(hl, wh) + _bdot(hh, wl)


def _router_logits(hf, w_router, *, tm=1024):
    n, d = hf.shape
    e = w_router.shape[1]
    wpad = jnp.pad(w_router, ((0, 0), (0, V7X_LANES - e)))
    tm = _pick(n, tm)
    out = pl.pallas_call(
        _router_kernel,
        grid=(n // tm,),
        in_specs=[pl.BlockSpec((tm, d), lambda i: (i, 0)),
                  pl.BlockSpec((d, V7X_LANES), lambda i: (0, 0))],
        out_specs=pl.BlockSpec((tm, V7X_LANES), lambda i: (i, 0)),
        out_shape=jax.ShapeDtypeStruct((n, V7X_LANES), F32),
        compiler_params=_cparams(1),
        name="moe_router",
    )(hf, wpad)
    return out[:, :e]


def _moe_gather_kernel(nu_ref, tok_ref, tokn_ref, hf_hbm, o_ref, xg_ref, sem, *, tm):
    i = pl.program_id(0)
    nu = nu_ref[0]
    slot = lax.rem(i, 2)

    def row_copy(tref, r, s):
        return pltpu.make_async_copy(hf_hbm.at[pl.ds(tref[0, 0, r], 1), :], xg_ref.at[s, pl.ds(r, 1), :], sem.at[s])

    def start_tile(tref, s):
        def issue(r2, carry):
            for q in range(2):
                row_copy(tref, 2 * r2 + q, s).start(priority=q)
            return carry

        lax.fori_loop(0, tm // 2, issue, 0, unroll=DMA_UNROLL // 2)

    @pl.when(jnp.logical_and(i == 0, nu > 0))
    def _():
        start_tile(tok_ref, 0)

    @pl.when(i + 1 < nu)
    def _():
        start_tile(tokn_ref, 1 - slot)

    @pl.when(i < nu)
    def _():
        def wait(r, carry):
            row_copy(tok_ref, r, slot).wait()
            return carry

        lax.fori_loop(0, tm, wait, 0, unroll=DMA_UNROLL)
        o_ref[...] = xg_ref[slot].astype(o_ref.dtype)

    @pl.when(i >= nu)
    def _():
        o_ref[...] = jnp.zeros(o_ref.shape, o_ref.dtype)


def _moe_gather(hf, row_tok, n_used, *, tm):
    n, d = hf.shape
    n_tiles = row_tok.shape[0]
    tok_spec = lambda fn: pl.BlockSpec((1, 1, tm), fn, memory_space=pltpu.SMEM)
    return pl.pallas_call(
        functools.partial(_moe_gather_kernel, tm=tm),
        grid_spec=pltpu.PrefetchScalarGridSpec(
            num_scalar_prefetch=1,
            grid=(n_tiles,),
            in_specs=[tok_spec(lambda i, nu: (i, 0, 0)),
                      tok_spec(lambda i, nu: (jnp.minimum(i + 1, n_tiles - 1), 0, 0)),
                      pl.BlockSpec(memory_space=pl.ANY)],
            out_specs=pl.BlockSpec((tm, d), lambda i, nu: (i, 0)),
            scratch_shapes=[pltpu.VMEM((2, tm, d), hf.dtype), pltpu.SemaphoreType.DMA((2,))]),
        out_shape=jax.ShapeDtypeStruct((n_tiles * tm, d), BF16),
        compiler_params=_cparams(1),
        name="moe_gather",
    )(n_used, row_tok, row_tok, hf)


def _moe_combine_kernel(pos_ref, posn_ref, y_hbm, x_ref, gate_ref, g_ref, gam_ref, bet_ref, o_ref, yb_ref, sem,
                        *, tm, alpha):
    i = pl.program_id(0)
    nt = pl.num_programs(0)
    slot = lax.rem(i, 2)

    def row_copy(pref, k, r, s):
        return pltpu.make_async_copy(y_hbm.at[pl.ds(pref[0, 0, k * tm + r], 1), :],
                                     yb_ref.at[s, k, pl.ds(r, 1), :], sem.at[s])

    def start_tile(pref, s):
        def issue(r, carry):
            for k in range(TOP_K):
                row_copy(pref, k, r, s).start(priority=k % 2)
            return carry

        lax.fori_loop(0, tm, issue, 0, unroll=DMA_UNROLL)

    @pl.when(i == 0)
    def _():
        start_tile(pos_ref, 0)

    @pl.when(i + 1 < nt)
    def _():
        start_tile(posn_ref, 1 - slot)

    def wait(r, carry):
        for k in range(TOP_K):
            row_copy(pos_ref, k, r, slot).wait()
        return carry

    lax.fori_loop(0, tm, wait, 0, unroll=DMA_UNROLL)
    gate = gate_ref[...]
    y = gate[:, 0:1] * yb_ref[slot, 0]
    for k in range(1, TOP_K):
        y = y + gate[:, k:k + 1] * yb_ref[slot, k]
    r = alpha * x_ref[...] + g_ref[0] * y
    o_ref[...] = _layer_norm_rows(r, gam_ref[...], bet_ref[...])


def _moe_combine_ln(y_rows, pos, gates, x, g, gamma, beta, *, alpha, rows_per_batch, tm=256):
    n, d = x.shape
    tm = _pick(rows_per_batch, tm)
    tpb = rows_per_batch // tm
    nt = n // tm
    pos3 = pos.reshape(nt, tm, TOP_K).transpose(0, 2, 1).reshape(nt, 1, TOP_K * tm)
    row = pl.BlockSpec((tm, d), lambda i: (i, 0))
    vec = pl.BlockSpec((1, d), lambda i: (0, 0))
    return pl.pallas_call(
        functools.partial(_moe_combine_kernel, tm=tm, alpha=alpha),
        grid=(nt,),
        in_specs=[pl.BlockSpec((1, 1, TOP_K * tm), lambda i: (i, 0, 0), memory_space=pltpu.SMEM),
                  pl.BlockSpec((1, 1, TOP_K * tm), lambda i: (jnp.minimum(i + 1, nt - 1), 0, 0),
                               memory_space=pltpu.SMEM),
                  pl.BlockSpec(memory_space=pl.ANY),
                  row,
                  pl.BlockSpec((tm, TOP_K), lambda i: (i, 0)),
                  pl.BlockSpec((1, 1, d), lambda i: (i // tpb, 0, 0)),
                  vec, vec],
        out_specs=row,
        out_shape=jax.ShapeDtypeStruct((n, d), F32),
        scratch_shapes=[pltpu.VMEM((2, TOP_K, tm, d), F32), pltpu.SemaphoreType.DMA((2,))],
        compiler_params=_cparams(1),
        name="moe_combine_ln",
    )(pos3, pos3, y_rows, x, gates, g, gamma.reshape(1, d), beta.reshape(1, d))


def _route(logits, tm):
    n, e = logits.shape
    top_v, top_e = lax.top_k(logits, TOP_K)
    gates = jax.nn.softmax(top_v, axis=-1)
    e_flat = top_e.reshape(-1).astype(jnp.int32)
    onehot = (e_flat[:, None] == jnp.arange(e, dtype=jnp.int32)[None, :]).astype(jnp.int32)
    cum = jnp.cumsum(onehot, axis=0)
    rank = jnp.take_along_axis(cum, e_flat[:, None], axis=1)[:, 0] - 1
    counts = cum[-1]
    padded = (counts + tm - 1) // tm * tm
    pend = jnp.cumsum(padded)
    pstart = pend - padded
    dest = (pstart[e_flat] + rank).astype(jnp.int32)
    n_tiles = (n * TOP_K) // tm + e
    tok_flat = jnp.repeat(jnp.arange(n, dtype=jnp.int32), TOP_K)
    row_tok = jnp.zeros((n_tiles * tm,), jnp.int32).at[dest].set(tok_flat)
    n_used = (pend[-1] // tm).astype(jnp.int32)
    tidx = jnp.arange(n_tiles, dtype=jnp.int32)
    tile_e = jnp.minimum(jnp.searchsorted(pend, tidx * tm, side='right'), e - 1).astype(jnp.int32)
    tile_e = jnp.where(tidx < n_used, tile_e, tile_e[jnp.maximum(n_used - 1, 0)])
    return gates, dest.reshape(n, TOP_K), row_tok.reshape(n_tiles, 1, tm), tile_e, n_used.reshape(1)


def _moe_layer(x, hf, g2, gamma, beta, w_router, w1, w3, w2, layer_idx, *, alpha, rows_per_batch, tm=1024):
    n = hf.shape[0]
    tm = _pick(n * TOP_K, tm)
    logits = _router_logits(hf, w_router)
    gates, pos, row_tok, tile_e, n_used = _route(logits, tm)
    xs = _moe_gather(hf, row_tok, n_used, tm=tm)
    y_rows = _grouped_ffn(xs, w1, w3, w2, tile_e + layer_idx * N_EXPERTS, n_used, tm=tm)
    return _moe_combine_ln(y_rows, pos, gates, x, g2, gamma, beta, alpha=alpha, rows_per_batch=rows_per_batch)


def kernel(x, c, ada_w, ada_b, ln1_g, ln1_b, ln2_g, ln2_b, a_w_in, a_b_in, a_w_dw, a_b_dw, a_ln_g, a_ln_b, a_w_out, a_b_out, b_w_in, b_b_in, b_a_re, b_a_im, b_log_dt, b_b_re, b_b_im, b_c_re, b_c_im, b_d, b_w_glu, b_b_glu, c_w_in, c_b_in, c_w_conv, c_b_conv, c_mh_g, c_w_out, c_b_out, d_w_in, d_b_in, d_ln_g, d_ln_b, d_w_sp, d_b_sp, d_w_out, d_b_out, f_w1, f_w3, f_w2, m_router, m_w1, m_w3, m_w2):
    bsz, seq, d = x.shape
    depth = ada_w.shape[0]
    alpha = (2.0 * depth) ** 0.25
    n = bsz * seq
    mm = functools.partial(_matmul, rows_per_batch=seq)

    mods = _ada_mods(c, ada_w, ada_b)
    xf = x.reshape(n, d)
    ia = ib = ic = idd = 0
    i_dense = i_moe = 0
    for layer in range(depth):
        sh1, sc1, g1, sh2, sc2, g2 = [mods[layer, :, k * d:(k + 1) * d].reshape(bsz, 1, d) for k in range(6)]
        kind = layer % N_MIXERS
        is_moe = layer % 2 == 1
        h_dtype = F32 if is_moe else BF16
        proj_ln = functools.partial(_proj_ln, x=xf, g=g1, gamma=ln1_g[layer], beta=ln1_b[layer], mod=(sc2, sh2),
                                    alpha=alpha, rows_per_batch=seq, h_dtype=h_dtype)
        if kind == 0:
            glu = mm(xf, a_w_in[ia], a_b_in[ia], mod=(sc1, sh1), act="glu")
            u = _conv_ln_swish(glu, a_w_dw[ia], a_b_dw[ia], a_ln_g[ia], a_ln_b[ia], bsz=bsz, seq=seq)
            xf, hf = proj_ln(u, a_w_out[ia], a_b_out[ia])
            ia += 1
        elif kind == 1:
            u = mm(xf, b_w_in[ib], b_b_in[ib], mod=(sc1, sh1))
            tables = _s5_tables(b_a_re[ib], b_a_im[ib], b_log_dt[ib], b_b_re[ib], b_b_im[ib],
                                b_c_re[ib], b_c_im[ib], b_d[ib])
            yg = _s5_core(u, tables, bsz=bsz, seq=seq)
            y = mm(yg, b_w_glu[ib], b_b_glu[ib], act="glu")
            xf, hf = _ln_res(xf, y, g1, ln1_g[layer], ln1_b[layer], alpha=alpha, rows_per_batch=seq,
                             mod=(sc2, sh2), h_dtype=h_dtype)
            ib += 1
        elif kind == 2:
            heads = ML_HEADS
            dv = d // heads
            n_main = 2 * heads * (dv // 2) + 2 * heads * dv
            w_in = jnp.pad(c_w_in[ic], ((0, 0), (0, V7X_LANES - 2 * heads)))
            b_in = jnp.pad(c_b_in[ic], ((0, V7X_LANES - 2 * heads),))
            assert w_in.shape[1] == n_main + V7X_LANES
            z = mm(xf, w_in, b_in, mod=(sc1, sh1), tn=896)
            hs = _mlstm_core(z, c_w_conv[ic], c_b_conv[ic], c_mh_g[ic], bsz=bsz, seq=seq, d=d)
            xf, hf = proj_ln(hs, c_w_out[ic], c_b_out[ic])
            ic += 1
        else:
            zz = mm(xf, d_w_in[idd], d_b_in[idd], mod=(sc1, sh1), act="gelu")
            gated = _gmlp_core(zz, d_ln_g[idd], d_ln_b[idd], d_w_sp[idd], d_b_sp[idd], seq=seq)
            xf, hf = proj_ln(gated, d_w_out[idd], d_b_out[idd])
            idd += 1
        if not is_moe:
            tm_ffn = _pick(n, 1024)
            tile_g = jnp.full((n // tm_ffn,), i_dense, jnp.int32)
            n_used = jnp.full((1,), n // tm_ffn, jnp.int32)
            y = _grouped_ffn(hf, f_w1, f_w3, f_w2, tile_g, n_used, tm=tm_ffn)
            xf = _ln_res(xf, y, g2, ln2_g[layer], ln2_b[layer], alpha=alpha, rows_per_batch=seq)
            i_dense += 1
        else:
            merge = lambda w: w.reshape((w.shape[0] * w.shape[1],) + w.shape[2:])
            xf = _moe_layer(xf, hf, g2, ln2_g[layer], ln2_b[layer], m_router[i_moe], merge(m_w1),
                            merge(m_w3), merge(m_w2), i_moe, alpha=alpha, rows_per_batch=seq)
            i_moe += 1
    return xf.reshape(bsz, seq, d)
```

```python
import functools
import math

import jax
import jax.numpy as jnp
from jax import lax
from jax.experimental import pallas as pl
from jax.experimental.pallas import tpu as pltpu

F32 = jnp.float32
BF16 = jnp.bfloat16

LN_EPS = 1e-5
N_MIXERS = 4
CONV_WIDTH = 31
S5_GROUP = 16
S5_STATE = 64
ML_HEADS = 8
ML_CONV = 4
GM_CHUNK = 128
GM_GROUPS = 8
N_EXPERTS = 8
TOP_K = 2

V7X_VMEM_BYTES = 64 * 1024 * 1024
V7X_LANES = 128
V7X_SUBLANES = 8
VMEM_LIMIT = V7X_VMEM_BYTES - 8 * 1024 * 1024

S5_CHUNK = 16
ML_CHUNK_ROWS = 256
CONV_HALO = 32
DMA_UNROLL = 8


def _cparams(n_axes):
    return pltpu.CompilerParams(dimension_semantics=("arbitrary",) * n_axes,
                                vmem_limit_bytes=VMEM_LIMIT)


def _pick(n, pref):
    t = min(n, pref)
    while n % t:
        t //= 2
    return t


def _sigmoid(x):
    return 1.0 / (1.0 + jnp.exp(-x))


def _silu(x):
    return x * _sigmoid(x)


def _layer_norm_rows(v, gamma, beta):
    mu = jnp.mean(v, axis=-1, keepdims=True)
    d = v - mu
    var = jnp.mean(d * d, axis=-1, keepdims=True)
    y = d * lax.rsqrt(var + LN_EPS)
    if gamma is not None:
        y = y * gamma
    if beta is not None:
        y = y + beta
    return y


def _bdot(a, b):
    return jnp.dot(a, b, preferred_element_type=F32)


def _ada_kernel(c_ref, w_ref, b_ref, o_ref):
    cond = _silu(c_ref[...]).astype(BF16)
    o_ref[0] = _bdot(cond, w_ref[0].astype(BF16)) + b_ref[0]


def _ada_mods(c, ada_w, ada_b):
    depth, d, d6 = ada_w.shape
    bsz = c.shape[0]
    tn = _pick(d6, 1024)
    return pl.pallas_call(
        _ada_kernel,
        grid=(depth, d6 // tn),
        in_specs=[pl.BlockSpec((bsz, d), lambda l, j: (0, 0)),
                  pl.BlockSpec((1, d, tn), lambda l, j: (l, 0, j)),
                  pl.BlockSpec((1, 1, tn), lambda l, j: (l, 0, j))],
        out_specs=pl.BlockSpec((1, bsz, tn), lambda l, j: (l, 0, j)),
        out_shape=jax.ShapeDtypeStruct((depth, bsz, d6), F32),
        compiler_params=_cparams(2),
        name="ada_mods",
    )(c, ada_w, ada_b.reshape(depth, 1, d6))


def _mm_kernel(*refs, n_w, has_mod, use_scratch, act):
    it = iter(refs)
    x_ref = next(it)
    sc_ref = sh_ref = None
    if has_mod:
        sc_ref, sh_ref = next(it), next(it)
    w_refs = [next(it) for _ in range(n_w)]
    b_refs = [next(it) for _ in range(n_w)]
    o_ref = next(it)
    if use_scratch:
        xb_ref = next(it)

        @pl.when(pl.program_id(1) == 0)
        def _():
            xv = x_ref[...].astype(F32)
            if has_mod:
                xv = xv * (1.0 + sc_ref[0]) + sh_ref[0]
            xb_ref[...] = xv.astype(BF16)

        xb = xb_ref[...]
    else:
        xb = x_ref[...]
    z = [_bdot(xb, w_refs[k][...].astype(BF16)) + b_refs[k][...] for k in range(n_w)]
    if act == "glu":
        out = z[0] * _sigmoid(z[1])
    elif act == "gelu":
        out = jax.nn.gelu(z[0])
    else:
        out = z[0]
    o_ref[...] = out.astype(o_ref.dtype)


def _matmul(x, w, b, *, rows_per_batch, mod=None, act=None, out_dtype=F32, tm=1024, tn=512):
    n, k = x.shape
    nw = w.shape[1]
    n_out = nw // 2 if act == "glu" else nw
    tm = _pick(rows_per_batch, tm)
    tn = _pick(n_out, tn)
    tiles_per_batch = rows_per_batch // tm
    has_mod = mod is not None
    use_scratch = has_mod or x.dtype != BF16
    n_w = 2 if act == "glu" else 1
    half = n_out // tn

    in_specs = [pl.BlockSpec((tm, k), lambda i, j: (i, 0))]
    args = [x]
    if has_mod:
        mspec = pl.BlockSpec((1, 1, k), lambda i, j: (i // tiles_per_batch, 0, 0))
        in_specs += [mspec, mspec]
        args += [mod[0], mod[1]]
    b2 = b.reshape(1, nw)
    in_specs.append(pl.BlockSpec((k, tn), lambda i, j: (0, j)))
    args.append(w)
    if n_w == 2:
        in_specs.append(pl.BlockSpec((k, tn), lambda i, j: (0, j + half)))
        args.append(w)
    in_specs.append(pl.BlockSpec((1, tn), lambda i, j: (0, j)))
    args.append(b2)
    if n_w == 2:
        in_specs.append(pl.BlockSpec((1, tn), lambda i, j: (0, j + half)))
        args.append(b2)
    scratch = [pltpu.VMEM((tm, k), BF16)] if use_scratch else []
    return pl.pallas_call(
        functools.partial(_mm_kernel, n_w=n_w, has_mod=has_mod, use_scratch=use_scratch, act=act),
        grid=(n // tm, n_out // tn),
        in_specs=in_specs,
        out_specs=pl.BlockSpec((tm, tn), lambda i, j: (i, j)),
        out_shape=jax.ShapeDtypeStruct((n, n_out), out_dtype),
        scratch_shapes=scratch,
        compiler_params=_cparams(2),
        name="matmul_" + (act or "bias"),
    )(*args)


def _ln_res_kernel(*refs, alpha, emit_h):
    if emit_h:
        x_ref, y_ref, g_ref, gam_ref, bet_ref, sc_ref, sh_ref, o_ref, h_ref = refs
    else:
        x_ref, y_ref, g_ref, gam_ref, bet_ref, o_ref = refs
    r = alpha * x_ref[...] + g_ref[0] * y_ref[...].astype(F32)
    xn = _layer_norm_rows(r, gam_ref[...], bet_ref[...])
    o_ref[...] = xn
    if emit_h:
        h_ref[...] = (xn * (1.0 + sc_ref[0]) + sh_ref[0]).astype(h_ref.dtype)


def _ln_res(x, y, g, gamma, beta, *, alpha, rows_per_batch, mod=None, h_dtype=BF16, tm=512):
    n, d = x.shape
    tm = _pick(rows_per_batch, tm)
    tpb = rows_per_batch // tm
    row = pl.BlockSpec((tm, d), lambda i: (i, 0))
    per_b = pl.BlockSpec((1, 1, d), lambda i: (i // tpb, 0, 0))
    vec = pl.BlockSpec((1, d), lambda i: (0, 0))
    emit_h = mod is not None
    in_specs = [row, row, per_b, vec, vec]
    args = [x, y, g, gamma.reshape(1, d), beta.reshape(1, d)]
    out_shape = jax.ShapeDtypeStruct((n, d), F32)
    out_specs = row
    if emit_h:
        in_specs += [per_b, per_b]
        args += [mod[0], mod[1]]
        out_shape = (out_shape, jax.ShapeDtypeStruct((n, d), h_dtype))
        out_specs = (row, row)
    return pl.pallas_call(
        functools.partial(_ln_res_kernel, alpha=alpha, emit_h=emit_h),
        grid=(n // tm,),
        in_specs=in_specs,
        out_specs=out_specs,
        out_shape=out_shape,
        compiler_params=_cparams(1),
        name="ln_res",
    )(*args)


def _proj_ln_kernel(h_ref, w_ref, b_ref, x_ref, g_ref, gam_ref, bet_ref, sc_ref, sh_ref, o_ref, h2_ref, wb_ref,
                    *, alpha):
    @pl.when(pl.program_id(0) == 0)
    def _():
        wb_ref[...] = w_ref[...].astype(BF16)

    tm = h_ref.shape[0]
    for r0 in range(0, tm, tm // 2):
        rows = slice(r0, r0 + tm // 2)
        y = _bdot(h_ref[rows, :], wb_ref[...]) + b_ref[...]
        xn = _layer_norm_rows(alpha * x_ref[rows, :] + g_ref[0] * y, gam_ref[...], bet_ref[...])
        o_ref[rows, :] = xn
        h2_ref[rows, :] = (xn * (1.0 + sc_ref[0]) + sh_ref[0]).astype(h2_ref.dtype)


def _proj_ln(h, w, b, x, g, gamma, beta, mod, *, alpha, rows_per_batch, h_dtype, tm=256):
    n, k = h.shape
    d = w.shape[1]
    tm = _pick(rows_per_batch, tm)
    tpb = rows_per_batch // tm
    per_b = pl.BlockSpec((1, 1, d), lambda i: (i // tpb, 0, 0))
    vec = pl.BlockSpec((1, d), lambda i: (0, 0))
    row = pl.BlockSpec((tm, d), lambda i: (i, 0))
    return pl.pallas_call(
        functools.partial(_proj_ln_kernel, alpha=alpha),
        grid=(n // tm,),
        in_specs=[pl.BlockSpec((tm, k), lambda i: (i, 0)),
                  pl.BlockSpec((k, d), lambda i: (0, 0), pipeline_mode=pl.Buffered(1)),
                  vec, row, per_b, vec, vec, per_b, per_b],
        out_specs=(row, row),
        out_shape=(jax.ShapeDtypeStruct((n, d), F32), jax.ShapeDtypeStruct((n, d), h_dtype)),
        scratch_shapes=[pltpu.VMEM((k, d), BF16)],
        compiler_params=_cparams(1),
        name="proj_ln",
    )(h, w, b.reshape(1, d), x, g, gamma.reshape(1, d), beta.reshape(1, d), mod[0], mod[1])


def _group_state(tg_ref, nu_ref):
    i = pl.program_id(1)
    active = i < nu_ref[0]
    changed = jnp.logical_or(i == 0, tg_ref[i] != tg_ref[jnp.maximum(i - 1, 0)])
    return active, jnp.logical_and(active, changed)


def _ffn_up_kernel(tg_ref, nu_ref, x_ref, w1_ref, w3_ref, h_ref, w1b_ref, w3b_ref):
    active, recast = _group_state(tg_ref, nu_ref)

    @pl.when(recast)
    def _():
        w1b_ref[...] = w1_ref[0].astype(BF16)
        w3b_ref[...] = w3_ref[0].astype(BF16)

    @pl.when(active)
    def _():
        x = x_ref[...]
        a = _bdot(x, w1b_ref[...])
        b = _bdot(x, w3b_ref[...])
        h_ref[...] = (_silu(a) * b).astype(h_ref.dtype)

    @pl.when(jnp.logical_not(active))
    def _():
        h_ref[...] = jnp.zeros(h_ref.shape, h_ref.dtype)


def _ffn_down_kernel(tg_ref, nu_ref, h_ref, w2_ref, o_ref, w2b_ref):
    active, recast = _group_state(tg_ref, nu_ref)

    @pl.when(recast)
    def _():
        w2b_ref[...] = w2_ref[0].astype(BF16)

    @pl.when(active)
    def _():
        o_ref[...] = _bdot(h_ref[...], w2b_ref[...])

    @pl.when(jnp.logical_not(active))
    def _():
        o_ref[...] = jnp.zeros(o_ref.shape, o_ref.dtype)


def _grouped_ffn(xb, w1, w3, w2, tile_g, n_used, *, tm, tf=512, tm_down=512, tn=512):
    r, d = xb.shape
    ff = w1.shape[2]
    tf = _pick(ff, tf)
    tn = _pick(d, tn)
    tm_down = _pick(tm, tm_down)
    n_tiles = r // tm
    h = pl.pallas_call(
        _ffn_up_kernel,
        grid_spec=pltpu.PrefetchScalarGridSpec(
            num_scalar_prefetch=2,
            grid=(ff // tf, n_tiles),
            in_specs=[pl.BlockSpec((tm, d), lambda f, i, tg, nu: (i, 0)),
                      pl.BlockSpec((1, d, tf), lambda f, i, tg, nu: (tg[i], 0, f)),
                      pl.BlockSpec((1, d, tf), lambda f, i, tg, nu: (tg[i], 0, f))],
            out_specs=pl.BlockSpec((tm, tf), lambda f, i, tg, nu: (i, f)),
            scratch_shapes=[pltpu.VMEM((d, tf), BF16), pltpu.VMEM((d, tf), BF16)]),
        out_shape=jax.ShapeDtypeStruct((r, ff), BF16),
        compiler_params=_cparams(2),
        name="ffn_up",
    )(tile_g, n_used, xb, w1, w3)
    sub = tm // tm_down
    tile_g2 = jnp.repeat(tile_g, sub)
    n_used2 = n_used * sub
    return pl.pallas_call(
        _ffn_down_kernel,
        grid_spec=pltpu.PrefetchScalarGridSpec(
            num_scalar_prefetch=2,
            grid=(d // tn, n_tiles * sub),
            in_specs=[pl.BlockSpec((tm_down, ff), lambda n, i, tg, nu: (i, 0)),
                      pl.BlockSpec((1, ff, tn), lambda n, i, tg, nu: (tg[i], 0, n))],
            out_specs=pl.BlockSpec((tm_down, tn), lambda n, i, tg, nu: (i, n)),
            scratch_shapes=[pltpu.VMEM((ff, tn), BF16)]),
        out_shape=jax.ShapeDtypeStruct((r, d), F32),
        compiler_params=_cparams(2),
        name="ffn_down",
    )(tile_g2, n_used2, h, w2)


def _conv_kernel(x_ref, w_ref, b_ref, g_ref, be_ref, o_ref, xx_ref, u_ref, *, ts, kw, rc, cc):
    i = pl.program_id(1)
    d = x_ref.shape[1]
    halo = CONV_HALO

    @pl.when(i == 0)
    def _():
        xx_ref[0:halo, :] = jnp.zeros((halo, d), F32)

    @pl.when(i > 0)
    def _():
        xx_ref[0:halo, :] = xx_ref[ts:ts + halo, :]

    xx_ref[halo:halo + ts, :] = x_ref[...]
    off = halo - (kw - 1)

    sub = V7X_SUBLANES
    for r0 in range(0, ts, rc):
        for c0 in range(0, d, cc):
            acc = jnp.broadcast_to(b_ref[:, c0:c0 + cc], (rc, cc))
            for rho in range(sub):
                taps = [j for j in range(kw) if (off + j) % sub == rho]
                if not taps:
                    continue
                base = r0 + off + taps[0]
                win = xx_ref[base:base + rc + taps[-1] - taps[0], c0:c0 + cc]
                part = None
                for j in taps:
                    o = j - taps[0]
                    term = w_ref[j:j + 1, c0:c0 + cc] * win[o:o + rc, :]
                    part = term if part is None else part + term
                acc = acc + part
            u_ref[r0:r0 + rc, c0:c0 + cc] = acc
    y = _layer_norm_rows(u_ref[...], g_ref[...], be_ref[...])
    o_ref[...] = _silu(y).astype(o_ref.dtype)


def _conv_ln_swish(glu, w_dw, b_dw, ln_g, ln_b, *, bsz, seq, ts=128):
    n, d = glu.shape
    kw = w_dw.shape[0]
    ts = _pick(seq, ts)
    nt = seq // ts
    rc = _pick(ts, 64)
    cc = _pick(d, 256)
    row = pl.BlockSpec((ts, d), lambda b, i: (b * nt + i, 0))
    vec = pl.BlockSpec((1, d), lambda b, i: (0, 0))
    return pl.pallas_call(
        functools.partial(_conv_kernel, ts=ts, kw=kw, rc=rc, cc=cc),
        grid=(bsz, nt),
        in_specs=[row, pl.BlockSpec((kw, d), lambda b, i: (0, 0)), vec, vec, vec],
        out_specs=row,
        out_shape=jax.ShapeDtypeStruct((n, d), BF16),
        scratch_shapes=[pltpu.VMEM((CONV_HALO + ts, d), F32), pltpu.VMEM((ts, d), F32)],
        compiler_params=_cparams(2),
        name="conv_ln_swish",
    )(glu, w_dw, b_dw.reshape(1, d), ln_g.reshape(1, d), ln_b.reshape(1, d))


def _iota_div(x, k):
    assert k & (k - 1) == 0
    return lax.shift_right_logical(x, k.bit_length() - 1)


def _iota_mod(x, k):
    assert k & (k - 1) == 0
    return lax.bitwise_and(x, k - 1)


def _s5_expand_tables(kq_ref, mq_ref, cq_ref, t_scr, m_scr, c_scr, *, lc, p):
    lanes = t_scr.shape[0] // lc
    ns2 = m_scr.shape[1]
    ns = ns2 // 2
    nst = ns // (lanes // p)
    iota = lambda shape, dim: lax.broadcasted_iota(jnp.int32, shape, dim)
    r, c = iota((lanes, lanes), 0), iota((lanes, lanes), 1)
    same_g = _iota_div(r, p) == _iota_div(c, p)
    e_t = (_iota_mod(iota((p, lanes), 1), p) == iota((p, lanes), 0)).astype(BF16)
    tiles = [jnp.where(same_g, _bdot(kq_ref[0, tau].astype(BF16), e_t), 0.0).astype(BF16) for tau in range(lc)]
    zero_tile = jnp.zeros((lanes, lanes), BF16)
    for s in range(lc):
        for t in range(lc):
            t_scr[s * lanes:(s + 1) * lanes, t * lanes:(t + 1) * lanes] = tiles[t - s] if t >= s else zero_tile
    r, c = iota((lanes, ns2), 0), iota((lanes, ns2), 1)
    e_m = (r == _iota_div(c, ns) * nst + _iota_mod(c, nst)).astype(BF16)
    mask_m = _iota_div(r, p) == _iota_div(_iota_mod(c, ns), nst)
    for s in range(lc):
        m_scr[s * lanes:(s + 1) * lanes, :] = jnp.where(
            mask_m, _bdot(mq_ref[0, s].astype(BF16), e_m), 0.0).astype(BF16)
    r, c = iota((ns2, lanes), 0), iota((ns2, lanes), 1)
    e_c = (c == _iota_div(r, ns) * nst + _iota_mod(r, nst)).astype(BF16)
    mask_c = _iota_div(_iota_mod(r, ns), nst) == _iota_div(c, p)
    for t in range(lc):
        c_scr[:, t * lanes:(t + 1) * lanes] = jnp.where(
            mask_c, _bdot(e_c, cq_ref[0, t].astype(BF16)), 0.0).astype(BF16)


def _s5_kernel(u_ref, kq_ref, mq_ref, cq_ref, lr_ref, li_ref, d_ref, o_ref,
               t_scr, m_scr, c_scr, a_ref, v_ref, p_ref, *, lc, nc, p):
    @pl.when(pl.program_id(1) == 0)
    def _():
        _s5_expand_tables(kq_ref, mq_ref, cq_ref, t_scr, m_scr, c_scr, lc=lc, p=p)

    lanes = u_ref.shape[2]
    for t in range(lc):
        a_ref[:, t * lanes:(t + 1) * lanes] = u_ref[0, pl.ds(t, nc, stride=lc), :]
    a = a_ref[...]
    ab = a.astype(BF16)
    v_ref[...] = _bdot(ab, m_scr[...])
    lr = lr_ref[0]
    li = li_ref[0]
    ns = lr.shape[1]

    slab = V7X_SUBLANES
    srow = lax.broadcasted_iota(jnp.int32, (slab, ns), 0)

    def step(k, carry):
        sr, si = carry
        r0 = pl.multiple_of(k * slab, slab)
        xr = v_ref[pl.ds(r0, slab), 0:ns]
        xi = v_ref[pl.ds(r0, slab), ns:2 * ns]
        prev_r = sr
        prev_i = si
        for j in range(slab):
            if j > 0:
                keep = srow < j
                prev_r = jnp.where(keep, prev_r, pltpu.roll(sr, j, 0))
                prev_i = jnp.where(keep, prev_i, pltpu.roll(si, j, 0))
                xrj = pltpu.roll(xr, slab - j, 0)
                xij = pltpu.roll(xi, slab - j, 0)
            else:
                xrj, xij = xr, xi
            sr, si = lr * sr - li * si + xrj, lr * si + li * sr + xij
        p_ref[pl.ds(r0, slab), 0:ns] = prev_r
        p_ref[pl.ds(r0, slab), ns:2 * ns] = prev_i
        return sr, si

    zero = jnp.zeros((slab, ns), F32)
    lax.fori_loop(0, nc // slab, step, (zero, zero))
    y = _bdot(ab, t_scr[...]) + _bdot(p_ref[...].astype(BF16), c_scr[...]) + d_ref[0] * a
    y = jax.nn.gelu(y)
    for t in range(lc):
        o_ref[0, pl.ds(t, nc, stride=lc), :] = y[:, t * lanes:(t + 1) * lanes]


def _s5_tables(a_re, a_im, log_dt, b_re, b_im, c_re, c_im, d_skip):
    g, n = a_re.shape
    p = b_re.shape[2]
    lc = S5_CHUNK
    ar, ai = a_re.astype(F32), a_im.astype(F32)
    dt = jnp.exp(log_dt.astype(F32))[:, None]
    decay = jnp.exp(ar * dt)
    lr, li = decay * jnp.cos(ai * dt), decay * jnp.sin(ai * dt)
    den = ar * ar + ai * ai
    zr = ((lr - 1.0) * ar + li * ai) / den
    zi = (li * ar - (lr - 1.0) * ai) / den
    br, bi = b_re.astype(F32), b_im.astype(F32)
    bbr = zr[..., None] * br - zi[..., None] * bi
    bbi = zr[..., None] * bi + zi[..., None] * br
    tau = jnp.arange(lc + 1, dtype=F32)[:, None, None]
    pdec = jnp.exp(tau * (ar * dt)[None])
    pr, pi = pdec * jnp.cos(tau * (ai * dt)[None]), pdec * jnp.sin(tau * (ai * dt)[None])
    cr, ci = c_re.astype(F32), c_im.astype(F32)
    hp = lax.Precision.HIGHEST
    lbr = pr[..., None] * bbr[None] - pi[..., None] * bbi[None]
    lbi = pr[..., None] * bbi[None] + pi[..., None] * bbr[None]
    ktau = (jnp.einsum('gpn,tgnq->tgpq', cr, lbr[:lc], precision=hp)
            - jnp.einsum('gpn,tgnq->tgpq', ci, lbi[:lc], precision=hp))
    gb = V7X_LANES // p
    nj = g // gb
    kq = ktau.reshape(lc, nj, gb, p, p).transpose(1, 0, 2, 4, 3).reshape(nj, lc, gb * p, p)
    rev = lc - 1 - jnp.arange(lc)
    mq = jnp.stack([lbr[rev], lbi[rev]], axis=2).reshape(lc, nj, gb, 2, n, p)
    mq = mq.transpose(1, 0, 2, 5, 3, 4).reshape(nj, lc, gb * p, 2 * n)
    pr1, pi1 = pr[1:], pi[1:]
    cmr = (cr[None] * pr1[:, :, None, :] - ci[None] * pi1[:, :, None, :])
    cmi = -(cr[None] * pi1[:, :, None, :] + ci[None] * pr1[:, :, None, :])
    cq = jnp.stack([cmr, cmi], axis=0).reshape(2, lc, nj, gb, p, n)
    cq = cq.transpose(2, 1, 0, 5, 3, 4).reshape(nj, lc, 2 * n, gb * p)
    lam_r = pr[lc].reshape(nj, 1, gb * n)
    lam_i = pi[lc].reshape(nj, 1, gb * n)
    dvec = jnp.broadcast_to(d_skip.astype(F32).reshape(nj, 1, 1, gb * p), (nj, 1, lc, gb * p))
    return kq, mq, cq, lam_r, lam_i, dvec.reshape(nj, 1, lc * gb * p)


def _s5_core(u, tables, *, bsz, seq):
    n_rows, d = u.shape
    kq, mq, cq, lam_r, lam_i, dvec = tables
    nj, lc, lanes, p = kq.shape
    ns = lam_r.shape[2]
    nc = seq // lc
    wk = lc * lanes
    assert lanes == V7X_LANES and nc % V7X_SUBLANES == 0 and nj * lanes == d
    tab = lambda *shape: pl.BlockSpec((1,) + shape, lambda j, b: (j,) + (0,) * len(shape))
    seq_blk = pl.BlockSpec((1, seq, lanes), lambda j, b: (b, 0, j))
    y = pl.pallas_call(
        functools.partial(_s5_kernel, lc=lc, nc=nc, p=p),
        grid=(nj, bsz),
        in_specs=[seq_blk, tab(lc, lanes, p), tab(lc, lanes, lanes), tab(lc, lanes, lanes),
                  tab(1, ns), tab(1, ns), tab(1, wk)],
        out_specs=seq_blk,
        out_shape=jax.ShapeDtypeStruct((bsz, seq, d), F32),
        scratch_shapes=[pltpu.VMEM((wk, wk), BF16), pltpu.VMEM((wk, 2 * ns), BF16), pltpu.VMEM((2 * ns, wk), BF16),
                        pltpu.VMEM((nc, wk), F32), pltpu.VMEM((nc, 2 * ns), F32), pltpu.VMEM((nc, 2 * ns), F32)],
        compiler_params=_cparams(2),
        name="s5_core",
    )(u.reshape(bsz, seq, d), kq, mq, cq, lam_r, lam_i, dvec)
    return y.reshape(n_rows, d)


def _log_sigmoid(x):
    return jnp.minimum(x, 0.0) - jnp.log(1.0 + jnp.exp(-jnp.abs(x)))


def _mlstm_kernel(q_ref, k_ref, v_ref, o_ref, g_ref, cw_ref, cb_ref, mh_ref, out_ref,
                  c_st, n_st, m_st, qx_ref, kx_ref, *, lm, dk, heads, kw):
    c = pl.program_id(1)
    pad = V7X_SUBLANES
    hdk = heads * dk
    dv = v_ref.shape[1] // heads

    @pl.when(c == 0)
    def _():
        c_st[...] = jnp.zeros(c_st.shape, F32)
        n_st[...] = jnp.zeros(n_st.shape, F32)
        m_st[...] = jnp.zeros(m_st.shape, F32)
        qx_ref[0:pad, :] = jnp.zeros((pad, hdk), F32)
        kx_ref[0:pad, :] = jnp.zeros((pad, hdk), F32)

    def conv_swish(x_ref, xx_ref, c0):
        xx_ref[pad:pad + lm, :] = x_ref[...]
        acc = jnp.broadcast_to(cb_ref[:, c0:c0 + hdk], (lm, hdk))
        for j in range(kw):
            acc = acc + cw_ref[j:j + 1, c0:c0 + hdk] * xx_ref[pad - (kw - 1) + j:pad - (kw - 1) + j + lm, :]
        xx_ref[0:pad, :] = xx_ref[lm:lm + pad, :]
        return _silu(acc)

    q_all = conv_swish(q_ref, qx_ref, 0)
    k_all = conv_swish(k_ref, kx_ref, hdk) * (dk ** -0.5)

    gts = g_ref[...]
    lf = _log_sigmoid(gts)
    rows = lax.broadcasted_iota(jnp.int32, (lm, lm), 0)
    cols = lax.broadcasted_iota(jnp.int32, (lm, lm), 1)
    causal = rows >= cols
    tri = causal.astype(BF16)
    lf_hi = lf.astype(BF16)
    r1 = lf - lf_hi.astype(F32)
    lf_mid = r1.astype(BF16)
    lf_lo = (r1 - lf_mid.astype(F32)).astype(BF16)
    bcum = _bdot(tri, lf_hi) + _bdot(tri, lf_mid) + _bdot(tri, lf_lo)

    bcum_t = bcum.T
    gts_t = gts.T
    for h in range(heads):
        q = q_all[:, h * dk:(h + 1) * dk]
        k = k_all[:, h * dk:(h + 1) * dk]
        vcols = slice(h * dv, (h + 1) * dv)
        b_col = bcum[:, heads + h:heads + h + 1]
        li_col = gts[:, h:h + 1]
        b_row = bcum_t[heads + h:heads + h + 1, :]
        li_row = gts_t[h:h + 1, :]

        m_prev = m_st[h]
        dmat = jnp.where(causal, b_col - b_row + li_row, -jnp.inf)
        inter = b_col + m_prev
        m_j = jnp.maximum(inter, jnp.max(dmat, axis=1, keepdims=True))
        w_intra = jnp.exp(dmat - m_j)
        w_inter = jnp.exp(inter - m_j)
        qb, kb, vb = q.astype(BF16), k.astype(BF16), v_ref[:, vcols].astype(BF16)
        sc = lax.dot_general(qb, kb, (((1,), (1,)), ((), ())), preferred_element_type=F32) * w_intra
        num = w_inter * _bdot(qb, c_st[h].astype(BF16)) + _bdot(sc.astype(BF16), vb)
        den = w_inter * jnp.sum(q * n_st[h], axis=1, keepdims=True) + jnp.sum(sc, axis=1, keepdims=True)
        hb = num / jnp.maximum(jnp.abs(den), jnp.exp(-m_j))

        bl = b_col[lm - 1:lm, :]
        gsum = bl - b_col + li_col
        m_new = jnp.maximum(bl + m_prev, jnp.max(gsum, axis=0, keepdims=True))
        wc = jnp.exp(bl + m_prev - m_new)
        wk = jnp.exp(gsum - m_new)
        kwt = wk * k
        c_st[h] = wc * c_st[h] + _bdot(kwt.T.astype(BF16), vb)
        n_st[h] = wc * n_st[h] + jnp.sum(kwt, axis=0, keepdims=True)
        m_st[h] = m_new

        hs = _sigmoid(o_ref[:, vcols]) * hb
        out_ref[:, vcols] = (_layer_norm_rows(hs, None, None) * mh_ref[:, vcols]).astype(out_ref.dtype)


def _mlstm_core(z, w_conv, b_conv, mh_g, *, bsz, seq, d):
    n_rows = z.shape[0]
    heads = ML_HEADS
    dv = d // heads
    dk = dv // 2
    lm = _pick(seq, ML_CHUNK_ROWS)
    nc = seq // lm
    kw = w_conv.shape[0]
    hdk = heads * dk
    hdv = heads * dv
    assert 2 * hdk == hdv
    gate_blk = (2 * hdk + 2 * hdv) // V7X_LANES
    row = lambda b, c: b * nc + c
    in_specs = [
        pl.BlockSpec((lm, hdk), lambda b, c: (row(b, c), 0)),
        pl.BlockSpec((lm, hdk), lambda b, c: (row(b, c), 1)),
        pl.BlockSpec((lm, hdv), lambda b, c: (row(b, c), 1)),
        pl.BlockSpec((lm, hdv), lambda b, c: (row(b, c), 2)),
        pl.BlockSpec((lm, V7X_LANES), lambda b, c: (row(b, c), gate_blk)),
        pl.BlockSpec((kw, 2 * hdk), lambda b, c: (0, 0)),
        pl.BlockSpec((1, 2 * hdk), lambda b, c: (0, 0)),
        pl.BlockSpec((1, d), lambda b, c: (0, 0)),
    ]
    return pl.pallas_call(
        functools.partial(_mlstm_kernel, lm=lm, dk=dk, heads=heads, kw=kw),
        grid=(bsz, nc),
        in_specs=in_specs,
        out_specs=pl.BlockSpec((lm, d), lambda b, c: (row(b, c), 0)),
        out_shape=jax.ShapeDtypeStruct((n_rows, d), BF16),
        scratch_shapes=[pltpu.VMEM((heads, dk, dv), F32), pltpu.VMEM((heads, 1, dk), F32),
                        pltpu.VMEM((heads, 1, 1), F32),
                        pltpu.VMEM((lm + V7X_SUBLANES, hdk), F32), pltpu.VMEM((lm + V7X_SUBLANES, hdk), F32)],
        compiler_params=_cparams(2),
        name="mlstm_core",
    )(z, z, z, z, z, w_conv, b_conv.reshape(1, -1), mh_g.reshape(1, d))


def _gmlp_kernel(u_ref, v_ref, g_ref, be_ref, w_ref, bsp_ref, o_ref, *, lg, groups, dg, nsub):
    rows = lax.broadcasted_iota(jnp.int32, (lg, lg), 0)
    cols = lax.broadcasted_iota(jnp.int32, (lg, lg), 1)
    keep = rows >= cols
    wm = [jnp.where(keep, w_ref[g], 0.0).astype(BF16) for g in range(groups)]
    for s in range(nsub):
        r = slice(s * lg, (s + 1) * lg)
        vn = _layer_norm_rows(v_ref[r, :], g_ref[...], be_ref[...]).astype(BF16)
        for g in range(groups):
            cs = slice(g * dg, (g + 1) * dg)
            sv = _bdot(wm[g], vn[:, cs]) + bsp_ref[:, g:g + 1]
            o_ref[r, cs] = (u_ref[r, cs] * sv).astype(o_ref.dtype)


def _gmlp_core(zz, ln_g, ln_b, w_sp, b_sp, *, seq, tm=512):
    n, d2 = zz.shape
    d = d2 // 2
    groups, lg, _ = w_sp.shape
    tm = max(lg, _pick(seq, tm))
    return pl.pallas_call(
        functools.partial(_gmlp_kernel, lg=lg, groups=groups, dg=d // groups, nsub=tm // lg),
        grid=(n // tm,),
        in_specs=[pl.BlockSpec((tm, d), lambda i: (i, 0)),
                  pl.BlockSpec((tm, d), lambda i: (i, 1)),
                  pl.BlockSpec((1, d), lambda i: (0, 0)),
                  pl.BlockSpec((1, d), lambda i: (0, 0)),
                  pl.BlockSpec((groups, lg, lg), lambda i: (0, 0, 0)),
                  pl.BlockSpec((lg, groups), lambda i: (0, 0))],
        out_specs=pl.BlockSpec((tm, d), lambda i: (i, 0)),
        out_shape=jax.ShapeDtypeStruct((n, d), BF16),
        compiler_params=_cparams(1),
        name="gmlp_spatial",
    )(zz, zz, ln_g.reshape(1, d), ln_b.reshape(1, d), w_sp, b_sp.T)


def _router_kernel(h_ref, w_ref, o_ref):
    h = h_ref[...]
    w = w_ref[...]
    hh = h.astype(BF16)
    hl = (h - hh.astype(F32)).astype(BF16)
    wh = w.astype(BF16)
    wl = (w - wh.astype(F32)).astype(BF16)
    o_ref[...] = _bdot(hh, wh) + _bdot(hl, wh) + _bdot(hh, wl)


def _router_logits(hf, w_router, *, tm=1024):
    n, d = hf.shape
    e = w_router.shape[1]
    wpad = jnp.pad(w_router, ((0, 0), (0, V7X_LANES - e)))
    tm = _pick(n, tm)
    out = pl.pallas_call(
        _router_kernel,
        grid=(n // tm,),
        in_specs=[pl.BlockSpec((tm, d), lambda i: (i, 0)),
                  pl.BlockSpec((d, V7X_LANES), lambda i: (0, 0))],
        out_specs=pl.BlockSpec((tm, V7X_LANES), lambda i: (i, 0)),
        out_shape=jax.ShapeDtypeStruct((n, V7X_LANES), F32),
        compiler_params=_cparams(1),
        name="moe_router",
    )(hf, wpad)
    return out[:, :e]


def _moe_gather_kernel(nu_ref, tok_ref, tokn_ref, hf_hbm, o_ref, xg_ref, sem, *, tm):
    i = pl.program_id(0)
    nu = nu_ref[0]
    slot = lax.rem(i, 2)

    def row_copy(tref, r, s):
        return pltpu.make_async_copy(hf_hbm.at[pl.ds(tref[0, 0, r], 1), :], xg_ref.at[s, pl.ds(r, 1), :], sem.at[s])

    def start_tile(tref, s):
        def issue(r, carry):
            row_copy(tref, r, s).start()
            return carry

        lax.fori_loop(0, tm, issue, 0, unroll=DMA_UNROLL)

    @pl.when(jnp.logical_and(i == 0, nu > 0))
    def _():
        start_tile(tok_ref, 0)

    @pl.when(i + 1 < nu)
    def _():
        start_tile(tokn_ref, 1 - slot)

    @pl.when(i < nu)
    def _():
        def wait(r, carry):
            row_copy(tok_ref, r, slot).wait()
            return carry

        lax.fori_loop(0, tm, wait, 0, unroll=DMA_UNROLL)
        o_ref[...] = xg_ref[slot].astype(o_ref.dtype)

    @pl.when(i >= nu)
    def _():
        o_ref[...] = jnp.zeros(o_ref.shape, o_ref.dtype)


def _moe_gather(hf, row_tok, n_used, *, tm):
    n, d = hf.shape
    n_tiles = row_tok.shape[0]
    tok_spec = lambda fn: pl.BlockSpec((1, 1, tm), fn, memory_space=pltpu.SMEM)
    return pl.pallas_call(
        functools.partial(_moe_gather_kernel, tm=tm),
        grid_spec=pltpu.PrefetchScalarGridSpec(
            num_scalar_prefetch=1,
            grid=(n_tiles,),
            in_specs=[tok_spec(lambda i, nu: (i, 0, 0)),
                      tok_spec(lambda i, nu: (jnp.minimum(i + 1, n_tiles - 1), 0, 0)),
                      pl.BlockSpec(memory_space=pl.ANY)],
            out_specs=pl.BlockSpec((tm, d), lambda i, nu: (i, 0)),
            scratch_shapes=[pltpu.VMEM((2, tm, d), hf.dtype), pltpu.SemaphoreType.DMA((2,))]),
        out_shape=jax.ShapeDtypeStruct((n_tiles * tm, d), BF16),
        compiler_params=_cparams(1),
        name="moe_gather",
    )(n_used, row_tok, row_tok, hf)


def _moe_combine_kernel(*refs, tm, alpha, emit_h):
    if emit_h:
        (pos_ref, posn_ref, y_hbm, x_ref, gate_ref, g_ref, gam_ref, bet_ref, sc_ref, sh_ref,
         o_ref, h_ref, yb_ref, sem) = refs
    else:
        pos_ref, posn_ref, y_hbm, x_ref, gate_ref, g_ref, gam_ref, bet_ref, o_ref, yb_ref, sem = refs
    i = pl.program_id(0)
    nt = pl.num_programs(0)
    slot = lax.rem(i, 2)

    def row_copy(pref, k, r, s):
        return pltpu.make_async_copy(y_hbm.at[pl.ds(pref[0, 0, k * tm + r], 1), :],
                                     yb_ref.at[s, k, pl.ds(r, 1), :], sem.at[s])

    def start_tile(pref, s):
        def issue(r, carry):
            for k in range(TOP_K):
                row_copy(pref, k, r, s).start()
            return carry

        lax.fori_loop(0, tm, issue, 0, unroll=DMA_UNROLL)

    @pl.when(i == 0)
    def _():
        start_tile(pos_ref, 0)

    @pl.when(i + 1 < nt)
    def _():
        start_tile(posn_ref, 1 - slot)

    def wait(r, carry):
        for k in range(TOP_K):
            row_copy(pos_ref, k, r, slot).wait()
        return carry

    lax.fori_loop(0, tm, wait, 0, unroll=DMA_UNROLL)
    gate = gate_ref[...]
    y = gate[:, 0:1] * yb_ref[slot, 0]
    for k in range(1, TOP_K):
        y = y + gate[:, k:k + 1] * yb_ref[slot, k]
    r = alpha * x_ref[...] + g_ref[0] * y
    xn = _layer_norm_rows(r, gam_ref[...], bet_ref[...])
    o_ref[...] = xn
    if emit_h:
        h_ref[...] = (xn * (1.0 + sc_ref[0]) + sh_ref[0]).astype(h_ref.dtype)


def _moe_combine_ln(y_rows, pos, gates, x, g, gamma, beta, *, alpha, rows_per_batch, mod=None, tm=256):
    n, d = x.shape
    tm = _pick(rows_per_batch, tm)
    tpb = rows_per_batch // tm
    nt = n // tm
    pos3 = pos.reshape(nt, tm, TOP_K).transpose(0, 2, 1).reshape(nt, 1, TOP_K * tm)
    row = pl.BlockSpec((tm, d), lambda i: (i, 0))
    vec = pl.BlockSpec((1, d), lambda i: (0, 0))
    per_b = pl.BlockSpec((1, 1, d), lambda i: (i // tpb, 0, 0))
    emit_h = mod is not None
    in_specs = [pl.BlockSpec((1, 1, TOP_K * tm), lambda i: (i, 0, 0), memory_space=pltpu.SMEM),
                pl.BlockSpec((1, 1, TOP_K * tm), lambda i: (jnp.minimum(i + 1, nt - 1), 0, 0),
                             memory_space=pltpu.SMEM),
                pl.BlockSpec(memory_space=pl.ANY),
                row,
                pl.BlockSpec((tm, TOP_K), lambda i: (i, 0)),
                per_b, vec, vec]
    args = [pos3, pos3, y_rows, x, gates, g, gamma.reshape(1, d), beta.reshape(1, d)]
    out_specs, out_shape = row, jax.ShapeDtypeStruct((n, d), F32)
    if emit_h:
        in_specs += [per_b, per_b]
        args += [mod[0], mod[1]]
        out_specs = (row, row)
        out_shape = (out_shape, jax.ShapeDtypeStruct((n, d), BF16))
    return pl.pallas_call(
        functools.partial(_moe_combine_kernel, tm=tm, alpha=alpha, emit_h=emit_h),
        grid=(nt,),
        in_specs=in_specs,
        out_specs=out_specs,
        out_shape=out_shape,
        scratch_shapes=[pltpu.VMEM((2, TOP_K, tm, d), F32), pltpu.SemaphoreType.DMA((2,))],
        compiler_params=_cparams(1),
        name="moe_combine_ln",
    )(*args)


def _route(logits, tm):
    n, e = logits.shape
    top_v, top_e = lax.top_k(logits, TOP_K)
    gates = jax.nn.softmax(top_v, axis=-1)
    e_flat = top_e.reshape(-1).astype(jnp.int32)
    onehot = (e_flat[:, None] == jnp.arange(e, dtype=jnp.int32)[None, :]).astype(jnp.int32)
    cum = jnp.cumsum(onehot, axis=0)
    rank = jnp.take_along_axis(cum, e_flat[:, None], axis=1)[:, 0] - 1
    counts = cum[-1]
    padded = (counts + tm - 1) // tm * tm
    pend = jnp.cumsum(padded)
    pstart = pend - padded
    dest = (pstart[e_flat] + rank).astype(jnp.int32)
    n_tiles = (n * TOP_K) // tm + e
    tok_flat = jnp.repeat(jnp.arange(n, dtype=jnp.int32), TOP_K)
    row_tok = jnp.zeros((n_tiles * tm,), jnp.int32).at[dest].set(tok_flat)
    n_used = (pend[-1] // tm).astype(jnp.int32)
    tidx = jnp.arange(n_tiles, dtype=jnp.int32)
    tile_e = jnp.minimum(jnp.searchsorted(pend, tidx * tm, side='right'), e - 1).astype(jnp.int32)
    tile_e = jnp.where(tidx < n_used, tile_e, tile_e[jnp.maximum(n_used - 1, 0)])
    return gates, dest.reshape(n, TOP_K), row_tok.reshape(n_tiles, 1, tm), tile_e, n_used.reshape(1)


def _moe_layer(x, hf, g2, gamma, beta, w_router, w1, w3, w2, layer_idx, *, alpha, rows_per_batch, mod_next=None,
               tm=1024):
    n = hf.shape[0]
    tm = _pick(n * TOP_K, tm)
    logits = _router_logits(hf, w_router)
    gates, pos, row_tok, tile_e, n_used = _route(logits, tm)
    xs = _moe_gather(hf, row_tok, n_used, tm=tm)
    y_rows = _grouped_ffn(xs, w1, w3, w2, tile_e + layer_idx * N_EXPERTS, n_used, tm=tm)
    return _moe_combine_ln(y_rows, pos, gates, x, g2, gamma, beta, alpha=alpha, rows_per_batch=rows_per_batch,
                           mod=mod_next)


def kernel(x, c, ada_w, ada_b, ln1_g, ln1_b, ln2_g, ln2_b, a_w_in, a_b_in, a_w_dw, a_b_dw, a_ln_g, a_ln_b, a_w_out, a_b_out, b_w_in, b_b_in, b_a_re, b_a_im, b_log_dt, b_b_re, b_b_im, b_c_re, b_c_im, b_d, b_w_glu, b_b_glu, c_w_in, c_b_in, c_w_conv, c_b_conv, c_mh_g, c_w_out, c_b_out, d_w_in, d_b_in, d_ln_g, d_ln_b, d_w_sp, d_b_sp, d_w_out, d_b_out, f_w1, f_w3, f_w2, m_router, m_w1, m_w3, m_w2):
    bsz, seq, d = x.shape
    depth = ada_w.shape[0]
    alpha = (2.0 * depth) ** 0.25
    n = bsz * seq
    mm = functools.partial(_matmul, rows_per_batch=seq)

    mods = _ada_mods(c, ada_w, ada_b)
    xf = x.reshape(n, d)
    ia = ib = ic = idd = 0
    i_dense = i_moe = 0
    hm = None
    mod_of = lambda l: [mods[l, :, k * d:(k + 1) * d].reshape(bsz, 1, d) for k in range(6)]
    for layer in range(depth):
        sh1, sc1, g1, sh2, sc2, g2 = mod_of(layer)
        mod_next = None
        if layer + 1 < depth:
            sh1n, sc1n = mod_of(layer + 1)[:2]
            mod_next = (sc1n, sh1n)

        def mix_in(w, b, xf=xf, hm=hm, sc1=sc1, sh1=sh1, tm_bf16=2048, **kw):
            if hm is None:
                return mm(xf, w, b, mod=(sc1, sh1), **kw)
            return mm(hm, w, b, tm=tm_bf16, **kw)

        kind = layer % N_MIXERS
        is_moe = layer % 2 == 1
        h_dtype = F32 if is_moe else BF16
        proj_ln = functools.partial(_proj_ln, x=xf, g=g1, gamma=ln1_g[layer], beta=ln1_b[layer], mod=(sc2, sh2),
                                    alpha=alpha, rows_per_batch=seq, h_dtype=h_dtype)
        if kind == 0:
            glu = mix_in(a_w_in[ia], a_b_in[ia], act="glu")
            u = _conv_ln_swish(glu, a_w_dw[ia], a_b_dw[ia], a_ln_g[ia], a_ln_b[ia], bsz=bsz, seq=seq)
            xf, hf = proj_ln(u, a_w_out[ia], a_b_out[ia])
            ia += 1
        elif kind == 1:
            u = mix_in(b_w_in[ib], b_b_in[ib])
            tables = _s5_tables(b_a_re[ib], b_a_im[ib], b_log_dt[ib], b_b_re[ib], b_b_im[ib],
                                b_c_re[ib], b_c_im[ib], b_d[ib])
            yg = _s5_core(u, tables, bsz=bsz, seq=seq)
            y = mm(yg, b_w_glu[ib], b_b_glu[ib], act="glu")
            xf, hf = _ln_res(xf, y, g1, ln1_g[layer], ln1_b[layer], alpha=alpha, rows_per_batch=seq,
                             mod=(sc2, sh2), h_dtype=h_dtype)
            ib += 1
        elif kind == 2:
            heads = ML_HEADS
            dv = d // heads
            n_main = 2 * heads * (dv // 2) + 2 * heads * dv
            w_in = jnp.pad(c_w_in[ic], ((0, 0), (0, V7X_LANES - 2 * heads)))
            b_in = jnp.pad(c_b_in[ic], ((0, V7X_LANES - 2 * heads),))
            assert w_in.shape[1] == n_main + V7X_LANES
            z = mix_in(w_in, b_in, tn=896, tm_bf16=1024)
            hs = _mlstm_core(z, c_w_conv[ic], c_b_conv[ic], c_mh_g[ic], bsz=bsz, seq=seq, d=d)
            xf, hf = proj_ln(hs, c_w_out[ic], c_b_out[ic])
            ic += 1
        else:
            zz = mix_in(d_w_in[idd], d_b_in[idd], act="gelu")
            gated = _gmlp_core(zz, d_ln_g[idd], d_ln_b[idd], d_w_sp[idd], d_b_sp[idd], seq=seq)
            xf, hf = proj_ln(gated, d_w_out[idd], d_b_out[idd])
            idd += 1
        if not is_moe:
            tm_ffn = _pick(n, 1024)
            tile_g = jnp.full((n // tm_ffn,), i_dense, jnp.int32)
            n_used = jnp.full((1,), n // tm_ffn, jnp.int32)
            y = _grouped_ffn(hf, f_w1, f_w3, f_w2, tile_g, n_used, tm=tm_ffn)
            res = _ln_res(xf, y, g2, ln2_g[layer], ln2_b[layer], alpha=alpha, rows_per_batch=seq, mod=mod_next)
            i_dense += 1
        else:
            merge = lambda w: w.reshape((w.shape[0] * w.shape[1],) + w.shape[2:])
            res = _moe_layer(xf, hf, g2, ln2_g[layer], ln2_b[layer], m_router[i_moe], merge(m_w1),
                             merge(m_w3), merge(m_w2), i_moe, alpha=alpha, rows_per_batch=seq, mod_next=mod_next)
            i_moe += 1
        xf, hm = res if mod_next is not None else (res, None)
    return xf.reshape(bsz, seq, d)
```

```python
import functools
import math

import jax
import jax.numpy as jnp
from jax import lax
from jax.experimental import pallas as pl
from jax.experimental.pallas import tpu as pltpu

F32 = jnp.float32
BF16 = jnp.bfloat16

LN_EPS = 1e-5
N_MIXERS = 4
CONV_WIDTH = 31
S5_GROUP = 16
S5_STATE = 64
ML_HEADS = 8
ML_CONV = 4
GM_CHUNK = 128
GM_GROUPS = 8
N_EXPERTS = 8
TOP_K = 2

V7X_VMEM_BYTES = 64 * 1024 * 1024
V7X_LANES = 128
V7X_SUBLANES = 8
VMEM_LIMIT = V7X_VMEM_BYTES - 8 * 1024 * 1024

S5_CHUNK = 16
ML_CHUNK_ROWS = 256
CONV_HALO = 32
DMA_UNROLL = 8


def _cparams(n_axes):
    return pltpu.CompilerParams(dimension_semantics=("arbitrary",) * n_axes,
                                vmem_limit_bytes=VMEM_LIMIT)


def _pick(n, pref):
    t = min(n, pref)
    while n % t:
        t //= 2
    return t


def _sigmoid(x):
    return 1.0 / (1.0 + jnp.exp(-x))


def _silu(x):
    return x * _sigmoid(x)


def _layer_norm_rows(v, gamma, beta):
    mu = jnp.mean(v, axis=-1, keepdims=True)
    d = v - mu
    var = jnp.mean(d * d, axis=-1, keepdims=True)
    y = d * lax.rsqrt(var + LN_EPS)
    if gamma is not None:
        y = y * gamma
    if beta is not None:
        y = y + beta
    return y


def _bdot(a, b):
    return jnp.dot(a, b, preferred_element_type=F32)


def _ada_kernel(c_ref, w_ref, b_ref, o_ref):
    cond = _silu(c_ref[...]).astype(BF16)
    o_ref[0] = _bdot(cond, w_ref[0].astype(BF16)) + b_ref[0]


def _ada_mods(c, ada_w, ada_b):
    depth, d, d6 = ada_w.shape
    bsz = c.shape[0]
    tn = _pick(d6, 1024)
    return pl.pallas_call(
        _ada_kernel,
        grid=(depth, d6 // tn),
        in_specs=[pl.BlockSpec((bsz, d), lambda l, j: (0, 0)),
                  pl.BlockSpec((1, d, tn), lambda l, j: (l, 0, j)),
                  pl.BlockSpec((1, 1, tn), lambda l, j: (l, 0, j))],
        out_specs=pl.BlockSpec((1, bsz, tn), lambda l, j: (l, 0, j)),
        out_shape=jax.ShapeDtypeStruct((depth, bsz, d6), F32),
        compiler_params=_cparams(2),
        name="ada_mods",
    )(c, ada_w, ada_b.reshape(depth, 1, d6))


def _mm_kernel(*refs, n_w, has_mod, use_scratch, act):
    it = iter(refs)
    x_ref = next(it)
    sc_ref = sh_ref = None
    if has_mod:
        sc_ref, sh_ref = next(it), next(it)
    w_refs = [next(it) for _ in range(n_w)]
    b_refs = [next(it) for _ in range(n_w)]
    o_ref = next(it)
    if use_scratch:
        xb_ref = next(it)

        @pl.when(pl.program_id(1) == 0)
        def _():
            xv = x_ref[...].astype(F32)
            if has_mod:
                xv = xv * (1.0 + sc_ref[0]) + sh_ref[0]
            xb_ref[...] = xv.astype(BF16)

        xb = xb_ref[...]
    else:
        xb = x_ref[...]
    z = [_bdot(xb, w_refs[k][...].astype(BF16)) + b_refs[k][...] for k in range(n_w)]
    if act == "glu":
        out = z[0] * _sigmoid(z[1])
    elif act == "gelu":
        out = jax.nn.gelu(z[0])
    else:
        out = z[0]
    o_ref[...] = out.astype(o_ref.dtype)


def _matmul(x, w, b, *, rows_per_batch, mod=None, act=None, out_dtype=F32, tm=1024, tn=512):
    n, k = x.shape
    nw = w.shape[1]
    n_out = nw // 2 if act == "glu" else nw
    tm = _pick(rows_per_batch, tm)
    tn = _pick(n_out, tn)
    tiles_per_batch = rows_per_batch // tm
    has_mod = mod is not None
    use_scratch = has_mod or x.dtype != BF16
    n_w = 2 if act == "glu" else 1
    half = n_out // tn

    in_specs = [pl.BlockSpec((tm, k), lambda i, j: (i, 0))]
    args = [x]
    if has_mod:
        mspec = pl.BlockSpec((1, 1, k), lambda i, j: (i // tiles_per_batch, 0, 0))
        in_specs += [mspec, mspec]
        args += [mod[0], mod[1]]
    b2 = b.reshape(1, nw)
    in_specs.append(pl.BlockSpec((k, tn), lambda i, j: (0, j)))
    args.append(w)
    if n_w == 2:
        in_specs.append(pl.BlockSpec((k, tn), lambda i, j: (0, j + half)))
        args.append(w)
    in_specs.append(pl.BlockSpec((1, tn), lambda i, j: (0, j)))
    args.append(b2)
    if n_w == 2:
        in_specs.append(pl.BlockSpec((1, tn), lambda i, j: (0, j + half)))
        args.append(b2)
    scratch = [pltpu.VMEM((tm, k), BF16)] if use_scratch else []
    return pl.pallas_call(
        functools.partial(_mm_kernel, n_w=n_w, has_mod=has_mod, use_scratch=use_scratch, act=act),
        grid=(n // tm, n_out // tn),
        in_specs=in_specs,
        out_specs=pl.BlockSpec((tm, tn), lambda i, j: (i, j)),
        out_shape=jax.ShapeDtypeStruct((n, n_out), out_dtype),
        scratch_shapes=scratch,
        compiler_params=_cparams(2),
        name="matmul_" + (act or "bias"),
    )(*args)


def _ln_res_kernel(*refs, alpha, emit_h):
    if emit_h:
        x_ref, y_ref, g_ref, gam_ref, bet_ref, sc_ref, sh_ref, o_ref, h_ref = refs
    else:
        x_ref, y_ref, g_ref, gam_ref, bet_ref, o_ref = refs
    r = alpha * x_ref[...] + g_ref[0] * y_ref[...].astype(F32)
    xn = _layer_norm_rows(r, gam_ref[...], bet_ref[...])
    o_ref[...] = xn
    if emit_h:
        h_ref[...] = (xn * (1.0 + sc_ref[0]) + sh_ref[0]).astype(h_ref.dtype)


def _ln_res(x, y, g, gamma, beta, *, alpha, rows_per_batch, mod=None, h_dtype=BF16, tm=512):
    n, d = x.shape
    tm = _pick(rows_per_batch, tm)
    tpb = rows_per_batch // tm
    row = pl.BlockSpec((tm, d), lambda i: (i, 0))
    per_b = pl.BlockSpec((1, 1, d), lambda i: (i // tpb, 0, 0))
    vec = pl.BlockSpec((1, d), lambda i: (0, 0))
    emit_h = mod is not None
    in_specs = [row, row, per_b, vec, vec]
    args = [x, y, g, gamma.reshape(1, d), beta.reshape(1, d)]
    out_shape = jax.ShapeDtypeStruct((n, d), F32)
    out_specs = row
    if emit_h:
        in_specs += [per_b, per_b]
        args += [mod[0], mod[1]]
        out_shape = (out_shape, jax.ShapeDtypeStruct((n, d), h_dtype))
        out_specs = (row, row)
    return pl.pallas_call(
        functools.partial(_ln_res_kernel, alpha=alpha, emit_h=emit_h),
        grid=(n // tm,),
        in_specs=in_specs,
        out_specs=out_specs,
        out_shape=out_shape,
        compiler_params=_cparams(1),
        name="ln_res",
    )(*args)


def _proj_ln_kernel(h_ref, w_ref, b_ref, x_ref, g_ref, gam_ref, bet_ref, sc_ref, sh_ref, o_ref, h2_ref, wb_ref,
                    *, alpha):
    @pl.when(pl.program_id(0) == 0)
    def _():
        wb_ref[...] = w_ref[...].astype(BF16)

    tm = h_ref.shape[0]
    for r0 in range(0, tm, tm // 2):
        rows = slice(r0, r0 + tm // 2)
        y = _bdot(h_ref[rows, :], wb_ref[...]) + b_ref[...]
        xn = _layer_norm_rows(alpha * x_ref[rows, :] + g_ref[0] * y, gam_ref[...], bet_ref[...])
        o_ref[rows, :] = xn
        h2_ref[rows, :] = (xn * (1.0 + sc_ref[0]) + sh_ref[0]).astype(h2_ref.dtype)


def _proj_ln(h, w, b, x, g, gamma, beta, mod, *, alpha, rows_per_batch, h_dtype, tm=256):
    n, k = h.shape
    d = w.shape[1]
    tm = _pick(rows_per_batch, tm)
    tpb = rows_per_batch // tm
    per_b = pl.BlockSpec((1, 1, d), lambda i: (i // tpb, 0, 0))
    vec = pl.BlockSpec((1, d), lambda i: (0, 0))
    row = pl.BlockSpec((tm, d), lambda i: (i, 0))
    return pl.pallas_call(
        functools.partial(_proj_ln_kernel, alpha=alpha),
        grid=(n // tm,),
        in_specs=[pl.BlockSpec((tm, k), lambda i: (i, 0)),
                  pl.BlockSpec((k, d), lambda i: (0, 0), pipeline_mode=pl.Buffered(1)),
                  vec, row, per_b, vec, vec, per_b, per_b],
        out_specs=(row, row),
        out_shape=(jax.ShapeDtypeStruct((n, d), F32), jax.ShapeDtypeStruct((n, d), h_dtype)),
        scratch_shapes=[pltpu.VMEM((k, d), BF16)],
        compiler_params=_cparams(1),
        name="proj_ln",
    )(h, w, b.reshape(1, d), x, g, gamma.reshape(1, d), beta.reshape(1, d), mod[0], mod[1])


def _group_state(tg_ref, nu_ref):
    i = pl.program_id(1)
    active = i < nu_ref[0]
    changed = jnp.logical_or(i == 0, tg_ref[i] != tg_ref[jnp.maximum(i - 1, 0)])
    return active, jnp.logical_and(active, changed)


def _ffn_up_kernel(tg_ref, nu_ref, x_ref, w1_ref, w3_ref, h_ref, w1b_ref, w3b_ref):
    active, recast = _group_state(tg_ref, nu_ref)

    @pl.when(recast)
    def _():
        w1b_ref[...] = w1_ref[0].astype(BF16)
        w3b_ref[...] = w3_ref[0].astype(BF16)

    @pl.when(active)
    def _():
        x = x_ref[...]
        a = _bdot(x, w1b_ref[...])
        b = _bdot(x, w3b_ref[...])
        h_ref[...] = (_silu(a) * b).astype(h_ref.dtype)

    @pl.when(jnp.logical_not(active))
    def _():
        h_ref[...] = jnp.zeros(h_ref.shape, h_ref.dtype)


def _ffn_down_kernel(tg_ref, nu_ref, h_ref, w2_ref, o_ref, w2b_ref):
    active, recast = _group_state(tg_ref, nu_ref)

    @pl.when(recast)
    def _():
        w2b_ref[...] = w2_ref[0].astype(BF16)

    @pl.when(active)
    def _():
        o_ref[...] = _bdot(h_ref[...], w2b_ref[...])

    @pl.when(jnp.logical_not(active))
    def _():
        o_ref[...] = jnp.zeros(o_ref.shape, o_ref.dtype)


def _grouped_ffn(xb, w1, w3, w2, tile_g, n_used, *, tm, tf=512, tm_down=512, tn=512):
    r, d = xb.shape
    ff = w1.shape[2]
    tf = _pick(ff, tf)
    tn = _pick(d, tn)
    tm_down = _pick(tm, tm_down)
    n_tiles = r // tm
    h = pl.pallas_call(
        _ffn_up_kernel,
        grid_spec=pltpu.PrefetchScalarGridSpec(
            num_scalar_prefetch=2,
            grid=(ff // tf, n_tiles),
            in_specs=[pl.BlockSpec((tm, d), lambda f, i, tg, nu: (i, 0)),
                      pl.BlockSpec((1, d, tf), lambda f, i, tg, nu: (tg[i], 0, f)),
                      pl.BlockSpec((1, d, tf), lambda f, i, tg, nu: (tg[i], 0, f))],
            out_specs=pl.BlockSpec((tm, tf), lambda f, i, tg, nu: (i, f)),
            scratch_shapes=[pltpu.VMEM((d, tf), BF16), pltpu.VMEM((d, tf), BF16)]),
        out_shape=jax.ShapeDtypeStruct((r, ff), BF16),
        compiler_params=_cparams(2),
        name="ffn_up",
    )(tile_g, n_used, xb, w1, w3)
    sub = tm // tm_down
    tile_g2 = jnp.repeat(tile_g, sub)
    n_used2 = n_used * sub
    return pl.pallas_call(
        _ffn_down_kernel,
        grid_spec=pltpu.PrefetchScalarGridSpec(
            num_scalar_prefetch=2,
            grid=(d // tn, n_tiles * sub),
            in_specs=[pl.BlockSpec((tm_down, ff), lambda n, i, tg, nu: (i, 0)),
                      pl.BlockSpec((1, ff, tn), lambda n, i, tg, nu: (tg[i], 0, n))],
            out_specs=pl.BlockSpec((tm_down, tn), lambda n, i, tg, nu: (i, n)),
            scratch_shapes=[pltpu.VMEM((ff, tn), BF16)]),
        out_shape=jax.ShapeDtypeStruct((r, d), F32),
        compiler_params=_cparams(2),
        name="ffn_down",
    )(tile_g2, n_used2, h, w2)


def _conv_kernel(x_ref, w_ref, b_ref, g_ref, be_ref, o_ref, xx_ref, u_ref, *, ts, kw, rc, cc):
    i = pl.program_id(1)
    d = x_ref.shape[1]
    halo = CONV_HALO

    @pl.when(i == 0)
    def _():
        xx_ref[0:halo, :] = jnp.zeros((halo, d), F32)

    @pl.when(i > 0)
    def _():
        xx_ref[0:halo, :] = xx_ref[ts:ts + halo, :]

    xx_ref[halo:halo + ts, :] = x_ref[...]
    off = halo - (kw - 1)

    sub = V7X_SUBLANES
    for r0 in range(0, ts, rc):
        for c0 in range(0, d, cc):
            acc = jnp.broadcast_to(b_ref[:, c0:c0 + cc], (rc, cc))
            for rho in range(sub):
                taps = [j for j in range(kw) if (off + j) % sub == rho]
                if not taps:
                    continue
                base = r0 + off + taps[0]
                win = xx_ref[base:base + rc + taps[-1] - taps[0], c0:c0 + cc]
                part = None
                for j in taps:
                    o = j - taps[0]
                    term = w_ref[j:j + 1, c0:c0 + cc] * win[o:o + rc, :]
                    part = term if part is None else part + term
                acc = acc + part
            u_ref[r0:r0 + rc, c0:c0 + cc] = acc
    y = _layer_norm_rows(u_ref[...], g_ref[...], be_ref[...])
    o_ref[...] = _silu(y).astype(o_ref.dtype)


def _conv_ln_swish(glu, w_dw, b_dw, ln_g, ln_b, *, bsz, seq, ts=128):
    n, d = glu.shape
    kw = w_dw.shape[0]
    ts = _pick(seq, ts)
    nt = seq // ts
    rc = _pick(ts, 64)
    cc = _pick(d, 256)
    row = pl.BlockSpec((ts, d), lambda b, i: (b * nt + i, 0))
    vec = pl.BlockSpec((1, d), lambda b, i: (0, 0))
    return pl.pallas_call(
        functools.partial(_conv_kernel, ts=ts, kw=kw, rc=rc, cc=cc),
        grid=(bsz, nt),
        in_specs=[row, pl.BlockSpec((kw, d), lambda b, i: (0, 0)), vec, vec, vec],
        out_specs=row,
        out_shape=jax.ShapeDtypeStruct((n, d), BF16),
        scratch_shapes=[pltpu.VMEM((CONV_HALO + ts, d), F32), pltpu.VMEM((ts, d), F32)],
        compiler_params=_cparams(2),
        name="conv_ln_swish",
    )(glu, w_dw, b_dw.reshape(1, d), ln_g.reshape(1, d), ln_b.reshape(1, d))


def _iota_div(x, k):
    assert k & (k - 1) == 0
    return lax.shift_right_logical(x, k.bit_length() - 1)


def _iota_mod(x, k):
    assert k & (k - 1) == 0
    return lax.bitwise_and(x, k - 1)


def _s5_expand_tables(kq_ref, mq_ref, cq_ref, t_scr, m_scr, c_scr, *, lc, p):
    lanes = t_scr.shape[0] // lc
    ns2 = m_scr.shape[1]
    ns = ns2 // 2
    nst = ns // (lanes // p)
    iota = lambda shape, dim: lax.broadcasted_iota(jnp.int32, shape, dim)
    r, c = iota((lanes, lanes), 0), iota((lanes, lanes), 1)
    same_g = _iota_div(r, p) == _iota_div(c, p)
    e_t = (_iota_mod(iota((p, lanes), 1), p) == iota((p, lanes), 0)).astype(BF16)
    tiles = [jnp.where(same_g, _bdot(kq_ref[0, tau].astype(BF16), e_t), 0.0).astype(BF16) for tau in range(lc)]
    zero_tile = jnp.zeros((lanes, lanes), BF16)
    for s in range(lc):
        for t in range(lc):
            t_scr[s * lanes:(s + 1) * lanes, t * lanes:(t + 1) * lanes] = tiles[t - s] if t >= s else zero_tile
    r, c = iota((lanes, ns2), 0), iota((lanes, ns2), 1)
    e_m = (r == _iota_div(c, ns) * nst + _iota_mod(c, nst)).astype(BF16)
    mask_m = _iota_div(r, p) == _iota_div(_iota_mod(c, ns), nst)
    for s in range(lc):
        m_scr[s * lanes:(s + 1) * lanes, :] = jnp.where(
            mask_m, _bdot(mq_ref[0, s].astype(BF16), e_m), 0.0).astype(BF16)
    r, c = iota((ns2, lanes), 0), iota((ns2, lanes), 1)
    e_c = (c == _iota_div(r, ns) * nst + _iota_mod(r, nst)).astype(BF16)
    mask_c = _iota_div(_iota_mod(r, ns), nst) == _iota_div(c, p)
    for t in range(lc):
        c_scr[:, t * lanes:(t + 1) * lanes] = jnp.where(
            mask_c, _bdot(e_c, cq_ref[0, t].astype(BF16)), 0.0).astype(BF16)


def _s5_kernel(u_ref, kq_ref, mq_ref, cq_ref, lp_ref, d_ref, o_ref,
               t_scr, m_scr, c_scr, a_ref, v_ref, p_ref, *, lc, nc, p):
    @pl.when(pl.program_id(1) == 0)
    def _():
        _s5_expand_tables(kq_ref, mq_ref, cq_ref, t_scr, m_scr, c_scr, lc=lc, p=p)

    lanes = u_ref.shape[2]
    for t in range(lc):
        a_ref[:, t * lanes:(t + 1) * lanes] = u_ref[0, pl.ds(t, nc, stride=lc), :]
    a = a_ref[...]
    ab = a.astype(BF16)
    v_ref[...] = _bdot(ab, m_scr[...])
    pw_r = lp_ref[0, 0]
    pw_i = lp_ref[0, 1]
    ns = pw_r.shape[1]

    slab = V7X_SUBLANES
    srow = lax.broadcasted_iota(jnp.int32, (slab, ns), 0)

    def step(k, carry):
        cr, ci = carry
        r0 = pl.multiple_of(k * slab, slab)
        yr = v_ref[pl.ds(r0, slab), 0:ns]
        yi = v_ref[pl.ds(r0, slab), ns:2 * ns]
        sh = 1
        while sh < slab:
            mr, mi = pw_r[sh - 1:sh, :], pw_i[sh - 1:sh, :]
            zr = jnp.where(srow >= sh, pltpu.roll(yr, sh, 0), 0.0)
            zi = jnp.where(srow >= sh, pltpu.roll(yi, sh, 0), 0.0)
            yr, yi = yr + mr * zr - mi * zi, yi + mr * zi + mi * zr
            sh *= 2
        sr = yr + pw_r * cr - pw_i * ci
        si = yi + pw_r * ci + pw_i * cr
        p_ref[pl.ds(r0, slab), 0:ns] = jnp.where(srow == 0, cr, pltpu.roll(sr, 1, 0))
        p_ref[pl.ds(r0, slab), ns:2 * ns] = jnp.where(srow == 0, ci, pltpu.roll(si, 1, 0))
        return sr[slab - 1:slab, :], si[slab - 1:slab, :]

    zero = jnp.zeros((1, ns), F32)
    lax.fori_loop(0, nc // slab, step, (zero, zero))
    y = _bdot(ab, t_scr[...]) + _bdot(p_ref[...].astype(BF16), c_scr[...]) + d_ref[0] * a
    y = jax.nn.gelu(y)
    for t in range(lc):
        o_ref[0, pl.ds(t, nc, stride=lc), :] = y[:, t * lanes:(t + 1) * lanes]


def _s5_tables(a_re, a_im, log_dt, b_re, b_im, c_re, c_im, d_skip):
    g, n = a_re.shape
    p = b_re.shape[2]
    lc = S5_CHUNK
    ar, ai = a_re.astype(F32), a_im.astype(F32)
    dt = jnp.exp(log_dt.astype(F32))[:, None]
    decay = jnp.exp(ar * dt)
    lr, li = decay * jnp.cos(ai * dt), decay * jnp.sin(ai * dt)
    den = ar * ar + ai * ai
    zr = ((lr - 1.0) * ar + li * ai) / den
    zi = (li * ar - (lr - 1.0) * ai) / den
    br, bi = b_re.astype(F32), b_im.astype(F32)
    bbr = zr[..., None] * br - zi[..., None] * bi
    bbi = zr[..., None] * bi + zi[..., None] * br
    tau = jnp.arange(lc + 1, dtype=F32)[:, None, None]
    pdec = jnp.exp(tau * (ar * dt)[None])
    pr, pi = pdec * jnp.cos(tau * (ai * dt)[None]), pdec * jnp.sin(tau * (ai * dt)[None])
    cr, ci = c_re.astype(F32), c_im.astype(F32)
    hp = lax.Precision.HIGHEST
    lbr = pr[..., None] * bbr[None] - pi[..., None] * bbi[None]
    lbi = pr[..., None] * bbi[None] + pi[..., None] * bbr[None]
    ktau = (jnp.einsum('gpn,tgnq->tgpq', cr, lbr[:lc], precision=hp)
            - jnp.einsum('gpn,tgnq->tgpq', ci, lbi[:lc], precision=hp))
    gb = V7X_LANES // p
    nj = g // gb
    kq = ktau.reshape(lc, nj, gb, p, p).transpose(1, 0, 2, 4, 3).reshape(nj, lc, gb * p, p)
    rev = lc - 1 - jnp.arange(lc)
    mq = jnp.stack([lbr[rev], lbi[rev]], axis=2).reshape(lc, nj, gb, 2, n, p)
    mq = mq.transpose(1, 0, 2, 5, 3, 4).reshape(nj, lc, gb * p, 2 * n)
    pr1, pi1 = pr[1:], pi[1:]
    cmr = (cr[None] * pr1[:, :, None, :] - ci[None] * pi1[:, :, None, :])
    cmi = -(cr[None] * pi1[:, :, None, :] + ci[None] * pr1[:, :, None, :])
    cq = jnp.stack([cmr, cmi], axis=0).reshape(2, lc, nj, gb, p, n)
    cq = cq.transpose(2, 1, 0, 5, 3, 4).reshape(nj, lc, 2 * n, gb * p)
    mpw = (lc * jnp.arange(1, V7X_SUBLANES + 1, dtype=F32))[:, None, None]
    cdec = jnp.exp(mpw * (ar * dt)[None])
    lam_pw = jnp.stack([cdec * jnp.cos(mpw * (ai * dt)[None]), cdec * jnp.sin(mpw * (ai * dt)[None])], axis=0)
    lam_pw = lam_pw.reshape(2, V7X_SUBLANES, nj, gb * n).transpose(2, 0, 1, 3)
    dvec = jnp.broadcast_to(d_skip.astype(F32).reshape(nj, 1, 1, gb * p), (nj, 1, lc, gb * p))
    return kq, mq, cq, lam_pw, dvec.reshape(nj, 1, lc * gb * p)


def _s5_core(u, tables, *, bsz, seq):
    n_rows, d = u.shape
    kq, mq, cq, lam_pw, dvec = tables
    nj, lc, lanes, p = kq.shape
    ns = lam_pw.shape[3]
    nc = seq // lc
    wk = lc * lanes
    assert lanes == V7X_LANES and nc % V7X_SUBLANES == 0 and nj * lanes == d
    tab = lambda *shape: pl.BlockSpec((1,) + shape, lambda j, b: (j,) + (0,) * len(shape))
    seq_blk = pl.BlockSpec((1, seq, lanes), lambda j, b: (b, 0, j))
    y = pl.pallas_call(
        functools.partial(_s5_kernel, lc=lc, nc=nc, p=p),
        grid=(nj, bsz),
        in_specs=[seq_blk, tab(lc, lanes, p), tab(lc, lanes, lanes), tab(lc, lanes, lanes),
                  tab(2, V7X_SUBLANES, ns), tab(1, wk)],
        out_specs=seq_blk,
        out_shape=jax.ShapeDtypeStruct((bsz, seq, d), F32),
        scratch_shapes=[pltpu.VMEM((wk, wk), BF16), pltpu.VMEM((wk, 2 * ns), BF16), pltpu.VMEM((2 * ns, wk), BF16),
                        pltpu.VMEM((nc, wk), F32), pltpu.VMEM((nc, 2 * ns), F32), pltpu.VMEM((nc, 2 * ns), F32)],
        compiler_params=_cparams(2),
        name="s5_core",
    )(u.reshape(bsz, seq, d), kq, mq, cq, lam_pw, dvec)
    return y.reshape(n_rows, d)


def _log_sigmoid(x):
    return jnp.minimum(x, 0.0) - jnp.log(1.0 + jnp.exp(-jnp.abs(x)))


def _mlstm_kernel(q_ref, k_ref, v_ref, o_ref, g_ref, cw_ref, cb_ref, mh_ref, out_ref,
                  c_st, n_st, m_st, qx_ref, kx_ref, *, lm, dk, heads, kw):
    c = pl.program_id(1)
    pad = V7X_SUBLANES
    hdk = heads * dk
    dv = v_ref.shape[1] // heads

    @pl.when(c == 0)
    def _():
        c_st[...] = jnp.zeros(c_st.shape, F32)
        n_st[...] = jnp.zeros(n_st.shape, F32)
        m_st[...] = jnp.zeros(m_st.shape, F32)
        qx_ref[0:pad, :] = jnp.zeros((pad, hdk), F32)
        kx_ref[0:pad, :] = jnp.zeros((pad, hdk), F32)

    def conv_swish(x_ref, xx_ref, c0):
        xx_ref[pad:pad + lm, :] = x_ref[...]
        acc = jnp.broadcast_to(cb_ref[:, c0:c0 + hdk], (lm, hdk))
        for j in range(kw):
            acc = acc + cw_ref[j:j + 1, c0:c0 + hdk] * xx_ref[pad - (kw - 1) + j:pad - (kw - 1) + j + lm, :]
        xx_ref[0:pad, :] = xx_ref[lm:lm + pad, :]
        return _silu(acc)

    q_all = conv_swish(q_ref, qx_ref, 0)
    k_all = conv_swish(k_ref, kx_ref, hdk) * (dk ** -0.5)

    gts = g_ref[...]
    lf = _log_sigmoid(gts)
    rows = lax.broadcasted_iota(jnp.int32, (lm, lm), 0)
    cols = lax.broadcasted_iota(jnp.int32, (lm, lm), 1)
    causal = rows >= cols
    tri = causal.astype(BF16)
    lf_hi = lf.astype(BF16)
    r1 = lf - lf_hi.astype(F32)
    lf_mid = r1.astype(BF16)
    lf_lo = (r1 - lf_mid.astype(F32)).astype(BF16)
    bcum = _bdot(tri, lf_hi) + _bdot(tri, lf_mid) + _bdot(tri, lf_lo)

    bcum_t = bcum.T
    gts_t = gts.T
    for h in range(heads):
        q = q_all[:, h * dk:(h + 1) * dk]
        k = k_all[:, h * dk:(h + 1) * dk]
        vcols = slice(h * dv, (h + 1) * dv)
        b_col = bcum[:, heads + h:heads + h + 1]
        li_col = gts[:, h:h + 1]
        b_row = bcum_t[heads + h:heads + h + 1, :]
        li_row = gts_t[h:h + 1, :]

        m_prev = m_st[h]
        dmat = jnp.where(causal, b_col - b_row + li_row, -jnp.inf)
        inter = b_col + m_prev
        m_j = jnp.maximum(inter, jnp.max(dmat, axis=1, keepdims=True))
        w_intra = jnp.exp(dmat - m_j)
        w_inter = jnp.exp(inter - m_j)
        qb, kb, vb = q.astype(BF16), k.astype(BF16), v_ref[:, vcols].astype(BF16)
        sc = lax.dot_general(qb, kb, (((1,), (1,)), ((), ())), preferred_element_type=F32) * w_intra
        num = w_inter * _bdot(qb, c_st[h].astype(BF16)) + _bdot(sc.astype(BF16), vb)
        den = w_inter * jnp.sum(q * n_st[h], axis=1, keepdims=True) + jnp.sum(sc, axis=1, keepdims=True)
        hb = num / jnp.maximum(jnp.abs(den), jnp.exp(-m_j))

        bl = b_col[lm - 1:lm, :]
        gsum = bl - b_col + li_col
        m_new = jnp.maximum(bl + m_prev, jnp.max(gsum, axis=0, keepdims=True))
        wc = jnp.exp(bl + m_prev - m_new)
        wk = jnp.exp(gsum - m_new)
        kwt = wk * k
        c_st[h] = wc * c_st[h] + _bdot(kwt.T.astype(BF16), vb)
        n_st[h] = wc * n_st[h] + jnp.sum(kwt, axis=0, keepdims=True)
        m_st[h] = m_new

        hs = _sigmoid(o_ref[:, vcols]) * hb
        out_ref[:, vcols] = (_layer_norm_rows(hs, None, None) * mh_ref[:, vcols]).astype(out_ref.dtype)


def _mlstm_core(z, w_conv, b_conv, mh_g, *, bsz, seq, d):
    n_rows = z.shape[0]
    heads = ML_HEADS
    dv = d // heads
    dk = dv // 2
    lm = _pick(seq, ML_CHUNK_ROWS)
    nc = seq // lm
    kw = w_conv.shape[0]
    hdk = heads * dk
    hdv = heads * dv
    assert 2 * hdk == hdv
    gate_blk = (2 * hdk + 2 * hdv) // V7X_LANES
    row = lambda b, c: b * nc + c
    in_specs = [
        pl.BlockSpec((lm, hdk), lambda b, c: (row(b, c), 0)),
        pl.BlockSpec((lm, hdk), lambda b, c: (row(b, c), 1)),
        pl.BlockSpec((lm, hdv), lambda b, c: (row(b, c), 1)),
        pl.BlockSpec((lm, hdv), lambda b, c: (row(b, c), 2)),
        pl.BlockSpec((lm, V7X_LANES), lambda b, c: (row(b, c), gate_blk)),
        pl.BlockSpec((kw, 2 * hdk), lambda b, c: (0, 0)),
        pl.BlockSpec((1, 2 * hdk), lambda b, c: (0, 0)),
        pl.BlockSpec((1, d), lambda b, c: (0, 0)),
    ]
    return pl.pallas_call(
        functools.partial(_mlstm_kernel, lm=lm, dk=dk, heads=heads, kw=kw),
        grid=(bsz, nc),
        in_specs=in_specs,
        out_specs=pl.BlockSpec((lm, d), lambda b, c: (row(b, c), 0)),
        out_shape=jax.ShapeDtypeStruct((n_rows, d), BF16),
        scratch_shapes=[pltpu.VMEM((heads, dk, dv), F32), pltpu.VMEM((heads, 1, dk), F32),
                        pltpu.VMEM((heads, 1, 1), F32),
                        pltpu.VMEM((lm + V7X_SUBLANES, hdk), F32), pltpu.VMEM((lm + V7X_SUBLANES, hdk), F32)],
        compiler_params=_cparams(2),
        name="mlstm_core",
    )(z, z, z, z, z, w_conv, b_conv.reshape(1, -1), mh_g.reshape(1, d))


def _gmlp_kernel(u_ref, v_ref, g_ref, be_ref, w_ref, bsp_ref, o_ref, *, lg, groups, dg, nsub):
    rows = lax.broadcasted_iota(jnp.int32, (lg, lg), 0)
    cols = lax.broadcasted_iota(jnp.int32, (lg, lg), 1)
    keep = rows >= cols
    wm = [jnp.where(keep, w_ref[g], 0.0).astype(BF16) for g in range(groups)]
    for s in range(nsub):
        r = slice(s * lg, (s + 1) * lg)
        vn = _layer_norm_rows(v_ref[r, :], g_ref[...], be_ref[...]).astype(BF16)
        for g in range(groups):
            cs = slice(g * dg, (g + 1) * dg)
            sv = _bdot(wm[g], vn[:, cs]) + bsp_ref[:, g:g + 1]
            o_ref[r, cs] = (u_ref[r, cs] * sv).astype(o_ref.dtype)


def _gmlp_core(zz, ln_g, ln_b, w_sp, b_sp, *, seq, tm=512):
    n, d2 = zz.shape
    d = d2 // 2
    groups, lg, _ = w_sp.shape
    tm = max(lg, _pick(seq, tm))
    return pl.pallas_call(
        functools.partial(_gmlp_kernel, lg=lg, groups=groups, dg=d // groups, nsub=tm // lg),
        grid=(n // tm,),
        in_specs=[pl.BlockSpec((tm, d), lambda i: (i, 0)),
                  pl.BlockSpec((tm, d), lambda i: (i, 1)),
                  pl.BlockSpec((1, d), lambda i: (0, 0)),
                  pl.BlockSpec((1, d), lambda i: (0, 0)),
                  pl.BlockSpec((groups, lg, lg), lambda i: (0, 0, 0)),
                  pl.BlockSpec((lg, groups), lambda i: (0, 0))],
        out_specs=pl.BlockSpec((tm, d), lambda i: (i, 0)),
        out_shape=jax.ShapeDtypeStruct((n, d), BF16),
        compiler_params=_cparams(1),
        name="gmlp_spatial",
    )(zz, zz, ln_g.reshape(1, d), ln_b.reshape(1, d), w_sp, b_sp.T)


def _router_kernel(h_ref, w_ref, e_ref, g_ref, *, n_exp):
    h = h_ref[...]
    w = w_ref[...]
    hh = h.astype(BF16)
    hl = (h - hh.astype(F32)).astype(BF16)
    wh = w.astype(BF16)
    wl = (w - wh.astype(F32)).astype(BF16)
    logits = _bdot(hh, wh) + _bdot(hl, wh) + _bdot(hh, wl)
    nl = logits.shape[1]
    lane = lax.broadcasted_iota(jnp.int32, logits.shape, 1).astype(F32)
    cur = jnp.where(lane < n_exp, logits, -jnp.inf)
    vals, idxs = [], []
    for _ in range(TOP_K):
        m = jnp.max(cur, axis=1, keepdims=True)
        idx = jnp.min(jnp.where(cur == m, lane, float(nl)), axis=1, keepdims=True)
        vals.append(m)
        idxs.append(idx)
        cur = jnp.where(lane == idx, -jnp.inf, cur)
    ex = [jnp.exp(v - vals[0]) for v in vals]
    tot = ex[0]
    for t in ex[1:]:
        tot = tot + t
    e_out = jnp.zeros(logits.shape, F32)
    g_out = jnp.zeros(logits.shape, F32)
    for k in range(TOP_K):
        e_out = jnp.where(lane == k, idxs[k], e_out)
        g_out = jnp.where(lane == k, ex[k] / tot, g_out)
    e_ref[...] = e_out.astype(jnp.int32)
    g_ref[...] = g_out


def _router_topk(hf, w_router, *, tm=1024):
    n, d = hf.shape
    e = w_router.shape[1]
    wpad = jnp.pad(w_router, ((0, 0), (0, V7X_LANES - e)))
    tm = _pick(n, tm)
    out_blk = pl.BlockSpec((tm, V7X_LANES), lambda i: (i, 0))
    top_e, gates = pl.pallas_call(
        functools.partial(_router_kernel, n_exp=e),
        grid=(n // tm,),
        in_specs=[pl.BlockSpec((tm, d), lambda i: (i, 0)),
                  pl.BlockSpec((d, V7X_LANES), lambda i: (0, 0))],
        out_specs=(out_blk, out_blk),
        out_shape=(jax.ShapeDtypeStruct((n, V7X_LANES), jnp.int32), jax.ShapeDtypeStruct((n, V7X_LANES), F32)),
        compiler_params=_cparams(1),
        name="moe_router",
    )(hf, wpad)
    return top_e[:, :TOP_K], gates[:, :TOP_K]


def _moe_gather_kernel(nu_ref, tok_ref, tokn_ref, hf_hbm, o_ref, xg_ref, sem, *, tm):
    i = pl.program_id(0)
    nu = nu_ref[0]
    slot = lax.rem(i, 2)

    def row_copy(tref, r, s):
        return pltpu.make_async_copy(hf_hbm.at[pl.ds(tref[0, 0, r], 1), :], xg_ref.at[s, pl.ds(r, 1), :], sem.at[s])

    def start_tile(tref, s):
        def issue(r, carry):
            row_copy(tref, r, s).start()
            return carry

        lax.fori_loop(0, tm, issue, 0, unroll=DMA_UNROLL)

    @pl.when(jnp.logical_and(i == 0, nu > 0))
    def _():
        start_tile(tok_ref, 0)

    @pl.when(i + 1 < nu)
    def _():
        start_tile(tokn_ref, 1 - slot)

    @pl.when(i < nu)
    def _():
        def wait(r, carry):
            row_copy(tok_ref, r, slot).wait()
            return carry

        lax.fori_loop(0, tm, wait, 0, unroll=DMA_UNROLL)
        o_ref[...] = xg_ref[slot].astype(o_ref.dtype)

    @pl.when(i >= nu)
    def _():
        o_ref[...] = jnp.zeros(o_ref.shape, o_ref.dtype)


def _moe_gather(hf, row_tok, n_used, *, tm):
    n, d = hf.shape
    n_tiles = row_tok.shape[0]
    tok_spec = lambda fn: pl.BlockSpec((1, 1, tm), fn, memory_space=pltpu.SMEM)
    return pl.pallas_call(
        functools.partial(_moe_gather_kernel, tm=tm),
        grid_spec=pltpu.PrefetchScalarGridSpec(
            num_scalar_prefetch=1,
            grid=(n_tiles,),
            in_specs=[tok_spec(lambda i, nu: (i, 0, 0)),
                      tok_spec(lambda i, nu: (jnp.minimum(i + 1, n_tiles - 1), 0, 0)),
                      pl.BlockSpec(memory_space=pl.ANY)],
            out_specs=pl.BlockSpec((tm, d), lambda i, nu: (i, 0)),
            scratch_shapes=[pltpu.VMEM((2, tm, d), hf.dtype), pltpu.SemaphoreType.DMA((2,))]),
        out_shape=jax.ShapeDtypeStruct((n_tiles * tm, d), BF16),
        compiler_params=_cparams(1),
        name="moe_gather",
    )(n_used, row_tok, row_tok, hf)


def _moe_combine_kernel(*refs, tm, alpha, emit_h):
    if emit_h:
        (pos_ref, posn_ref, y_hbm, x_ref, gate_ref, g_ref, gam_ref, bet_ref, sc_ref, sh_ref,
         o_ref, h_ref, yb_ref, sem) = refs
    else:
        pos_ref, posn_ref, y_hbm, x_ref, gate_ref, g_ref, gam_ref, bet_ref, o_ref, yb_ref, sem = refs
    i = pl.program_id(0)
    nt = pl.num_programs(0)
    slot = lax.rem(i, 2)

    def row_copy(pref, k, r, s):
        return pltpu.make_async_copy(y_hbm.at[pl.ds(pref[0, 0, k * tm + r], 1), :],
                                     yb_ref.at[s, k, pl.ds(r, 1), :], sem.at[s])

    def start_tile(pref, s):
        def issue(r, carry):
            for k in range(TOP_K):
                row_copy(pref, k, r, s).start()
            return carry

        lax.fori_loop(0, tm, issue, 0, unroll=DMA_UNROLL)

    @pl.when(i == 0)
    def _():
        start_tile(pos_ref, 0)

    @pl.when(i + 1 < nt)
    def _():
        start_tile(posn_ref, 1 - slot)

    def wait(r, carry):
        for k in range(TOP_K):
            row_copy(pos_ref, k, r, slot).wait()
        return carry

    lax.fori_loop(0, tm, wait, 0, unroll=DMA_UNROLL)
    gate = gate_ref[...]
    y = gate[:, 0:1] * yb_ref[slot, 0]
    for k in range(1, TOP_K):
        y = y + gate[:, k:k + 1] * yb_ref[slot, k]
    r = alpha * x_ref[...] + g_ref[0] * y
    xn = _layer_norm_rows(r, gam_ref[...], bet_ref[...])
    o_ref[...] = xn
    if emit_h:
        h_ref[...] = (xn * (1.0 + sc_ref[0]) + sh_ref[0]).astype(h_ref.dtype)


def _moe_combine_ln(y_rows, pos, gates, x, g, gamma, beta, *, alpha, rows_per_batch, mod=None, tm=256):
    n, d = x.shape
    tm = _pick(rows_per_batch, tm)
    tpb = rows_per_batch // tm
    nt = n // tm
    pos3 = pos.reshape(nt, tm, TOP_K).transpose(0, 2, 1).reshape(nt, 1, TOP_K * tm)
    row = pl.BlockSpec((tm, d), lambda i: (i, 0))
    vec = pl.BlockSpec((1, d), lambda i: (0, 0))
    per_b = pl.BlockSpec((1, 1, d), lambda i: (i // tpb, 0, 0))
    emit_h = mod is not None
    in_specs = [pl.BlockSpec((1, 1, TOP_K * tm), lambda i: (i, 0, 0), memory_space=pltpu.SMEM),
                pl.BlockSpec((1, 1, TOP_K * tm), lambda i: (jnp.minimum(i + 1, nt - 1), 0, 0),
                             memory_space=pltpu.SMEM),
                pl.BlockSpec(memory_space=pl.ANY),
                row,
                pl.BlockSpec((tm, TOP_K), lambda i: (i, 0)),
                per_b, vec, vec]
    args = [pos3, pos3, y_rows, x, gates, g, gamma.reshape(1, d), beta.reshape(1, d)]
    out_specs, out_shape = row, jax.ShapeDtypeStruct((n, d), F32)
    if emit_h:
        in_specs += [per_b, per_b]
        args += [mod[0], mod[1]]
        out_specs = (row, row)
        out_shape = (out_shape, jax.ShapeDtypeStruct((n, d), BF16))
    return pl.pallas_call(
        functools.partial(_moe_combine_kernel, tm=tm, alpha=alpha, emit_h=emit_h),
        grid=(nt,),
        in_specs=in_specs,
        out_specs=out_specs,
        out_shape=out_shape,
        scratch_shapes=[pltpu.VMEM((2, TOP_K, tm, d), F32), pltpu.SemaphoreType.DMA((2,))],
        compiler_params=_cparams(1),
        name="moe_combine_ln",
    )(*args)


def _route(top_e, e, tm):
    n = top_e.shape[0]
    e_flat = top_e.reshape(-1).astype(jnp.int32)
    onehot = (e_flat[:, None] == jnp.arange(e, dtype=jnp.int32)[None, :]).astype(jnp.int32)
    cum = jnp.cumsum(onehot, axis=0)
    rank = jnp.take_along_axis(cum, e_flat[:, None], axis=1)[:, 0] - 1
    counts = cum[-1]
    padded = (counts + tm - 1) // tm * tm
    pend = jnp.cumsum(padded)
    pstart = pend - padded
    dest = (pstart[e_flat] + rank).astype(jnp.int32)
    n_tiles = (n * TOP_K) // tm + e
    tok_flat = jnp.repeat(jnp.arange(n, dtype=jnp.int32), TOP_K)
    row_tok = jnp.zeros((n_tiles * tm,), jnp.int32).at[dest].set(tok_flat)
    n_used = (pend[-1] // tm).astype(jnp.int32)
    tidx = jnp.arange(n_tiles, dtype=jnp.int32)
    tile_e = jnp.minimum(jnp.searchsorted(pend, tidx * tm, side='right'), e - 1).astype(jnp.int32)
    tile_e = jnp.where(tidx < n_used, tile_e, tile_e[jnp.maximum(n_used - 1, 0)])
    return dest.reshape(n, TOP_K), row_tok.reshape(n_tiles, 1, tm), tile_e, n_used.reshape(1)


def _moe_layer(x, hf, g2, gamma, beta, w_router, w1, w3, w2, layer_idx, *, alpha, rows_per_batch, mod_next=None,
               tm=1024):
    n = hf.shape[0]
    tm = _pick(n * TOP_K, tm)
    top_e, gates = _router_topk(hf, w_router)
    pos, row_tok, tile_e, n_used = _route(top_e, w_router.shape[1], tm)
    xs = _moe_gather(hf, row_tok, n_used, tm=tm)
    y_rows = _grouped_ffn(xs, w1, w3, w2, tile_e + layer_idx * N_EXPERTS, n_used, tm=tm)
    return _moe_combine_ln(y_rows, pos, gates, x, g2, gamma, beta, alpha=alpha, rows_per_batch=rows_per_batch,
                           mod=mod_next)


def kernel(x, c, ada_w, ada_b, ln1_g, ln1_b, ln2_g, ln2_b, a_w_in, a_b_in, a_w_dw, a_b_dw, a_ln_g, a_ln_b, a_w_out, a_b_out, b_w_in, b_b_in, b_a_re, b_a_im, b_log_dt, b_b_re, b_b_im, b_c_re, b_c_im, b_d, b_w_glu, b_b_glu, c_w_in, c_b_in, c_w_conv, c_b_conv, c_mh_g, c_w_out, c_b_out, d_w_in, d_b_in, d_ln_g, d_ln_b, d_w_sp, d_b_sp, d_w_out, d_b_out, f_w1, f_w3, f_w2, m_router, m_w1, m_w3, m_w2):
    bsz, seq, d = x.shape
    depth = ada_w.shape[0]
    alpha = (2.0 * depth) ** 0.25
    n = bsz * seq
    mm = functools.partial(_matmul, rows_per_batch=seq)

    mods = _ada_mods(c, ada_w, ada_b)
    xf = x.reshape(n, d)
    ia = ib = ic = idd = 0
    i_dense = i_moe = 0
    hm = None
    mod_of = lambda l: [mods[l, :, k * d:(k + 1) * d].reshape(bsz, 1, d) for k in range(6)]
    for layer in range(depth):
        sh1, sc1, g1, sh2, sc2, g2 = mod_of(layer)
        mod_next = None
        if layer + 1 < depth:
            sh1n, sc1n = mod_of(layer + 1)[:2]
            mod_next = (sc1n, sh1n)

        def mix_in(w, b, xf=xf, hm=hm, sc1=sc1, sh1=sh1, tm_bf16=2048, **kw):
            if hm is None:
                return mm(xf, w, b, mod=(sc1, sh1), **kw)
            return mm(hm, w, b, tm=tm_bf16, **kw)

        kind = layer % N_MIXERS
        is_moe = layer % 2 == 1
        h_dtype = F32 if is_moe else BF16
        proj_ln = functools.partial(_proj_ln, x=xf, g=g1, gamma=ln1_g[layer], beta=ln1_b[layer], mod=(sc2, sh2),
                                    alpha=alpha, rows_per_batch=seq, h_dtype=h_dtype)
        if kind == 0:
            glu = mix_in(a_w_in[ia], a_b_in[ia], act="glu")
            u = _conv_ln_swish(glu, a_w_dw[ia], a_b_dw[ia], a_ln_g[ia], a_ln_b[ia], bsz=bsz, seq=seq)
            xf, hf = proj_ln(u, a_w_out[ia], a_b_out[ia])
            ia += 1
        elif kind == 1:
            u = mix_in(b_w_in[ib], b_b_in[ib])
            tables = _s5_tables(b_a_re[ib], b_a_im[ib], b_log_dt[ib], b_b_re[ib], b_b_im[ib],
                                b_c_re[ib], b_c_im[ib], b_d[ib])
            yg = _s5_core(u, tables, bsz=bsz, seq=seq)
            y = mm(yg, b_w_glu[ib], b_b_glu[ib], act="glu")
            xf, hf = _ln_res(xf, y, g1, ln1_g[layer], ln1_b[layer], alpha=alpha, rows_per_batch=seq,
                             mod=(sc2, sh2), h_dtype=h_dtype)
            ib += 1
        elif kind == 2:
            heads = ML_HEADS
            dv = d // heads
            n_main = 2 * heads * (dv // 2) + 2 * heads * dv
            w_in = jnp.pad(c_w_in[ic], ((0, 0), (0, V7X_LANES - 2 * heads)))
            b_in = jnp.pad(c_b_in[ic], ((0, V7X_LANES - 2 * heads),))
            assert w_in.shape[1] == n_main + V7X_LANES
            z = mix_in(w_in, b_in, tn=896, tm_bf16=1024)
            hs = _mlstm_core(z, c_w_conv[ic], c_b_conv[ic], c_mh_g[ic], bsz=bsz, seq=seq, d=d)
            xf, hf = proj_ln(hs, c_w_out[ic], c_b_out[ic])
            ic += 1
        else:
            zz = mix_in(d_w_in[idd], d_b_in[idd], act="gelu")
            gated = _gmlp_core(zz, d_ln_g[idd], d_ln_b[idd], d_w_sp[idd], d_b_sp[idd], seq=seq)
            xf, hf = proj_ln(gated, d_w_out[idd], d_b_out[idd])
            idd += 1
        if not is_moe:
            tm_ffn = _pick(n, 1024)
            tile_g = jnp.full((n // tm_ffn,), i_dense, jnp.int32)
            n_used = jnp.full((1,), n // tm_ffn, jnp.int32)
            y = _grouped_ffn(hf, f_w1, f_w3, f_w2, tile_g, n_used, tm=tm_ffn)
            res = _ln_res(xf, y, g2, ln2_g[layer], ln2_b[layer], alpha=alpha, rows_per_batch=seq, mod=mod_next)
            i_dense += 1
        else:
            merge = lambda w: w.reshape((w.shape[0] * w.shape[1],) + w.shape[2:])
            res = _moe_layer(xf, hf, g2, ln2_g[layer], ln2_b[layer], m_router[i_moe], merge(m_w1),
                             merge(m_w3), merge(m_w2), i_moe, alpha=alpha, rows_per_batch=seq, mod_next=mod_next)
            i_moe += 1
        xf, hm = res if mod_next is not None else (res, None)
    return xf.reshape(bsz, seq, d)
```

```python
import functools
import math

import jax
import jax.numpy as jnp
from jax import lax
from jax.experimental import pallas as pl
from jax.experimental.pallas import tpu as pltpu

F32 = jnp.float32
BF16 = jnp.bfloat16

LN_EPS = 1e-5
N_MIXERS = 4
CONV_WIDTH = 31
S5_GROUP = 16
S5_STATE = 64
ML_HEADS = 8
ML_CONV = 4
GM_CHUNK = 128
GM_GROUPS = 8
N_EXPERTS = 8
TOP_K = 2

V7X_VMEM_BYTES = 64 * 1024 * 1024
V7X_LANES = 128
V7X_SUBLANES = 8
VMEM_LIMIT = V7X_VMEM_BYTES - 8 * 1024 * 1024

S5_CHUNK = 16
ML_CHUNK_ROWS = 256
CONV_HALO = 32
DMA_UNROLL = 8


def _cparams(n_axes):
    return pltpu.CompilerParams(dimension_semantics=("arbitrary",) * n_axes,
                                vmem_limit_bytes=VMEM_LIMIT)


def _pick(n, pref):
    t = min(n, pref)
    while n % t:
        t //= 2
    return t


def _sigmoid(x):
    return 1.0 / (1.0 + jnp.exp(-x))


def _silu(x):
    return x * _sigmoid(x)


def _layer_norm_rows(v, gamma, beta):
    mu = jnp.mean(v, axis=-1, keepdims=True)
    d = v - mu
    var = jnp.mean(d * d, axis=-1, keepdims=True)
    y = d * lax.rsqrt(var + LN_EPS)
    if gamma is not None:
        y = y * gamma
    if beta is not None:
        y = y + beta
    return y


def _bdot(a, b):
    return jnp.dot(a, b, preferred_element_type=F32)


def _ada_kernel(c_ref, w_ref, b_ref, o_ref):
    cond = _silu(c_ref[...]).astype(BF16)
    o_ref[0] = _bdot(cond, w_ref[0].astype(BF16)) + b_ref[0]


def _ada_mods(c, ada_w, ada_b):
    depth, d, d6 = ada_w.shape
    bsz = c.shape[0]
    tn = _pick(d6, 1024)
    return pl.pallas_call(
        _ada_kernel,
        grid=(depth, d6 // tn),
        in_specs=[pl.BlockSpec((bsz, d), lambda l, j: (0, 0)),
                  pl.BlockSpec((1, d, tn), lambda l, j: (l, 0, j)),
                  pl.BlockSpec((1, 1, tn), lambda l, j: (l, 0, j))],
        out_specs=pl.BlockSpec((1, bsz, tn), lambda l, j: (l, 0, j)),
        out_shape=jax.ShapeDtypeStruct((depth, bsz, d6), F32),
        compiler_params=_cparams(2),
        name="ada_mods",
    )(c, ada_w, ada_b.reshape(depth, 1, d6))


def _mm_kernel(*refs, n_w, has_mod, use_scratch, act):
    it = iter(refs)
    x_ref = next(it)
    sc_ref = sh_ref = None
    if has_mod:
        sc_ref, sh_ref = next(it), next(it)
    w_refs = [next(it) for _ in range(n_w)]
    b_refs = [next(it) for _ in range(n_w)]
    o_ref = next(it)
    if use_scratch:
        xb_ref = next(it)

        @pl.when(pl.program_id(1) == 0)
        def _():
            xv = x_ref[...].astype(F32)
            if has_mod:
                xv = xv * (1.0 + sc_ref[0]) + sh_ref[0]
            xb_ref[...] = xv.astype(BF16)

        xb = xb_ref[...]
    else:
        xb = x_ref[...]
    z = [_bdot(xb, w_refs[k][...].astype(BF16)) + b_refs[k][...] for k in range(n_w)]
    if act == "glu":
        out = z[0] * _sigmoid(z[1])
    elif act == "gelu":
        out = jax.nn.gelu(z[0])
    else:
        out = z[0]
    o_ref[...] = out.astype(o_ref.dtype)


def _matmul(x, w, b, *, rows_per_batch, mod=None, act=None, out_dtype=F32, tm=1024, tn=512):
    n, k = x.shape
    nw = w.shape[1]
    n_out = nw // 2 if act == "glu" else nw
    tm = _pick(rows_per_batch, tm)
    tn = _pick(n_out, tn)
    tiles_per_batch = rows_per_batch // tm
    has_mod = mod is not None
    use_scratch = has_mod or x.dtype != BF16
    n_w = 2 if act == "glu" else 1
    half = n_out // tn

    in_specs = [pl.BlockSpec((tm, k), lambda i, j: (i, 0))]
    args = [x]
    if has_mod:
        mspec = pl.BlockSpec((1, 1, k), lambda i, j: (i // tiles_per_batch, 0, 0))
        in_specs += [mspec, mspec]
        args += [mod[0], mod[1]]
    b2 = b.reshape(1, nw)
    in_specs.append(pl.BlockSpec((k, tn), lambda i, j: (0, j)))
    args.append(w)
    if n_w == 2:
        in_specs.append(pl.BlockSpec((k, tn), lambda i, j: (0, j + half)))
        args.append(w)
    in_specs.append(pl.BlockSpec((1, tn), lambda i, j: (0, j)))
    args.append(b2)
    if n_w == 2:
        in_specs.append(pl.BlockSpec((1, tn), lambda i, j: (0, j + half)))
        args.append(b2)
    scratch = [pltpu.VMEM((tm, k), BF16)] if use_scratch else []
    return pl.pallas_call(
        functools.partial(_mm_kernel, n_w=n_w, has_mod=has_mod, use_scratch=use_scratch, act=act),
        grid=(n // tm, n_out // tn),
        in_specs=in_specs,
        out_specs=pl.BlockSpec((tm, tn), lambda i, j: (i, j)),
        out_shape=jax.ShapeDtypeStruct((n, n_out), out_dtype),
        scratch_shapes=scratch,
        compiler_params=_cparams(2),
        name="matmul_" + (act or "bias"),
    )(*args)


def _ln_res_kernel(*refs, alpha, emit_h):
    if emit_h:
        x_ref, y_ref, g_ref, gam_ref, bet_ref, sc_ref, sh_ref, o_ref, h_ref = refs
    else:
        x_ref, y_ref, g_ref, gam_ref, bet_ref, o_ref = refs
    r = alpha * x_ref[...] + g_ref[0] * y_ref[...].astype(F32)
    xn = _layer_norm_rows(r, gam_ref[...], bet_ref[...])
    o_ref[...] = xn
    if emit_h:
        h_ref[...] = (xn * (1.0 + sc_ref[0]) + sh_ref[0]).astype(h_ref.dtype)


def _ln_res(x, y, g, gamma, beta, *, alpha, rows_per_batch, mod=None, h_dtype=BF16, tm=512):
    n, d = x.shape
    tm = _pick(rows_per_batch, tm)
    tpb = rows_per_batch // tm
    row = pl.BlockSpec((tm, d), lambda i: (i, 0))
    per_b = pl.BlockSpec((1, 1, d), lambda i: (i // tpb, 0, 0))
    vec = pl.BlockSpec((1, d), lambda i: (0, 0))
    emit_h = mod is not None
    in_specs = [row, row, per_b, vec, vec]
    args = [x, y, g, gamma.reshape(1, d), beta.reshape(1, d)]
    out_shape = jax.ShapeDtypeStruct((n, d), F32)
    out_specs = row
    if emit_h:
        in_specs += [per_b, per_b]
        args += [mod[0], mod[1]]
        out_shape = (out_shape, jax.ShapeDtypeStruct((n, d), h_dtype))
        out_specs = (row, row)
    return pl.pallas_call(
        functools.partial(_ln_res_kernel, alpha=alpha, emit_h=emit_h),
        grid=(n // tm,),
        in_specs=in_specs,
        out_specs=out_specs,
        out_shape=out_shape,
        compiler_params=_cparams(1),
        name="ln_res",
    )(*args)


def _proj_ln_kernel(h_ref, w_ref, b_ref, x_ref, g_ref, gam_ref, bet_ref, sc_ref, sh_ref, o_ref, h2_ref, wb_ref,
                    *, alpha):
    @pl.when(pl.program_id(0) == 0)
    def _():
        wb_ref[...] = w_ref[...].astype(BF16)

    tm = h_ref.shape[0]
    for r0 in range(0, tm, tm // 2):
        rows = slice(r0, r0 + tm // 2)
        y = _bdot(h_ref[rows, :], wb_ref[...]) + b_ref[...]
        xn = _layer_norm_rows(alpha * x_ref[rows, :] + g_ref[0] * y, gam_ref[...], bet_ref[...])
        o_ref[rows, :] = xn
        h2_ref[rows, :] = (xn * (1.0 + sc_ref[0]) + sh_ref[0]).astype(h2_ref.dtype)


def _proj_ln(h, w, b, x, g, gamma, beta, mod, *, alpha, rows_per_batch, h_dtype, tm=256):
    n, k = h.shape
    d = w.shape[1]
    tm = _pick(rows_per_batch, tm)
    tpb = rows_per_batch // tm
    per_b = pl.BlockSpec((1, 1, d), lambda i: (i // tpb, 0, 0))
    vec = pl.BlockSpec((1, d), lambda i: (0, 0))
    row = pl.BlockSpec((tm, d), lambda i: (i, 0))
    return pl.pallas_call(
        functools.partial(_proj_ln_kernel, alpha=alpha),
        grid=(n // tm,),
        in_specs=[pl.BlockSpec((tm, k), lambda i: (i, 0)),
                  pl.BlockSpec((k, d), lambda i: (0, 0), pipeline_mode=pl.Buffered(1)),
                  vec, row, per_b, vec, vec, per_b, per_b],
        out_specs=(row, row),
        out_shape=(jax.ShapeDtypeStruct((n, d), F32), jax.ShapeDtypeStruct((n, d), h_dtype)),
        scratch_shapes=[pltpu.VMEM((k, d), BF16)],
        compiler_params=_cparams(1),
        name="proj_ln",
    )(h, w, b.reshape(1, d), x, g, gamma.reshape(1, d), beta.reshape(1, d), mod[0], mod[1])


def _group_state(tg_ref, rows_ref):
    i = pl.program_id(1)
    active = rows_ref[i] > 0
    changed = jnp.logical_or(i == 0, tg_ref[i] != tg_ref[jnp.maximum(i - 1, 0)])
    return active, jnp.logical_and(active, changed)


def _ffn_up_kernel(tg_ref, nu_ref, x_ref, w1_ref, w3_ref, h_ref, w1b_ref, w3b_ref):
    active, recast = _group_state(tg_ref, nu_ref)

    @pl.when(recast)
    def _():
        w1b_ref[...] = w1_ref[0].astype(BF16)
        w3b_ref[...] = w3_ref[0].astype(BF16)

    @pl.when(active)
    def _():
        x = x_ref[...]
        a = _bdot(x, w1b_ref[...])
        b = _bdot(x, w3b_ref[...])
        h_ref[...] = (_silu(a) * b).astype(h_ref.dtype)

    @pl.when(jnp.logical_not(active))
    def _():
        h_ref[...] = jnp.zeros(h_ref.shape, h_ref.dtype)


def _ffn_down_kernel(tg_ref, nu_ref, h_ref, w2_ref, o_ref, w2b_ref):
    active, recast = _group_state(tg_ref, nu_ref)

    @pl.when(recast)
    def _():
        w2b_ref[...] = w2_ref[0].astype(BF16)

    @pl.when(active)
    def _():
        o_ref[...] = _bdot(h_ref[...], w2b_ref[...])

    @pl.when(jnp.logical_not(active))
    def _():
        o_ref[...] = jnp.zeros(o_ref.shape, o_ref.dtype)


def _grouped_ffn(xb, w1, w3, w2, tile_g, tile_rows, *, tm, tf=512, tm_down=512, tn=512):
    r, d = xb.shape
    ff = w1.shape[2]
    tf = _pick(ff, tf)
    tn = _pick(d, tn)
    tm_down = _pick(tm, tm_down)
    n_tiles = r // tm
    h = pl.pallas_call(
        _ffn_up_kernel,
        grid_spec=pltpu.PrefetchScalarGridSpec(
            num_scalar_prefetch=2,
            grid=(ff // tf, n_tiles),
            in_specs=[pl.BlockSpec((tm, d), lambda f, i, tg, nu: (i, 0)),
                      pl.BlockSpec((1, d, tf), lambda f, i, tg, nu: (tg[i], 0, f)),
                      pl.BlockSpec((1, d, tf), lambda f, i, tg, nu: (tg[i], 0, f))],
            out_specs=pl.BlockSpec((tm, tf), lambda f, i, tg, nu: (i, f)),
            scratch_shapes=[pltpu.VMEM((d, tf), BF16), pltpu.VMEM((d, tf), BF16)]),
        out_shape=jax.ShapeDtypeStruct((r, ff), BF16),
        compiler_params=_cparams(2),
        name="ffn_up",
    )(tile_g, tile_rows, xb, w1, w3)
    sub = tm // tm_down
    tile_g2 = jnp.repeat(tile_g, sub)
    rows2 = jnp.repeat(tile_rows, sub) - jnp.tile(jnp.arange(sub, dtype=jnp.int32) * tm_down, n_tiles)
    return pl.pallas_call(
        _ffn_down_kernel,
        grid_spec=pltpu.PrefetchScalarGridSpec(
            num_scalar_prefetch=2,
            grid=(d // tn, n_tiles * sub),
            in_specs=[pl.BlockSpec((tm_down, ff), lambda n, i, tg, nu: (i, 0)),
                      pl.BlockSpec((1, ff, tn), lambda n, i, tg, nu: (tg[i], 0, n))],
            out_specs=pl.BlockSpec((tm_down, tn), lambda n, i, tg, nu: (i, n)),
            scratch_shapes=[pltpu.VMEM((ff, tn), BF16)]),
        out_shape=jax.ShapeDtypeStruct((r, d), F32),
        compiler_params=_cparams(2),
        name="ffn_down",
    )(tile_g2, rows2, h, w2)


def _conv_kernel(x_ref, w_ref, b_ref, g_ref, be_ref, o_ref, xx_ref, u_ref, *, ts, kw, rc, cc):
    i = pl.program_id(1)
    d = x_ref.shape[1]
    halo = CONV_HALO

    @pl.when(i == 0)
    def _():
        xx_ref[0:halo, :] = jnp.zeros((halo, d), F32)

    @pl.when(i > 0)
    def _():
        xx_ref[0:halo, :] = xx_ref[ts:ts + halo, :]

    xx_ref[halo:halo + ts, :] = x_ref[...]
    off = halo - (kw - 1)

    sub = V7X_SUBLANES
    for r0 in range(0, ts, rc):
        for c0 in range(0, d, cc):
            acc = jnp.broadcast_to(b_ref[:, c0:c0 + cc], (rc, cc))
            for rho in range(sub):
                taps = [j for j in range(kw) if (off + j) % sub == rho]
                if not taps:
                    continue
                base = r0 + off + taps[0]
                win = xx_ref[base:base + rc + taps[-1] - taps[0], c0:c0 + cc]
                part = None
                for j in taps:
                    o = j - taps[0]
                    term = w_ref[j:j + 1, c0:c0 + cc] * win[o:o + rc, :]
                    part = term if part is None else part + term
                acc = acc + part
            u_ref[r0:r0 + rc, c0:c0 + cc] = acc
    y = _layer_norm_rows(u_ref[...], g_ref[...], be_ref[...])
    o_ref[...] = _silu(y).astype(o_ref.dtype)


def _conv_ln_swish(glu, w_dw, b_dw, ln_g, ln_b, *, bsz, seq, ts=128):
    n, d = glu.shape
    kw = w_dw.shape[0]
    ts = _pick(seq, ts)
    nt = seq // ts
    rc = _pick(ts, 64)
    cc = _pick(d, 256)
    row = pl.BlockSpec((ts, d), lambda b, i: (b * nt + i, 0))
    vec = pl.BlockSpec((1, d), lambda b, i: (0, 0))
    return pl.pallas_call(
        functools.partial(_conv_kernel, ts=ts, kw=kw, rc=rc, cc=cc),
        grid=(bsz, nt),
        in_specs=[row, pl.BlockSpec((kw, d), lambda b, i: (0, 0)), vec, vec, vec],
        out_specs=row,
        out_shape=jax.ShapeDtypeStruct((n, d), BF16),
        scratch_shapes=[pltpu.VMEM((CONV_HALO + ts, d), F32), pltpu.VMEM((ts, d), F32)],
        compiler_params=_cparams(2),
        name="conv_ln_swish",
    )(glu, w_dw, b_dw.reshape(1, d), ln_g.reshape(1, d), ln_b.reshape(1, d))


def _iota_div(x, k):
    assert k & (k - 1) == 0
    return lax.shift_right_logical(x, k.bit_length() - 1)


def _iota_mod(x, k):
    assert k & (k - 1) == 0
    return lax.bitwise_and(x, k - 1)


def _s5_expand_tables(kq_ref, mq_ref, cq_ref, t_scr, m_scr, c_scr, *, lc, p):
    lanes = t_scr.shape[0] // lc
    ns2 = m_scr.shape[1]
    ns = ns2 // 2
    nst = ns // (lanes // p)
    iota = lambda shape, dim: lax.broadcasted_iota(jnp.int32, shape, dim)
    r, c = iota((lanes, lanes), 0), iota((lanes, lanes), 1)
    same_g = _iota_div(r, p) == _iota_div(c, p)
    e_t = (_iota_mod(iota((p, lanes), 1), p) == iota((p, lanes), 0)).astype(BF16)
    tiles = [jnp.where(same_g, _bdot(kq_ref[0, tau].astype(BF16), e_t), 0.0).astype(BF16) for tau in range(lc)]
    zero_tile = jnp.zeros((lanes, lanes), BF16)
    for s in range(lc):
        for t in range(lc):
            t_scr[s * lanes:(s + 1) * lanes, t * lanes:(t + 1) * lanes] = tiles[t - s] if t >= s else zero_tile
    r, c = iota((lanes, ns2), 0), iota((lanes, ns2), 1)
    e_m = (r == _iota_div(c, ns) * nst + _iota_mod(c, nst)).astype(BF16)
    mask_m = _iota_div(r, p) == _iota_div(_iota_mod(c, ns), nst)
    for s in range(lc):
        m_scr[s * lanes:(s + 1) * lanes, :] = jnp.where(
            mask_m, _bdot(mq_ref[0, s].astype(BF16), e_m), 0.0).astype(BF16)
    r, c = iota((ns2, lanes), 0), iota((ns2, lanes), 1)
    e_c = (c == _iota_div(r, ns) * nst + _iota_mod(r, nst)).astype(BF16)
    mask_c = _iota_div(_iota_mod(r, ns), nst) == _iota_div(c, p)
    for t in range(lc):
        c_scr[:, t * lanes:(t + 1) * lanes] = jnp.where(
            mask_c, _bdot(e_c, cq_ref[0, t].astype(BF16)), 0.0).astype(BF16)


def _s5_kernel(u_ref, kq_ref, mq_ref, cq_ref, lp_ref, d_ref, o_ref,
               t_scr, m_scr, c_scr, a_ref, v_ref, p_ref, *, lc, nc, p):
    @pl.when(pl.program_id(1) == 0)
    def _():
        _s5_expand_tables(kq_ref, mq_ref, cq_ref, t_scr, m_scr, c_scr, lc=lc, p=p)

    lanes = u_ref.shape[2]
    for t in range(lc):
        a_ref[:, t * lanes:(t + 1) * lanes] = u_ref[0, pl.ds(t, nc, stride=lc), :]
    a = a_ref[...]
    ab = a.astype(BF16)
    v_ref[...] = _bdot(ab, m_scr[...])
    pw_r = lp_ref[0, 0]
    pw_i = lp_ref[0, 1]
    ns = pw_r.shape[1]

    slab = V7X_SUBLANES
    srow = lax.broadcasted_iota(jnp.int32, (slab, ns), 0)

    def step(k, carry):
        cr, ci = carry
        r0 = pl.multiple_of(k * slab, slab)
        yr = v_ref[pl.ds(r0, slab), 0:ns]
        yi = v_ref[pl.ds(r0, slab), ns:2 * ns]
        sh = 1
        while sh < slab:
            mr, mi = pw_r[sh - 1:sh, :], pw_i[sh - 1:sh, :]
            zr = jnp.where(srow >= sh, pltpu.roll(yr, sh, 0), 0.0)
            zi = jnp.where(srow >= sh, pltpu.roll(yi, sh, 0), 0.0)
            yr, yi = yr + mr * zr - mi * zi, yi + mr * zi + mi * zr
            sh *= 2
        sr = yr + pw_r * cr - pw_i * ci
        si = yi + pw_r * ci + pw_i * cr
        p_ref[pl.ds(r0, slab), 0:ns] = jnp.where(srow == 0, cr, pltpu.roll(sr, 1, 0))
        p_ref[pl.ds(r0, slab), ns:2 * ns] = jnp.where(srow == 0, ci, pltpu.roll(si, 1, 0))
        return sr[slab - 1:slab, :], si[slab - 1:slab, :]

    zero = jnp.zeros((1, ns), F32)
    lax.fori_loop(0, nc // slab, step, (zero, zero))
    y = _bdot(ab, t_scr[...]) + _bdot(p_ref[...].astype(BF16), c_scr[...]) + d_ref[0] * a
    y = jax.nn.gelu(y)
    for t in range(lc):
        o_ref[0, pl.ds(t, nc, stride=lc), :] = y[:, t * lanes:(t + 1) * lanes]


def _s5_tables(a_re, a_im, log_dt, b_re, b_im, c_re, c_im, d_skip):
    g, n = a_re.shape
    p = b_re.shape[2]
    lc = S5_CHUNK
    ar, ai = a_re.astype(F32), a_im.astype(F32)
    dt = jnp.exp(log_dt.astype(F32))[:, None]
    decay = jnp.exp(ar * dt)
    lr, li = decay * jnp.cos(ai * dt), decay * jnp.sin(ai * dt)
    den = ar * ar + ai * ai
    zr = ((lr - 1.0) * ar + li * ai) / den
    zi = (li * ar - (lr - 1.0) * ai) / den
    br, bi = b_re.astype(F32), b_im.astype(F32)
    bbr = zr[..., None] * br - zi[..., None] * bi
    bbi = zr[..., None] * bi + zi[..., None] * br
    tau = jnp.arange(lc + 1, dtype=F32)[:, None, None]
    pdec = jnp.exp(tau * (ar * dt)[None])
    pr, pi = pdec * jnp.cos(tau * (ai * dt)[None]), pdec * jnp.sin(tau * (ai * dt)[None])
    cr, ci = c_re.astype(F32), c_im.astype(F32)
    hp = lax.Precision.HIGHEST
    lbr = pr[..., None] * bbr[None] - pi[..., None] * bbi[None]
    lbi = pr[..., None] * bbi[None] + pi[..., None] * bbr[None]
    ktau = (jnp.einsum('gpn,tgnq->tgpq', cr, lbr[:lc], precision=hp)
            - jnp.einsum('gpn,tgnq->tgpq', ci, lbi[:lc], precision=hp))
    gb = V7X_LANES // p
    nj = g // gb
    kq = ktau.reshape(lc, nj, gb, p, p).transpose(1, 0, 2, 4, 3).reshape(nj, lc, gb * p, p)
    rev = lc - 1 - jnp.arange(lc)
    mq = jnp.stack([lbr[rev], lbi[rev]], axis=2).reshape(lc, nj, gb, 2, n, p)
    mq = mq.transpose(1, 0, 2, 5, 3, 4).reshape(nj, lc, gb * p, 2 * n)
    pr1, pi1 = pr[1:], pi[1:]
    cmr = (cr[None] * pr1[:, :, None, :] - ci[None] * pi1[:, :, None, :])
    cmi = -(cr[None] * pi1[:, :, None, :] + ci[None] * pr1[:, :, None, :])
    cq = jnp.stack([cmr, cmi], axis=0).reshape(2, lc, nj, gb, p, n)
    cq = cq.transpose(2, 1, 0, 5, 3, 4).reshape(nj, lc, 2 * n, gb * p)
    mpw = (lc * jnp.arange(1, V7X_SUBLANES + 1, dtype=F32))[:, None, None]
    cdec = jnp.exp(mpw * (ar * dt)[None])
    lam_pw = jnp.stack([cdec * jnp.cos(mpw * (ai * dt)[None]), cdec * jnp.sin(mpw * (ai * dt)[None])], axis=0)
    lam_pw = lam_pw.reshape(2, V7X_SUBLANES, nj, gb * n).transpose(2, 0, 1, 3)
    dvec = jnp.broadcast_to(d_skip.astype(F32).reshape(nj, 1, 1, gb * p), (nj, 1, lc, gb * p))
    return kq, mq, cq, lam_pw, dvec.reshape(nj, 1, lc * gb * p)


def _s5_core(u, tables, *, bsz, seq):
    n_rows, d = u.shape
    kq, mq, cq, lam_pw, dvec = tables
    nj, lc, lanes, p = kq.shape
    ns = lam_pw.shape[3]
    nc = seq // lc
    wk = lc * lanes
    assert lanes == V7X_LANES and nc % V7X_SUBLANES == 0 and nj * lanes == d
    tab = lambda *shape: pl.BlockSpec((1,) + shape, lambda j, b: (j,) + (0,) * len(shape))
    seq_blk = pl.BlockSpec((1, seq, lanes), lambda j, b: (b, 0, j))
    y = pl.pallas_call(
        functools.partial(_s5_kernel, lc=lc, nc=nc, p=p),
        grid=(nj, bsz),
        in_specs=[seq_blk, tab(lc, lanes, p), tab(lc, lanes, lanes), tab(lc, lanes, lanes),
                  tab(2, V7X_SUBLANES, ns), tab(1, wk)],
        out_specs=seq_blk,
        out_shape=jax.ShapeDtypeStruct((bsz, seq, d), F32),
        scratch_shapes=[pltpu.VMEM((wk, wk), BF16), pltpu.VMEM((wk, 2 * ns), BF16), pltpu.VMEM((2 * ns, wk), BF16),
                        pltpu.VMEM((nc, wk), F32), pltpu.VMEM((nc, 2 * ns), F32), pltpu.VMEM((nc, 2 * ns), F32)],
        compiler_params=_cparams(2),
        name="s5_core",
    )(u.reshape(bsz, seq, d), kq, mq, cq, lam_pw, dvec)
    return y.reshape(n_rows, d)


def _log_sigmoid(x):
    return jnp.minimum(x, 0.0) - jnp.log(1.0 + jnp.exp(-jnp.abs(x)))


def _mlstm_kernel(q_ref, k_ref, v_ref, o_ref, g_ref, cw_ref, cb_ref, mh_ref, out_ref,
                  c_st, n_st, m_st, qx_ref, kx_ref, *, lm, dk, heads, kw):
    c = pl.program_id(1)
    pad = V7X_SUBLANES
    hdk = heads * dk
    dv = v_ref.shape[1] // heads

    @pl.when(c == 0)
    def _():
        c_st[...] = jnp.zeros(c_st.shape, F32)
        n_st[...] = jnp.zeros(n_st.shape, F32)
        m_st[...] = jnp.zeros(m_st.shape, F32)
        qx_ref[0:pad, :] = jnp.zeros((pad, hdk), F32)
        kx_ref[0:pad, :] = jnp.zeros((pad, hdk), F32)

    def conv_swish(x_ref, xx_ref, c0):
        xx_ref[pad:pad + lm, :] = x_ref[...]
        acc = jnp.broadcast_to(cb_ref[:, c0:c0 + hdk], (lm, hdk))
        for j in range(kw):
            acc = acc + cw_ref[j:j + 1, c0:c0 + hdk] * xx_ref[pad - (kw - 1) + j:pad - (kw - 1) + j + lm, :]
        xx_ref[0:pad, :] = xx_ref[lm:lm + pad, :]
        return _silu(acc)

    q_all = conv_swish(q_ref, qx_ref, 0)
    k_all = conv_swish(k_ref, kx_ref, hdk) * (dk ** -0.5)

    gts = g_ref[...]
    lf = _log_sigmoid(gts)
    rows = lax.broadcasted_iota(jnp.int32, (lm, lm), 0)
    cols = lax.broadcasted_iota(jnp.int32, (lm, lm), 1)
    causal = rows >= cols
    tri = causal.astype(BF16)
    lf_hi = lf.astype(BF16)
    r1 = lf - lf_hi.astype(F32)
    lf_mid = r1.astype(BF16)
    lf_lo = (r1 - lf_mid.astype(F32)).astype(BF16)
    bcum = _bdot(tri, lf_hi) + _bdot(tri, lf_mid) + _bdot(tri, lf_lo)

    bcum_t = bcum.T
    gts_t = gts.T
    for h in range(heads):
        q = q_all[:, h * dk:(h + 1) * dk]
        k = k_all[:, h * dk:(h + 1) * dk]
        vcols = slice(h * dv, (h + 1) * dv)
        b_col = bcum[:, heads + h:heads + h + 1]
        li_col = gts[:, h:h + 1]
        b_row = bcum_t[heads + h:heads + h + 1, :]
        li_row = gts_t[h:h + 1, :]

        m_prev = m_st[h]
        dmat = jnp.where(causal, b_col - b_row + li_row, -jnp.inf)
        inter = b_col + m_prev
        m_j = jnp.maximum(inter, jnp.max(dmat, axis=1, keepdims=True))
        w_intra = jnp.exp(dmat - m_j)
        w_inter = jnp.exp(inter - m_j)
        qb, kb, vb = q.astype(BF16), k.astype(BF16), v_ref[:, vcols].astype(BF16)
        sc = lax.dot_general(qb, kb, (((1,), (1,)), ((), ())), preferred_element_type=F32) * w_intra
        num = w_inter * _bdot(qb, c_st[h].astype(BF16)) + _bdot(sc.astype(BF16), vb)
        den = w_inter * jnp.sum(q * n_st[h], axis=1, keepdims=True) + jnp.sum(sc, axis=1, keepdims=True)
        hb = num / jnp.maximum(jnp.abs(den), jnp.exp(-m_j))

        bl = b_col[lm - 1:lm, :]
        gsum = bl - b_col + li_col
        m_new = jnp.maximum(bl + m_prev, jnp.max(gsum, axis=0, keepdims=True))
        wc = jnp.exp(bl + m_prev - m_new)
        wk = jnp.exp(gsum - m_new)
        kwt = wk * k
        c_st[h] = wc * c_st[h] + _bdot(kwt.T.astype(BF16), vb)
        n_st[h] = wc * n_st[h] + jnp.sum(kwt, axis=0, keepdims=True)
        m_st[h] = m_new

        hs = _sigmoid(o_ref[:, vcols]) * hb
        out_ref[:, vcols] = (_layer_norm_rows(hs, None, None) * mh_ref[:, vcols]).astype(out_ref.dtype)


def _mlstm_core(z, w_conv, b_conv, mh_g, *, bsz, seq, d):
    n_rows = z.shape[0]
    heads = ML_HEADS
    dv = d // heads
    dk = dv // 2
    lm = _pick(seq, ML_CHUNK_ROWS)
    nc = seq // lm
    kw = w_conv.shape[0]
    hdk = heads * dk
    hdv = heads * dv
    assert 2 * hdk == hdv
    gate_blk = (2 * hdk + 2 * hdv) // V7X_LANES
    row = lambda b, c: b * nc + c
    in_specs = [
        pl.BlockSpec((lm, hdk), lambda b, c: (row(b, c), 0)),
        pl.BlockSpec((lm, hdk), lambda b, c: (row(b, c), 1)),
        pl.BlockSpec((lm, hdv), lambda b, c: (row(b, c), 1)),
        pl.BlockSpec((lm, hdv), lambda b, c: (row(b, c), 2)),
        pl.BlockSpec((lm, V7X_LANES), lambda b, c: (row(b, c), gate_blk)),
        pl.BlockSpec((kw, 2 * hdk), lambda b, c: (0, 0)),
        pl.BlockSpec((1, 2 * hdk), lambda b, c: (0, 0)),
        pl.BlockSpec((1, d), lambda b, c: (0, 0)),
    ]
    return pl.pallas_call(
        functools.partial(_mlstm_kernel, lm=lm, dk=dk, heads=heads, kw=kw),
        grid=(bsz, nc),
        in_specs=in_specs,
        out_specs=pl.BlockSpec((lm, d), lambda b, c: (row(b, c), 0)),
        out_shape=jax.ShapeDtypeStruct((n_rows, d), BF16),
        scratch_shapes=[pltpu.VMEM((heads, dk, dv), F32), pltpu.VMEM((heads, 1, dk), F32),
                        pltpu.VMEM((heads, 1, 1), F32),
                        pltpu.VMEM((lm + V7X_SUBLANES, hdk), F32), pltpu.VMEM((lm + V7X_SUBLANES, hdk), F32)],
        compiler_params=_cparams(2),
        name="mlstm_core",
    )(z, z, z, z, z, w_conv, b_conv.reshape(1, -1), mh_g.reshape(1, d))


def _gmlp_kernel(u_ref, v_ref, g_ref, be_ref, w_ref, bsp_ref, o_ref, *, lg, groups, dg, nsub):
    rows = lax.broadcasted_iota(jnp.int32, (lg, lg), 0)
    cols = lax.broadcasted_iota(jnp.int32, (lg, lg), 1)
    keep = rows >= cols
    wm = [jnp.where(keep, w_ref[g], 0.0).astype(BF16) for g in range(groups)]
    for s in range(nsub):
        r = slice(s * lg, (s + 1) * lg)
        vn = _layer_norm_rows(v_ref[r, :], g_ref[...], be_ref[...]).astype(BF16)
        for g in range(groups):
            cs = slice(g * dg, (g + 1) * dg)
            sv = _bdot(wm[g], vn[:, cs]) + bsp_ref[:, g:g + 1]
            o_ref[r, cs] = (u_ref[r, cs] * sv).astype(o_ref.dtype)


def _gmlp_core(zz, ln_g, ln_b, w_sp, b_sp, *, seq, tm=512):
    n, d2 = zz.shape
    d = d2 // 2
    groups, lg, _ = w_sp.shape
    tm = max(lg, _pick(seq, tm))
    return pl.pallas_call(
        functools.partial(_gmlp_kernel, lg=lg, groups=groups, dg=d // groups, nsub=tm // lg),
        grid=(n // tm,),
        in_specs=[pl.BlockSpec((tm, d), lambda i: (i, 0)),
                  pl.BlockSpec((tm, d), lambda i: (i, 1)),
                  pl.BlockSpec((1, d), lambda i: (0, 0)),
                  pl.BlockSpec((1, d), lambda i: (0, 0)),
                  pl.BlockSpec((groups, lg, lg), lambda i: (0, 0, 0)),
                  pl.BlockSpec((lg, groups), lambda i: (0, 0))],
        out_specs=pl.BlockSpec((tm, d), lambda i: (i, 0)),
        out_shape=jax.ShapeDtypeStruct((n, d), BF16),
        compiler_params=_cparams(1),
        name="gmlp_spatial",
    )(zz, zz, ln_g.reshape(1, d), ln_b.reshape(1, d), w_sp, b_sp.T)


def _router_kernel(h_ref, w_ref, e_ref, g_ref, *, n_exp):
    h = h_ref[...]
    w = w_ref[...]
    hh = h.astype(BF16)
    hl = (h - hh.astype(F32)).astype(BF16)
    wh = w.astype(BF16)
    wl = (w - wh.astype(F32)).astype(BF16)
    logits = _bdot(hh, wh) + _bdot(hl, wh) + _bdot(hh, wl)
    nl = logits.shape[1]
    lane = lax.broadcasted_iota(jnp.int32, logits.shape, 1).astype(F32)
    cur = jnp.where(lane < n_exp, logits, -jnp.inf)
    vals, idxs = [], []
    for _ in range(TOP_K):
        m = jnp.max(cur, axis=1, keepdims=True)
        idx = jnp.min(jnp.where(cur == m, lane, float(nl)), axis=1, keepdims=True)
        vals.append(m)
        idxs.append(idx)
        cur = jnp.where(lane == idx, -jnp.inf, cur)
    ex = [jnp.exp(v - vals[0]) for v in vals]
    tot = ex[0]
    for t in ex[1:]:
        tot = tot + t
    e_out = jnp.zeros(logits.shape, F32)
    g_out = jnp.zeros(logits.shape, F32)
    for k in range(TOP_K):
        e_out = jnp.where(lane == k, idxs[k], e_out)
        g_out = jnp.where(lane == k, ex[k] / tot, g_out)
    e_ref[...] = e_out.astype(jnp.int32)
    g_ref[...] = g_out


def _router_topk(hf, w_router, *, tm=1024):
    n, d = hf.shape
    e = w_router.shape[1]
    wpad = jnp.pad(w_router, ((0, 0), (0, V7X_LANES - e)))
    tm = _pick(n, tm)
    out_blk = pl.BlockSpec((tm, V7X_LANES), lambda i: (i, 0))
    top_e, gates = pl.pallas_call(
        functools.partial(_router_kernel, n_exp=e),
        grid=(n // tm,),
        in_specs=[pl.BlockSpec((tm, d), lambda i: (i, 0)),
                  pl.BlockSpec((d, V7X_LANES), lambda i: (0, 0))],
        out_specs=(out_blk, out_blk),
        out_shape=(jax.ShapeDtypeStruct((n, V7X_LANES), jnp.int32), jax.ShapeDtypeStruct((n, V7X_LANES), F32)),
        compiler_params=_cparams(1),
        name="moe_router",
    )(hf, wpad)
    return top_e[:, :TOP_K], gates[:, :TOP_K]


def _moe_gather_kernel(nu_ref, tok_ref, tokn_ref, hf_hbm, o_ref, xg_ref, sem, *, tm):
    i = pl.program_id(0)
    nu = nu_ref[0]
    slot = lax.rem(i, 2)

    def row_copy(tref, r, s):
        return pltpu.make_async_copy(hf_hbm.at[pl.ds(tref[0, 0, r], 1), :], xg_ref.at[s, pl.ds(r, 1), :], sem.at[s])

    def start_tile(tref, s):
        def issue(r, carry):
            row_copy(tref, r, s).start()
            return carry

        lax.fori_loop(0, tm, issue, 0, unroll=DMA_UNROLL)

    @pl.when(jnp.logical_and(i == 0, nu > 0))
    def _():
        start_tile(tok_ref, 0)

    @pl.when(i + 1 < nu)
    def _():
        start_tile(tokn_ref, 1 - slot)

    @pl.when(i < nu)
    def _():
        def wait(r, carry):
            row_copy(tok_ref, r, slot).wait()
            return carry

        lax.fori_loop(0, tm, wait, 0, unroll=DMA_UNROLL)
        o_ref[...] = xg_ref[slot].astype(o_ref.dtype)

    @pl.when(i >= nu)
    def _():
        o_ref[...] = jnp.zeros(o_ref.shape, o_ref.dtype)


def _moe_gather(hf, row_tok, n_used, *, tm):
    n, d = hf.shape
    n_tiles = row_tok.shape[0]
    tok_spec = lambda fn: pl.BlockSpec((1, 1, tm), fn, memory_space=pltpu.SMEM)
    return pl.pallas_call(
        functools.partial(_moe_gather_kernel, tm=tm),
        grid_spec=pltpu.PrefetchScalarGridSpec(
            num_scalar_prefetch=1,
            grid=(n_tiles,),
            in_specs=[tok_spec(lambda i, nu: (i, 0, 0)),
                      tok_spec(lambda i, nu: (jnp.minimum(i + 1, n_tiles - 1), 0, 0)),
                      pl.BlockSpec(memory_space=pl.ANY)],
            out_specs=pl.BlockSpec((tm, d), lambda i, nu: (i, 0)),
            scratch_shapes=[pltpu.VMEM((2, tm, d), hf.dtype), pltpu.SemaphoreType.DMA((2,))]),
        out_shape=jax.ShapeDtypeStruct((n_tiles * tm, d), BF16),
        compiler_params=_cparams(1),
        name="moe_gather",
    )(n_used, row_tok, row_tok, hf)


def _moe_combine_kernel(*refs, tm, alpha, emit_h):
    if emit_h:
        (pos_ref, posn_ref, y_hbm, x_ref, gate_ref, g_ref, gam_ref, bet_ref, sc_ref, sh_ref,
         o_ref, h_ref, yb_ref, sem) = refs
    else:
        pos_ref, posn_ref, y_hbm, x_ref, gate_ref, g_ref, gam_ref, bet_ref, o_ref, yb_ref, sem = refs
    i = pl.program_id(0)
    nt = pl.num_programs(0)
    slot = lax.rem(i, 2)

    def row_copy(pref, k, r, s):
        return pltpu.make_async_copy(y_hbm.at[pl.ds(pref[0, 0, k * tm + r], 1), :],
                                     yb_ref.at[s, k, pl.ds(r, 1), :], sem.at[s])

    def start_tile(pref, s):
        def issue(r, carry):
            for k in range(TOP_K):
                row_copy(pref, k, r, s).start()
            return carry

        lax.fori_loop(0, tm, issue, 0, unroll=DMA_UNROLL)

    @pl.when(i == 0)
    def _():
        start_tile(pos_ref, 0)

    @pl.when(i + 1 < nt)
    def _():
        start_tile(posn_ref, 1 - slot)

    def wait(r, carry):
        for k in range(TOP_K):
            row_copy(pos_ref, k, r, slot).wait()
        return carry

    lax.fori_loop(0, tm, wait, 0, unroll=DMA_UNROLL)
    gate = gate_ref[...]
    y = gate[:, 0:1] * yb_ref[slot, 0]
    for k in range(1, TOP_K):
        y = y + gate[:, k:k + 1] * yb_ref[slot, k]
    r = alpha * x_ref[...] + g_ref[0] * y
    xn = _layer_norm_rows(r, gam_ref[...], bet_ref[...])
    o_ref[...] = xn
    if emit_h:
        h_ref[...] = (xn * (1.0 + sc_ref[0]) + sh_ref[0]).astype(h_ref.dtype)


def _moe_combine_ln(y_rows, pos, gates, x, g, gamma, beta, *, alpha, rows_per_batch, mod=None, tm=256):
    n, d = x.shape
    tm = _pick(rows_per_batch, tm)
    tpb = rows_per_batch // tm
    nt = n // tm
    pos3 = pos.reshape(nt, tm, TOP_K).transpose(0, 2, 1).reshape(nt, 1, TOP_K * tm)
    row = pl.BlockSpec((tm, d), lambda i: (i, 0))
    vec = pl.BlockSpec((1, d), lambda i: (0, 0))
    per_b = pl.BlockSpec((1, 1, d), lambda i: (i // tpb, 0, 0))
    emit_h = mod is not None
    in_specs = [pl.BlockSpec((1, 1, TOP_K * tm), lambda i: (i, 0, 0), memory_space=pltpu.SMEM),
                pl.BlockSpec((1, 1, TOP_K * tm), lambda i: (jnp.minimum(i + 1, nt - 1), 0, 0),
                             memory_space=pltpu.SMEM),
                pl.BlockSpec(memory_space=pl.ANY),
                row,
                pl.BlockSpec((tm, TOP_K), lambda i: (i, 0)),
                per_b, vec, vec]
    args = [pos3, pos3, y_rows, x, gates, g, gamma.reshape(1, d), beta.reshape(1, d)]
    out_specs, out_shape = row, jax.ShapeDtypeStruct((n, d), F32)
    if emit_h:
        in_specs += [per_b, per_b]
        args += [mod[0], mod[1]]
        out_specs = (row, row)
        out_shape = (out_shape, jax.ShapeDtypeStruct((n, d), BF16))
    return pl.pallas_call(
        functools.partial(_moe_combine_kernel, tm=tm, alpha=alpha, emit_h=emit_h),
        grid=(nt,),
        in_specs=in_specs,
        out_specs=out_specs,
        out_shape=out_shape,
        scratch_shapes=[pltpu.VMEM((2, TOP_K, tm, d), F32), pltpu.SemaphoreType.DMA((2,))],
        compiler_params=_cparams(1),
        name="moe_combine_ln",
    )(*args)


def _route(top_e, e, tm):
    n = top_e.shape[0]
    e_flat = top_e.reshape(-1).astype(jnp.int32)
    onehot = (e_flat[:, None] == jnp.arange(e, dtype=jnp.int32)[None, :]).astype(jnp.int32)
    cum = jnp.cumsum(onehot, axis=0)
    rank = jnp.take_along_axis(cum, e_flat[:, None], axis=1)[:, 0] - 1
    counts = cum[-1]
    padded = (counts + tm - 1) // tm * tm
    pend = jnp.cumsum(padded)
    pstart = pend - padded
    dest = (pstart[e_flat] + rank).astype(jnp.int32)
    n_tiles = (n * TOP_K) // tm + e
    tok_flat = jnp.repeat(jnp.arange(n, dtype=jnp.int32), TOP_K)
    row_tok = jnp.zeros((n_tiles * tm,), jnp.int32).at[dest].set(tok_flat)
    n_used = (pend[-1] // tm).astype(jnp.int32)
    tidx = jnp.arange(n_tiles, dtype=jnp.int32)
    tile_e = jnp.minimum(jnp.sum((tidx[:, None] * tm >= pend[None, :]).astype(jnp.int32), axis=1), e - 1)
    tile_rows = jnp.clip(counts[tile_e] - (tidx * tm - pstart[tile_e]), 0, tm).astype(jnp.int32)
    tile_e = jnp.where(tidx < n_used, tile_e, tile_e[jnp.maximum(n_used - 1, 0)])
    return dest.reshape(n, TOP_K), row_tok.reshape(n_tiles, 1, tm), tile_e, tile_rows, n_used.reshape(1)


def _moe_layer(x, hf, g2, gamma, beta, w_router, w1, w3, w2, layer_idx, *, alpha, rows_per_batch, mod_next=None,
               tm=1024):
    n = hf.shape[0]
    tm = _pick(n * TOP_K, tm)
    top_e, gates = _router_topk(hf, w_router)
    pos, row_tok, tile_e, tile_rows, n_used = _route(top_e, w_router.shape[1], tm)
    xs = _moe_gather(hf, row_tok, n_used, tm=tm)
    y_rows = _grouped_ffn(xs, w1, w3, w2, tile_e + layer_idx * N_EXPERTS, tile_rows, tm=tm)
    return _moe_combine_ln(y_rows, pos, gates, x, g2, gamma, beta, alpha=alpha, rows_per_batch=rows_per_batch,
                           mod=mod_next)


def kernel(x, c, ada_w, ada_b, ln1_g, ln1_b, ln2_g, ln2_b, a_w_in, a_b_in, a_w_dw, a_b_dw, a_ln_g, a_ln_b, a_w_out, a_b_out, b_w_in, b_b_in, b_a_re, b_a_im, b_log_dt, b_b_re, b_b_im, b_c_re, b_c_im, b_d, b_w_glu, b_b_glu, c_w_in, c_b_in, c_w_conv, c_b_conv, c_mh_g, c_w_out, c_b_out, d_w_in, d_b_in, d_ln_g, d_ln_b, d_w_sp, d_b_sp, d_w_out, d_b_out, f_w1, f_w3, f_w2, m_router, m_w1, m_w3, m_w2):
    bsz, seq, d = x.shape
    depth = ada_w.shape[0]
    alpha = (2.0 * depth) ** 0.25
    n = bsz * seq
    mm = functools.partial(_matmul, rows_per_batch=seq)

    mods = _ada_mods(c, ada_w, ada_b)
    xf = x.reshape(n, d)
    ia = ib = ic = idd = 0
    i_dense = i_moe = 0
    hm = None
    mod_of = lambda l: [mods[l, :, k * d:(k + 1) * d].reshape(bsz, 1, d) for k in range(6)]
    for layer in range(depth):
        sh1, sc1, g1, sh2, sc2, g2 = mod_of(layer)
        mod_next = None
        if layer + 1 < depth:
            sh1n, sc1n = mod_of(layer + 1)[:2]
            mod_next = (sc1n, sh1n)

        def mix_in(w, b, xf=xf, hm=hm, sc1=sc1, sh1=sh1, tm_bf16=2048, **kw):
            if hm is None:
                return mm(xf, w, b, mod=(sc1, sh1), **kw)
            return mm(hm, w, b, tm=tm_bf16, **kw)

        kind = layer % N_MIXERS
        is_moe = layer % 2 == 1
        h_dtype = F32 if is_moe else BF16
        proj_ln = functools.partial(_proj_ln, x=xf, g=g1, gamma=ln1_g[layer], beta=ln1_b[layer], mod=(sc2, sh2),
                                    alpha=alpha, rows_per_batch=seq, h_dtype=h_dtype)
        if kind == 0:
            glu = mix_in(a_w_in[ia], a_b_in[ia], act="glu")
            u = _conv_ln_swish(glu, a_w_dw[ia], a_b_dw[ia], a_ln_g[ia], a_ln_b[ia], bsz=bsz, seq=seq)
            xf, hf = proj_ln(u, a_w_out[ia], a_b_out[ia])
            ia += 1
        elif kind == 1:
            u = mix_in(b_w_in[ib], b_b_in[ib])
            tables = _s5_tables(b_a_re[ib], b_a_im[ib], b_log_dt[ib], b_b_re[ib], b_b_im[ib],
                                b_c_re[ib], b_c_im[ib], b_d[ib])
            yg = _s5_core(u, tables, bsz=bsz, seq=seq)
            y = mm(yg, b_w_glu[ib], b_b_glu[ib], act="glu")
            xf, hf = _ln_res(xf, y, g1, ln1_g[layer], ln1_b[layer], alpha=alpha, rows_per_batch=seq,
                             mod=(sc2, sh2), h_dtype=h_dtype)
            ib += 1
        elif kind == 2:
            heads = ML_HEADS
            dv = d // heads
            n_main = 2 * heads * (dv // 2) + 2 * heads * dv
            w_in = jnp.pad(c_w_in[ic], ((0, 0), (0, V7X_LANES - 2 * heads)))
            b_in = jnp.pad(c_b_in[ic], ((0, V7X_LANES - 2 * heads),))
            assert w_in.shape[1] == n_main + V7X_LANES
            z = mix_in(w_in, b_in, tn=896, tm_bf16=1024)
            hs = _mlstm_core(z, c_w_conv[ic], c_b_conv[ic], c_mh_g[ic], bsz=bsz, seq=seq, d=d)
            xf, hf = proj_ln(hs, c_w_out[ic], c_b_out[ic])
            ic += 1
        else:
            zz = mix_in(d_w_in[idd], d_b_in[idd], act="gelu")
            gated = _gmlp_core(zz, d_ln_g[idd], d_ln_b[idd], d_w_sp[idd], d_b_sp[idd], seq=seq)
            xf, hf = proj_ln(gated, d_w_out[idd], d_b_out[idd])
            idd += 1
        if not is_moe:
            tm_ffn = _pick(n, 1024)
            tile_g = jnp.full((n // tm_ffn,), i_dense, jnp.int32)
            tile_rows = jnp.full((n // tm_ffn,), tm_ffn, jnp.int32)
            y = _grouped_ffn(hf, f_w1, f_w3, f_w2, tile_g, tile_rows, tm=tm_ffn)
            res = _ln_res(xf, y, g2, ln2_g[layer], ln2_b[layer], alpha=alpha, rows_per_batch=seq, mod=mod_next)
            i_dense += 1
        else:
            merge = lambda w: w.reshape((w.shape[0] * w.shape[1],) + w.shape[2:])
            res = _moe_layer(xf, hf, g2, ln2_g[layer], ln2_b[layer], m_router[i_moe], merge(m_w1),
                             merge(m_w3), merge(m_w2), i_moe, alpha=alpha, rows_per_batch=seq, mod_next=mod_next)
            i_moe += 1
        xf, hm = res if mod_next is not None else (res, None)
    return xf.reshape(bsz, seq, d)
```

```python
import functools
import math

import jax
import jax.numpy as jnp
from jax import lax
from jax.experimental import pallas as pl
from jax.experimental.pallas import tpu as pltpu

F32 = jnp.float32
BF16 = jnp.bfloat16

LN_EPS = 1e-5
N_MIXERS = 4
CONV_WIDTH = 31
S5_GROUP = 16
S5_STATE = 64
ML_HEADS = 8
ML_CONV = 4
GM_CHUNK = 128
GM_GROUPS = 8
N_EXPERTS = 8
TOP_K = 2

V7X_VMEM_BYTES = 64 * 1024 * 1024
V7X_LANES = 128
V7X_SUBLANES = 8
VMEM_LIMIT = V7X_VMEM_BYTES - 8 * 1024 * 1024

S5_CHUNK = 16
ML_CHUNK_ROWS = 256
CONV_HALO = 32
DMA_UNROLL = 8


def _cparams(n_axes):
    return pltpu.CompilerParams(dimension_semantics=("arbitrary",) * n_axes,
                                vmem_limit_bytes=VMEM_LIMIT)


def _pick(n, pref):
    t = min(n, pref)
    while n % t:
        t //= 2
    return t


def _sigmoid(x):
    return 1.0 / (1.0 + jnp.exp(-x))


def _silu(x):
    return x * _sigmoid(x)


def _layer_norm_rows(v, gamma, beta):
    mu = jnp.mean(v, axis=-1, keepdims=True)
    d = v - mu
    var = jnp.mean(d * d, axis=-1, keepdims=True)
    y = d * lax.rsqrt(var + LN_EPS)
    if gamma is not None:
        y = y * gamma
    if beta is not None:
        y = y + beta
    return y


def _bdot(a, b):
    return jnp.dot(a, b, preferred_element_type=F32)


def _ada_kernel(c_ref, w_ref, b_ref, o_ref):
    cond = _silu(c_ref[...]).astype(BF16)
    o_ref[0] = _bdot(cond, w_ref[0].astype(BF16)) + b_ref[0]


def _ada_mods(c, ada_w, ada_b):
    depth, d, d6 = ada_w.shape
    bsz = c.shape[0]
    tn = _pick(d6, 1024)
    return pl.pallas_call(
        _ada_kernel,
        grid=(depth, d6 // tn),
        in_specs=[pl.BlockSpec((bsz, d), lambda l, j: (0, 0)),
                  pl.BlockSpec((1, d, tn), lambda l, j: (l, 0, j)),
                  pl.BlockSpec((1, 1, tn), lambda l, j: (l, 0, j))],
        out_specs=pl.BlockSpec((1, bsz, tn), lambda l, j: (l, 0, j)),
        out_shape=jax.ShapeDtypeStruct((depth, bsz, d6), F32),
        compiler_params=_cparams(2),
        name="ada_mods",
    )(c, ada_w, ada_b.reshape(depth, 1, d6))


def _mm_kernel(*refs, n_w, has_mod, use_scratch, act):
    it = iter(refs)
    x_ref = next(it)
    sc_ref = sh_ref = None
    if has_mod:
        sc_ref, sh_ref = next(it), next(it)
    w_refs = [next(it) for _ in range(n_w)]
    b_refs = [next(it) for _ in range(n_w)]
    o_ref = next(it)
    if use_scratch:
        xb_ref = next(it)

        @pl.when(pl.program_id(1) == 0)
        def _():
            xv = x_ref[...].astype(F32)
            if has_mod:
                xv = xv * (1.0 + sc_ref[0]) + sh_ref[0]
            xb_ref[...] = xv.astype(BF16)

        xb = xb_ref[...]
    else:
        xb = x_ref[...]
    z = [_bdot(xb, w_refs[k][...].astype(BF16)) + b_refs[k][...] for k in range(n_w)]
    if act == "glu":
        out = z[0] * _sigmoid(z[1])
    elif act == "gelu":
        out = jax.nn.gelu(z[0])
    else:
        out = z[0]
    o_ref[...] = out.astype(o_ref.dtype)


def _matmul(x, w, b, *, rows_per_batch, mod=None, act=None, out_dtype=F32, tm=1024, tn=512):
    n, k = x.shape
    nw = w.shape[1]
    n_out = nw // 2 if act == "glu" else nw
    tm = _pick(rows_per_batch, tm)
    tn = _pick(n_out, tn)
    tiles_per_batch = rows_per_batch // tm
    has_mod = mod is not None
    use_scratch = has_mod or x.dtype != BF16
    n_w = 2 if act == "glu" else 1
    half = n_out // tn

    in_specs = [pl.BlockSpec((tm, k), lambda i, j: (i, 0))]
    args = [x]
    if has_mod:
        mspec = pl.BlockSpec((1, 1, k), lambda i, j: (i // tiles_per_batch, 0, 0))
        in_specs += [mspec, mspec]
        args += [mod[0], mod[1]]
    b2 = b.reshape(1, nw)
    in_specs.append(pl.BlockSpec((k, tn), lambda i, j: (0, j)))
    args.append(w)
    if n_w == 2:
        in_specs.append(pl.BlockSpec((k, tn), lambda i, j: (0, j + half)))
        args.append(w)
    in_specs.append(pl.BlockSpec((1, tn), lambda i, j: (0, j)))
    args.append(b2)
    if n_w == 2:
        in_specs.append(pl.BlockSpec((1, tn), lambda i, j: (0, j + half)))
        args.append(b2)
    scratch = [pltpu.VMEM((tm, k), BF16)] if use_scratch else []
    return pl.pallas_call(
        functools.partial(_mm_kernel, n_w=n_w, has_mod=has_mod, use_scratch=use_scratch, act=act),
        grid=(n // tm, n_out // tn),
        in_specs=in_specs,
        out_specs=pl.BlockSpec((tm, tn), lambda i, j: (i, j)),
        out_shape=jax.ShapeDtypeStruct((n, n_out), out_dtype),
        scratch_shapes=scratch,
        compiler_params=_cparams(2),
        name="matmul_" + (act or "bias"),
    )(*args)


def _ln_res_kernel(*refs, alpha, emit_h):
    if emit_h:
        x_ref, y_ref, g_ref, gam_ref, bet_ref, sc_ref, sh_ref, o_ref, h_ref = refs
    else:
        x_ref, y_ref, g_ref, gam_ref, bet_ref, o_ref = refs
    r = alpha * x_ref[...] + g_ref[0] * y_ref[...].astype(F32)
    xn = _layer_norm_rows(r, gam_ref[...], bet_ref[...])
    o_ref[...] = xn
    if emit_h:
        h_ref[...] = (xn * (1.0 + sc_ref[0]) + sh_ref[0]).astype(h_ref.dtype)


def _ln_res(x, y, g, gamma, beta, *, alpha, rows_per_batch, mod=None, h_dtype=BF16, tm=512):
    n, d = x.shape
    tm = _pick(rows_per_batch, tm)
    tpb = rows_per_batch // tm
    row = pl.BlockSpec((tm, d), lambda i: (i, 0))
    per_b = pl.BlockSpec((1, 1, d), lambda i: (i // tpb, 0, 0))
    vec = pl.BlockSpec((1, d), lambda i: (0, 0))
    emit_h = mod is not None
    in_specs = [row, row, per_b, vec, vec]
    args = [x, y, g, gamma.reshape(1, d), beta.reshape(1, d)]
    out_shape = jax.ShapeDtypeStruct((n, d), F32)
    out_specs = row
    if emit_h:
        in_specs += [per_b, per_b]
        args += [mod[0], mod[1]]
        out_shape = (out_shape, jax.ShapeDtypeStruct((n, d), h_dtype))
        out_specs = (row, row)
    return pl.pallas_call(
        functools.partial(_ln_res_kernel, alpha=alpha, emit_h=emit_h),
        grid=(n // tm,),
        in_specs=in_specs,
        out_specs=out_specs,
        out_shape=out_shape,
        compiler_params=_cparams(1),
        name="ln_res",
    )(*args)


def _proj_ln_kernel(h_ref, w_ref, b_ref, x_ref, g_ref, gam_ref, bet_ref, sc_ref, sh_ref, o_ref, h2_ref, wb_ref,
                    *, alpha):
    @pl.when(pl.program_id(0) == 0)
    def _():
        wb_ref[...] = w_ref[...].astype(BF16)

    tm = h_ref.shape[0]
    for r0 in range(0, tm, tm // 2):
        rows = slice(r0, r0 + tm // 2)
        y = _bdot(h_ref[rows, :], wb_ref[...]) + b_ref[...]
        xn = _layer_norm_rows(alpha * x_ref[rows, :] + g_ref[0] * y, gam_ref[...], bet_ref[...])
        o_ref[rows, :] = xn
        h2_ref[rows, :] = (xn * (1.0 + sc_ref[0]) + sh_ref[0]).astype(h2_ref.dtype)


def _proj_ln(h, w, b, x, g, gamma, beta, mod, *, alpha, rows_per_batch, h_dtype, tm=256):
    n, k = h.shape
    d = w.shape[1]
    tm = _pick(rows_per_batch, tm)
    tpb = rows_per_batch // tm
    per_b = pl.BlockSpec((1, 1, d), lambda i: (i // tpb, 0, 0))
    vec = pl.BlockSpec((1, d), lambda i: (0, 0))
    row = pl.BlockSpec((tm, d), lambda i: (i, 0))
    return pl.pallas_call(
        functools.partial(_proj_ln_kernel, alpha=alpha),
        grid=(n // tm,),
        in_specs=[pl.BlockSpec((tm, k), lambda i: (i, 0)),
                  pl.BlockSpec((k, d), lambda i: (0, 0), pipeline_mode=pl.Buffered(1)),
                  vec, row, per_b, vec, vec, per_b, per_b],
        out_specs=(row, row),
        out_shape=(jax.ShapeDtypeStruct((n, d), F32), jax.ShapeDtypeStruct((n, d), h_dtype)),
        scratch_shapes=[pltpu.VMEM((k, d), BF16)],
        compiler_params=_cparams(1),
        name="proj_ln",
    )(h, w, b.reshape(1, d), x, g, gamma.reshape(1, d), beta.reshape(1, d), mod[0], mod[1])


def _group_state(tg_ref, rows_ref):
    i = pl.program_id(1)
    active = rows_ref[i] > 0
    changed = jnp.logical_or(i == 0, tg_ref[i] != tg_ref[jnp.maximum(i - 1, 0)])
    return active, jnp.logical_and(active, changed)


def _ffn_up_kernel(tg_ref, nu_ref, x_ref, w1_ref, w3_ref, h_ref, w1b_ref, w3b_ref):
    active, recast = _group_state(tg_ref, nu_ref)

    @pl.when(recast)
    def _():
        w1b_ref[...] = w1_ref[0].astype(BF16)
        w3b_ref[...] = w3_ref[0].astype(BF16)

    @pl.when(active)
    def _():
        x = x_ref[...]
        a = _bdot(x, w1b_ref[...])
        b = _bdot(x, w3b_ref[...])
        h_ref[...] = (_silu(a) * b).astype(h_ref.dtype)

    @pl.when(jnp.logical_not(active))
    def _():
        h_ref[...] = jnp.zeros(h_ref.shape, h_ref.dtype)


def _ffn_down_kernel(tg_ref, nu_ref, h_ref, w2_ref, o_ref, w2b_ref):
    active, recast = _group_state(tg_ref, nu_ref)

    @pl.when(recast)
    def _():
        w2b_ref[...] = w2_ref[0].astype(BF16)

    @pl.when(active)
    def _():
        o_ref[...] = _bdot(h_ref[...], w2b_ref[...])

    @pl.when(jnp.logical_not(active))
    def _():
        o_ref[...] = jnp.zeros(o_ref.shape, o_ref.dtype)


def _grouped_ffn(xb, w1, w3, w2, tile_g, tile_rows, *, tm, tf=512, tm_down=512, tn=512):
    r, d = xb.shape
    ff = w1.shape[2]
    tf = _pick(ff, tf)
    tn = _pick(d, tn)
    tm_down = _pick(tm, tm_down)
    n_tiles = r // tm
    h = pl.pallas_call(
        _ffn_up_kernel,
        grid_spec=pltpu.PrefetchScalarGridSpec(
            num_scalar_prefetch=2,
            grid=(ff // tf, n_tiles),
            in_specs=[pl.BlockSpec((tm, d), lambda f, i, tg, nu: (i, 0)),
                      pl.BlockSpec((1, d, tf), lambda f, i, tg, nu: (tg[i], 0, f)),
                      pl.BlockSpec((1, d, tf), lambda f, i, tg, nu: (tg[i], 0, f))],
            out_specs=pl.BlockSpec((tm, tf), lambda f, i, tg, nu: (i, f)),
            scratch_shapes=[pltpu.VMEM((d, tf), BF16), pltpu.VMEM((d, tf), BF16)]),
        out_shape=jax.ShapeDtypeStruct((r, ff), BF16),
        compiler_params=_cparams(2),
        name="ffn_up",
    )(tile_g, tile_rows, xb, w1, w3)
    sub = tm // tm_down
    tile_g2 = jnp.repeat(tile_g, sub)
    rows2 = jnp.repeat(tile_rows, sub) - jnp.tile(jnp.arange(sub, dtype=jnp.int32) * tm_down, n_tiles)
    return pl.pallas_call(
        _ffn_down_kernel,
        grid_spec=pltpu.PrefetchScalarGridSpec(
            num_scalar_prefetch=2,
            grid=(d // tn, n_tiles * sub),
            in_specs=[pl.BlockSpec((tm_down, ff), lambda n, i, tg, nu: (i, 0)),
                      pl.BlockSpec((1, ff, tn), lambda n, i, tg, nu: (tg[i], 0, n))],
            out_specs=pl.BlockSpec((tm_down, tn), lambda n, i, tg, nu: (i, n)),
            scratch_shapes=[pltpu.VMEM((ff, tn), BF16)]),
        out_shape=jax.ShapeDtypeStruct((r, d), F32),
        compiler_params=_cparams(2),
        name="ffn_down",
    )(tile_g2, rows2, h, w2)


def _conv_kernel(x_ref, w_ref, b_ref, g_ref, be_ref, o_ref, xx_ref, u_ref, *, ts, kw, rc, cc):
    i = pl.program_id(1)
    d = x_ref.shape[1]
    halo = CONV_HALO

    @pl.when(i == 0)
    def _():
        xx_ref[0:halo, :] = jnp.zeros((halo, d), F32)

    @pl.when(i > 0)
    def _():
        xx_ref[0:halo, :] = xx_ref[ts:ts + halo, :]

    xx_ref[halo:halo + ts, :] = x_ref[...]
    off = halo - (kw - 1)

    sub = V7X_SUBLANES
    for r0 in range(0, ts, rc):
        for c0 in range(0, d, cc):
            acc = jnp.broadcast_to(b_ref[:, c0:c0 + cc], (rc, cc))
            for rho in range(sub):
                taps = [j for j in range(kw) if (off + j) % sub == rho]
                if not taps:
                    continue
                base = r0 + off + taps[0]
                win = xx_ref[base:base + rc + taps[-1] - taps[0], c0:c0 + cc]
                part = None
                for j in taps:
                    o = j - taps[0]
                    term = w_ref[j:j + 1, c0:c0 + cc] * win[o:o + rc, :]
                    part = term if part is None else part + term
                acc = acc + part
            u_ref[r0:r0 + rc, c0:c0 + cc] = acc
    y = _layer_norm_rows(u_ref[...], g_ref[...], be_ref[...])
    o_ref[...] = _silu(y).astype(o_ref.dtype)


def _conv_ln_swish(glu, w_dw, b_dw, ln_g, ln_b, *, bsz, seq, ts=128):
    n, d = glu.shape
    kw = w_dw.shape[0]
    ts = _pick(seq, ts)
    nt = seq // ts
    rc = _pick(ts, 64)
    cc = _pick(d, 256)
    row = pl.BlockSpec((ts, d), lambda b, i: (b * nt + i, 0))
    vec = pl.BlockSpec((1, d), lambda b, i: (0, 0))
    return pl.pallas_call(
        functools.partial(_conv_kernel, ts=ts, kw=kw, rc=rc, cc=cc),
        grid=(bsz, nt),
        in_specs=[row, pl.BlockSpec((kw, d), lambda b, i: (0, 0)), vec, vec, vec],
        out_specs=row,
        out_shape=jax.ShapeDtypeStruct((n, d), BF16),
        scratch_shapes=[pltpu.VMEM((CONV_HALO + ts, d), F32), pltpu.VMEM((ts, d), F32)],
        compiler_params=_cparams(2),
        name="conv_ln_swish",
    )(glu, w_dw, b_dw.reshape(1, d), ln_g.reshape(1, d), ln_b.reshape(1, d))


def _iota_div(x, k):
    assert k & (k - 1) == 0
    return lax.shift_right_logical(x, k.bit_length() - 1)


def _iota_mod(x, k):
    assert k & (k - 1) == 0
    return lax.bitwise_and(x, k - 1)


def _s5_expand_tables(kq_ref, mq_ref, cq_ref, t_scr, m_scr, c_scr, *, lc, p):
    lanes = t_scr.shape[0] // lc
    ns2 = m_scr.shape[1]
    ns = ns2 // 2
    nst = ns // (lanes // p)
    iota = lambda shape, dim: lax.broadcasted_iota(jnp.int32, shape, dim)
    r, c = iota((lanes, lanes), 0), iota((lanes, lanes), 1)
    same_g = _iota_div(r, p) == _iota_div(c, p)
    e_t = (_iota_mod(iota((p, lanes), 1), p) == iota((p, lanes), 0)).astype(BF16)
    tiles = [jnp.where(same_g, _bdot(kq_ref[0, tau].astype(BF16), e_t), 0.0).astype(BF16) for tau in range(lc)]
    zero_tile = jnp.zeros((lanes, lanes), BF16)
    for s in range(lc):
        for t in range(lc):
            t_scr[s * lanes:(s + 1) * lanes, t * lanes:(t + 1) * lanes] = tiles[t - s] if t >= s else zero_tile
    r, c = iota((lanes, ns2), 0), iota((lanes, ns2), 1)
    e_m = (r == _iota_div(c, ns) * nst + _iota_mod(c, nst)).astype(BF16)
    mask_m = _iota_div(r, p) == _iota_div(_iota_mod(c, ns), nst)
    for s in range(lc):
        m_scr[s * lanes:(s + 1) * lanes, :] = jnp.where(
            mask_m, _bdot(mq_ref[0, s].astype(BF16), e_m), 0.0).astype(BF16)
    r, c = iota((ns2, lanes), 0), iota((ns2, lanes), 1)
    e_c = (c == _iota_div(r, ns) * nst + _iota_mod(r, nst)).astype(BF16)
    mask_c = _iota_div(_iota_mod(r, ns), nst) == _iota_div(c, p)
    for t in range(lc):
        c_scr[:, t * lanes:(t + 1) * lanes] = jnp.where(
            mask_c, _bdot(e_c, cq_ref[0, t].astype(BF16)), 0.0).astype(BF16)


def _s5_kernel(u_ref, kq_ref, mq_ref, cq_ref, lp_ref, d_ref, o_ref,
               t_scr, m_scr, c_scr, a_ref, v_ref, p_ref, *, lc, nc, p):
    @pl.when(pl.program_id(1) == 0)
    def _():
        _s5_expand_tables(kq_ref, mq_ref, cq_ref, t_scr, m_scr, c_scr, lc=lc, p=p)

    lanes = u_ref.shape[2]
    for t in range(lc):
        a_ref[:, t * lanes:(t + 1) * lanes] = u_ref[0, pl.ds(t, nc, stride=lc), :]
    a = a_ref[...]
    ab = a.astype(BF16)
    v_ref[...] = _bdot(ab, m_scr[...])
    pw_r = lp_ref[0, 0]
    pw_i = lp_ref[0, 1]
    ns = pw_r.shape[1]

    slab = V7X_SUBLANES
    srow = lax.broadcasted_iota(jnp.int32, (slab, ns), 0)

    def step(k, carry):
        cr, ci = carry
        r0 = pl.multiple_of(k * slab, slab)
        yr = v_ref[pl.ds(r0, slab), 0:ns]
        yi = v_ref[pl.ds(r0, slab), ns:2 * ns]
        sh = 1
        while sh < slab:
            mr, mi = pw_r[sh - 1:sh, :], pw_i[sh - 1:sh, :]
            zr = jnp.where(srow >= sh, pltpu.roll(yr, sh, 0), 0.0)
            zi = jnp.where(srow >= sh, pltpu.roll(yi, sh, 0), 0.0)
            yr, yi = yr + mr * zr - mi * zi, yi + mr * zi + mi * zr
            sh *= 2
        sr = yr + pw_r * cr - pw_i * ci
        si = yi + pw_r * ci + pw_i * cr
        p_ref[pl.ds(r0, slab), 0:ns] = jnp.where(srow == 0, cr, pltpu.roll(sr, 1, 0))
        p_ref[pl.ds(r0, slab), ns:2 * ns] = jnp.where(srow == 0, ci, pltpu.roll(si, 1, 0))
        return sr[slab - 1:slab, :], si[slab - 1:slab, :]

    zero = jnp.zeros((1, ns), F32)
    lax.fori_loop(0, nc // slab, step, (zero, zero))
    y = _bdot(ab, t_scr[...]) + _bdot(p_ref[...].astype(BF16), c_scr[...]) + d_ref[0] * a
    y = jax.nn.gelu(y)
    for t in range(lc):
        o_ref[0, pl.ds(t, nc, stride=lc), :] = y[:, t * lanes:(t + 1) * lanes]


def _s5_tables(a_re, a_im, log_dt, b_re, b_im, c_re, c_im, d_skip):
    g, n = a_re.shape
    p = b_re.shape[2]
    lc = S5_CHUNK
    ar, ai = a_re.astype(F32), a_im.astype(F32)
    dt = jnp.exp(log_dt.astype(F32))[:, None]
    decay = jnp.exp(ar * dt)
    lr, li = decay * jnp.cos(ai * dt), decay * jnp.sin(ai * dt)
    den = ar * ar + ai * ai
    zr = ((lr - 1.0) * ar + li * ai) / den
    zi = (li * ar - (lr - 1.0) * ai) / den
    br, bi = b_re.astype(F32), b_im.astype(F32)
    bbr = zr[..., None] * br - zi[..., None] * bi
    bbi = zr[..., None] * bi + zi[..., None] * br
    tau = jnp.arange(lc + 1, dtype=F32)[:, None, None]
    pdec = jnp.exp(tau * (ar * dt)[None])
    pr, pi = pdec * jnp.cos(tau * (ai * dt)[None]), pdec * jnp.sin(tau * (ai * dt)[None])
    cr, ci = c_re.astype(F32), c_im.astype(F32)
    hp = lax.Precision.HIGHEST
    lbr = pr[..., None] * bbr[None] - pi[..., None] * bbi[None]
    lbi = pr[..., None] * bbi[None] + pi[..., None] * bbr[None]
    ktau = (jnp.einsum('gpn,tgnq->tgpq', cr, lbr[:lc], precision=hp)
            - jnp.einsum('gpn,tgnq->tgpq', ci, lbi[:lc], precision=hp))
    gb = V7X_LANES // p
    nj = g // gb
    kq = ktau.reshape(lc, nj, gb, p, p).transpose(1, 0, 2, 4, 3).reshape(nj, lc, gb * p, p)
    rev = lc - 1 - jnp.arange(lc)
    mq = jnp.stack([lbr[rev], lbi[rev]], axis=2).reshape(lc, nj, gb, 2, n, p)
    mq = mq.transpose(1, 0, 2, 5, 3, 4).reshape(nj, lc, gb * p, 2 * n)
    pr1, pi1 = pr[1:], pi[1:]
    cmr = (cr[None] * pr1[:, :, None, :] - ci[None] * pi1[:, :, None, :])
    cmi = -(cr[None] * pi1[:, :, None, :] + ci[None] * pr1[:, :, None, :])
    cq = jnp.stack([cmr, cmi], axis=0).reshape(2, lc, nj, gb, p, n)
    cq = cq.transpose(2, 1, 0, 5, 3, 4).reshape(nj, lc, 2 * n, gb * p)
    mpw = (lc * jnp.arange(1, V7X_SUBLANES + 1, dtype=F32))[:, None, None]
    cdec = jnp.exp(mpw * (ar * dt)[None])
    lam_pw = jnp.stack([cdec * jnp.cos(mpw * (ai * dt)[None]), cdec * jnp.sin(mpw * (ai * dt)[None])], axis=0)
    lam_pw = lam_pw.reshape(2, V7X_SUBLANES, nj, gb * n).transpose(2, 0, 1, 3)
    dvec = jnp.broadcast_to(d_skip.astype(F32).reshape(nj, 1, 1, gb * p), (nj, 1, lc, gb * p))
    return kq, mq, cq, lam_pw, dvec.reshape(nj, 1, lc * gb * p)


def _s5_core(u, tables, *, bsz, seq):
    n_rows, d = u.shape
    kq, mq, cq, lam_pw, dvec = tables
    nj, lc, lanes, p = kq.shape
    ns = lam_pw.shape[3]
    nc = seq // lc
    wk = lc * lanes
    assert lanes == V7X_LANES and nc % V7X_SUBLANES == 0 and nj * lanes == d
    tab = lambda *shape: pl.BlockSpec((1,) + shape, lambda j, b: (j,) + (0,) * len(shape))
    seq_blk = pl.BlockSpec((1, seq, lanes), lambda j, b: (b, 0, j))
    y = pl.pallas_call(
        functools.partial(_s5_kernel, lc=lc, nc=nc, p=p),
        grid=(nj, bsz),
        in_specs=[seq_blk, tab(lc, lanes, p), tab(lc, lanes, lanes), tab(lc, lanes, lanes),
                  tab(2, V7X_SUBLANES, ns), tab(1, wk)],
        out_specs=seq_blk,
        out_shape=jax.ShapeDtypeStruct((bsz, seq, d), F32),
        scratch_shapes=[pltpu.VMEM((wk, wk), BF16), pltpu.VMEM((wk, 2 * ns), BF16), pltpu.VMEM((2 * ns, wk), BF16),
                        pltpu.VMEM((nc, wk), F32), pltpu.VMEM((nc, 2 * ns), F32), pltpu.VMEM((nc, 2 * ns), F32)],
        compiler_params=_cparams(2),
        name="s5_core",
    )(u.reshape(bsz, seq, d), kq, mq, cq, lam_pw, dvec)
    return y.reshape(n_rows, d)


def _log_sigmoid(x):
    return jnp.minimum(x, 0.0) - jnp.log(1.0 + jnp.exp(-jnp.abs(x)))


def _mlstm_kernel(q_ref, k_ref, v_ref, o_ref, g_ref, cw_ref, cb_ref, mh_ref, out_ref,
                  c_st, n_st, m_st, qx_ref, kx_ref, *, lm, dk, heads, kw):
    c = pl.program_id(1)
    pad = V7X_SUBLANES
    hdk = heads * dk
    dv = v_ref.shape[1] // heads

    @pl.when(c == 0)
    def _():
        c_st[...] = jnp.zeros(c_st.shape, F32)
        n_st[...] = jnp.zeros(n_st.shape, F32)
        m_st[...] = jnp.zeros(m_st.shape, F32)
        qx_ref[0:pad, :] = jnp.zeros((pad, hdk), F32)
        kx_ref[0:pad, :] = jnp.zeros((pad, hdk), F32)

    def conv_swish(x_ref, xx_ref, c0):
        xx_ref[pad:pad + lm, :] = x_ref[...]
        acc = jnp.broadcast_to(cb_ref[:, c0:c0 + hdk], (lm, hdk))
        for j in range(kw):
            acc = acc + cw_ref[j:j + 1, c0:c0 + hdk] * xx_ref[pad - (kw - 1) + j:pad - (kw - 1) + j + lm, :]
        xx_ref[0:pad, :] = xx_ref[lm:lm + pad, :]
        return _silu(acc)

    q_all = conv_swish(q_ref, qx_ref, 0)
    k_all = conv_swish(k_ref, kx_ref, hdk) * (dk ** -0.5)

    gts = g_ref[...]
    lf = _log_sigmoid(gts)
    rows = lax.broadcasted_iota(jnp.int32, (lm, lm), 0)
    cols = lax.broadcasted_iota(jnp.int32, (lm, lm), 1)
    causal = rows >= cols
    tri = causal.astype(BF16)
    lf_hi = lf.astype(BF16)
    r1 = lf - lf_hi.astype(F32)
    lf_mid = r1.astype(BF16)
    lf_lo = (r1 - lf_mid.astype(F32)).astype(BF16)
    bcum = _bdot(tri, lf_hi) + _bdot(tri, lf_mid) + _bdot(tri, lf_lo)

    bcum_t = bcum.T
    gts_t = gts.T
    for h in range(heads):
        q = q_all[:, h * dk:(h + 1) * dk]
        k = k_all[:, h * dk:(h + 1) * dk]
        vcols = slice(h * dv, (h + 1) * dv)
        b_col = bcum[:, heads + h:heads + h + 1]
        li_col = gts[:, h:h + 1]
        b_row = bcum_t[heads + h:heads + h + 1, :]
        li_row = gts_t[h:h + 1, :]

        m_prev = m_st[h]
        dmat = jnp.where(causal, b_col - b_row + li_row, -jnp.inf)
        inter = b_col + m_prev
        m_j = jnp.maximum(inter, jnp.max(dmat, axis=1, keepdims=True))
        w_intra = jnp.exp(dmat - m_j)
        w_inter = jnp.exp(inter - m_j)
        qb, kb, vb = q.astype(BF16), k.astype(BF16), v_ref[:, vcols].astype(BF16)
        sc = lax.dot_general(qb, kb, (((1,), (1,)), ((), ())), preferred_element_type=F32) * w_intra
        num = w_inter * _bdot(qb, c_st[h].astype(BF16)) + _bdot(sc.astype(BF16), vb)
        den = w_inter * jnp.sum(q * n_st[h], axis=1, keepdims=True) + jnp.sum(sc, axis=1, keepdims=True)
        hb = num / jnp.maximum(jnp.abs(den), jnp.exp(-m_j))

        bl = b_col[lm - 1:lm, :]
        gsum = bl - b_col + li_col
        m_new = jnp.maximum(bl + m_prev, jnp.max(gsum, axis=0, keepdims=True))
        wc = jnp.exp(bl + m_prev - m_new)
        wk = jnp.exp(gsum - m_new)
        kwt = wk * k
        c_st[h] = wc * c_st[h] + _bdot(kwt.T.astype(BF16), vb)
        n_st[h] = wc * n_st[h] + jnp.sum(kwt, axis=0, keepdims=True)
        m_st[h] = m_new

        hs = _sigmoid(o_ref[:, vcols]) * hb
        out_ref[:, vcols] = (_layer_norm_rows(hs, None, None) * mh_ref[:, vcols]).astype(out_ref.dtype)


def _mlstm_core(z, w_conv, b_conv, mh_g, *, bsz, seq, d):
    n_rows = z.shape[0]
    heads = ML_HEADS
    dv = d // heads
    dk = dv // 2
    lm = _pick(seq, ML_CHUNK_ROWS)
    nc = seq // lm
    kw = w_conv.shape[0]
    hdk = heads * dk
    hdv = heads * dv
    assert 2 * hdk == hdv
    gate_blk = (2 * hdk + 2 * hdv) // V7X_LANES
    row = lambda b, c: b * nc + c
    in_specs = [
        pl.BlockSpec((lm, hdk), lambda b, c: (row(b, c), 0)),
        pl.BlockSpec((lm, hdk), lambda b, c: (row(b, c), 1)),
        pl.BlockSpec((lm, hdv), lambda b, c: (row(b, c), 1)),
        pl.BlockSpec((lm, hdv), lambda b, c: (row(b, c), 2)),
        pl.BlockSpec((lm, V7X_LANES), lambda b, c: (row(b, c), gate_blk)),
        pl.BlockSpec((kw, 2 * hdk), lambda b, c: (0, 0)),
        pl.BlockSpec((1, 2 * hdk), lambda b, c: (0, 0)),
        pl.BlockSpec((1, d), lambda b, c: (0, 0)),
    ]
    return pl.pallas_call(
        functools.partial(_mlstm_kernel, lm=lm, dk=dk, heads=heads, kw=kw),
        grid=(bsz, nc),
        in_specs=in_specs,
        out_specs=pl.BlockSpec((lm, d), lambda b, c: (row(b, c), 0)),
        out_shape=jax.ShapeDtypeStruct((n_rows, d), BF16),
        scratch_shapes=[pltpu.VMEM((heads, dk, dv), F32), pltpu.VMEM((heads, 1, dk), F32),
                        pltpu.VMEM((heads, 1, 1), F32),
                        pltpu.VMEM((lm + V7X_SUBLANES, hdk), F32), pltpu.VMEM((lm + V7X_SUBLANES, hdk), F32)],
        compiler_params=_cparams(2),
        name="mlstm_core",
    )(z, z, z, z, z, w_conv, b_conv.reshape(1, -1), mh_g.reshape(1, d))


def _gmlp_kernel(u_ref, v_ref, g_ref, be_ref, w_ref, bsp_ref, o_ref, *, lg, groups, dg, nsub):
    rows = lax.broadcasted_iota(jnp.int32, (lg, lg), 0)
    cols = lax.broadcasted_iota(jnp.int32, (lg, lg), 1)
    keep = rows >= cols
    wm = [jnp.where(keep, w_ref[g], 0.0).astype(BF16) for g in range(groups)]
    for s in range(nsub):
        r = slice(s * lg, (s + 1) * lg)
        vn = _layer_norm_rows(v_ref[r, :], g_ref[...], be_ref[...]).astype(BF16)
        for g in range(groups):
            cs = slice(g * dg, (g + 1) * dg)
            sv = _bdot(wm[g], vn[:, cs]) + bsp_ref[:, g:g + 1]
            o_ref[r, cs] = (u_ref[r, cs] * sv).astype(o_ref.dtype)


def _gmlp_core(zz, ln_g, ln_b, w_sp, b_sp, *, seq, tm=512):
    n, d2 = zz.shape
    d = d2 // 2
    groups, lg, _ = w_sp.shape
    tm = max(lg, _pick(seq, tm))
    return pl.pallas_call(
        functools.partial(_gmlp_kernel, lg=lg, groups=groups, dg=d // groups, nsub=tm // lg),
        grid=(n // tm,),
        in_specs=[pl.BlockSpec((tm, d), lambda i: (i, 0)),
                  pl.BlockSpec((tm, d), lambda i: (i, 1)),
                  pl.BlockSpec((1, d), lambda i: (0, 0)),
                  pl.BlockSpec((1, d), lambda i: (0, 0)),
                  pl.BlockSpec((groups, lg, lg), lambda i: (0, 0, 0)),
                  pl.BlockSpec((lg, groups), lambda i: (0, 0))],
        out_specs=pl.BlockSpec((tm, d), lambda i: (i, 0)),
        out_shape=jax.ShapeDtypeStruct((n, d), BF16),
        compiler_params=_cparams(1),
        name="gmlp_spatial",
    )(zz, zz, ln_g.reshape(1, d), ln_b.reshape(1, d), w_sp, b_sp.T)


def _router_kernel(h_ref, w_ref, e_ref, g_ref, *, n_exp):
    h = h_ref[...]
    w = w_ref[...]
    hh = h.astype(BF16)
    hl = (h - hh.astype(F32)).astype(BF16)
    wh = w.astype(BF16)
    wl = (w - wh.astype(F32)).astype(BF16)
    logits = _bdot(hh, wh) + _bdot(hl, wh) + _bdot(hh, wl)
    nl = logits.shape[1]
    lane = lax.broadcasted_iota(jnp.int32, logits.shape, 1).astype(F32)
    cur = jnp.where(lane < n_exp, logits, -jnp.inf)
    vals, idxs = [], []
    for _ in range(TOP_K):
        m = jnp.max(cur, axis=1, keepdims=True)
        idx = jnp.min(jnp.where(cur == m, lane, float(nl)), axis=1, keepdims=True)
        vals.append(m)
        idxs.append(idx)
        cur = jnp.where(lane == idx, -jnp.inf, cur)
    ex = [jnp.exp(v - vals[0]) for v in vals]
    tot = ex[0]
    for t in ex[1:]:
        tot = tot + t
    e_out = jnp.zeros(logits.shape, F32)
    g_out = jnp.zeros(logits.shape, F32)
    for k in range(TOP_K):
        e_out = jnp.where(lane == k, idxs[k], e_out)
        g_out = jnp.where(lane == k, ex[k] / tot, g_out)
    e_ref[...] = e_out.astype(jnp.int32)
    g_ref[...] = g_out


def _router_topk(hf, w_router, *, tm=1024):
    n, d = hf.shape
    e = w_router.shape[1]
    wpad = jnp.pad(w_router, ((0, 0), (0, V7X_LANES - e)))
    tm = _pick(n, tm)
    out_blk = pl.BlockSpec((tm, V7X_LANES), lambda i: (i, 0))
    top_e, gates = pl.pallas_call(
        functools.partial(_router_kernel, n_exp=e),
        grid=(n // tm,),
        in_specs=[pl.BlockSpec((tm, d), lambda i: (i, 0)),
                  pl.BlockSpec((d, V7X_LANES), lambda i: (0, 0))],
        out_specs=(out_blk, out_blk),
        out_shape=(jax.ShapeDtypeStruct((n, V7X_LANES), jnp.int32), jax.ShapeDtypeStruct((n, V7X_LANES), F32)),
        compiler_params=_cparams(1),
        name="moe_router",
    )(hf, wpad)
    return top_e[:, :TOP_K], gates[:, :TOP_K]


def _moe_gather_kernel(nu_ref, tok_ref, tokn_ref, hf_hbm, o_ref, xg_ref, sem, *, tm):
    i = pl.program_id(0)
    nu = nu_ref[0]
    slot = lax.rem(i, 2)

    def row_copy(tref, r, s):
        return pltpu.make_async_copy(hf_hbm.at[pl.ds(tref[0, 0, r], 1), :], xg_ref.at[s, pl.ds(r, 1), :], sem.at[s])

    def start_tile(tref, s):
        def issue(r, carry):
            row_copy(tref, r, s).start()
            return carry

        lax.fori_loop(0, tm, issue, 0, unroll=DMA_UNROLL)

    @pl.when(jnp.logical_and(i == 0, nu > 0))
    def _():
        start_tile(tok_ref, 0)

    @pl.when(i + 1 < nu)
    def _():
        start_tile(tokn_ref, 1 - slot)

    @pl.when(i < nu)
    def _():
        pltpu.make_async_copy(hf_hbm.at[pl.ds(0, tm), :], xg_ref.at[slot], sem.at[slot]).wait()
        o_ref[...] = xg_ref[slot].astype(o_ref.dtype)

    @pl.when(i >= nu)
    def _():
        o_ref[...] = jnp.zeros(o_ref.shape, o_ref.dtype)


def _moe_gather(hf, row_tok, n_used, *, tm):
    n, d = hf.shape
    n_tiles = row_tok.shape[0]
    tok_spec = lambda fn: pl.BlockSpec((1, 1, tm), fn, memory_space=pltpu.SMEM)
    return pl.pallas_call(
        functools.partial(_moe_gather_kernel, tm=tm),
        grid_spec=pltpu.PrefetchScalarGridSpec(
            num_scalar_prefetch=1,
            grid=(n_tiles,),
            in_specs=[tok_spec(lambda i, nu: (i, 0, 0)),
                      tok_spec(lambda i, nu: (jnp.minimum(i + 1, n_tiles - 1), 0, 0)),
                      pl.BlockSpec(memory_space=pl.ANY)],
            out_specs=pl.BlockSpec((tm, d), lambda i, nu: (i, 0)),
            scratch_shapes=[pltpu.VMEM((2, tm, d), hf.dtype), pltpu.SemaphoreType.DMA((2,))]),
        out_shape=jax.ShapeDtypeStruct((n_tiles * tm, d), BF16),
        compiler_params=_cparams(1),
        name="moe_gather",
    )(n_used, row_tok, row_tok, hf)


def _moe_combine_kernel(*refs, tm, alpha, emit_h):
    if emit_h:
        (pos_ref, posn_ref, y_hbm, x_ref, gate_ref, g_ref, gam_ref, bet_ref, sc_ref, sh_ref,
         o_ref, h_ref, yb_ref, sem) = refs
    else:
        pos_ref, posn_ref, y_hbm, x_ref, gate_ref, g_ref, gam_ref, bet_ref, o_ref, yb_ref, sem = refs
    i = pl.program_id(0)
    nt = pl.num_programs(0)
    slot = lax.rem(i, 2)

    def row_copy(pref, k, r, s):
        return pltpu.make_async_copy(y_hbm.at[pl.ds(pref[0, 0, k * tm + r], 1), :],
                                     yb_ref.at[s, k, pl.ds(r, 1), :], sem.at[s])

    def start_tile(pref, s):
        def issue(r, carry):
            for k in range(TOP_K):
                row_copy(pref, k, r, s).start()
            return carry

        lax.fori_loop(0, tm, issue, 0, unroll=DMA_UNROLL)

    @pl.when(i == 0)
    def _():
        start_tile(pos_ref, 0)

    @pl.when(i + 1 < nt)
    def _():
        start_tile(posn_ref, 1 - slot)

    for k in range(TOP_K):
        pltpu.make_async_copy(y_hbm.at[pl.ds(0, tm), :], yb_ref.at[slot, k], sem.at[slot]).wait()
    gate = gate_ref[...]
    y = gate[:, 0:1] * yb_ref[slot, 0]
    for k in range(1, TOP_K):
        y = y + gate[:, k:k + 1] * yb_ref[slot, k]
    r = alpha * x_ref[...] + g_ref[0] * y
    xn = _layer_norm_rows(r, gam_ref[...], bet_ref[...])
    o_ref[...] = xn
    if emit_h:
        h_ref[...] = (xn * (1.0 + sc_ref[0]) + sh_ref[0]).astype(h_ref.dtype)


def _moe_combine_ln(y_rows, pos, gates, x, g, gamma, beta, *, alpha, rows_per_batch, mod=None, tm=256):
    n, d = x.shape
    tm = _pick(rows_per_batch, tm)
    tpb = rows_per_batch // tm
    nt = n // tm
    pos3 = pos.reshape(nt, tm, TOP_K).transpose(0, 2, 1).reshape(nt, 1, TOP_K * tm)
    row = pl.BlockSpec((tm, d), lambda i: (i, 0))
    vec = pl.BlockSpec((1, d), lambda i: (0, 0))
    per_b = pl.BlockSpec((1, 1, d), lambda i: (i // tpb, 0, 0))
    emit_h = mod is not None
    in_specs = [pl.BlockSpec((1, 1, TOP_K * tm), lambda i: (i, 0, 0), memory_space=pltpu.SMEM),
                pl.BlockSpec((1, 1, TOP_K * tm), lambda i: (jnp.minimum(i + 1, nt - 1), 0, 0),
                             memory_space=pltpu.SMEM),
                pl.BlockSpec(memory_space=pl.ANY),
                row,
                pl.BlockSpec((tm, TOP_K), lambda i: (i, 0)),
                per_b, vec, vec]
    args = [pos3, pos3, y_rows, x, gates, g, gamma.reshape(1, d), beta.reshape(1, d)]
    out_specs, out_shape = row, jax.ShapeDtypeStruct((n, d), F32)
    if emit_h:
        in_specs += [per_b, per_b]
        args += [mod[0], mod[1]]
        out_specs = (row, row)
        out_shape = (out_shape, jax.ShapeDtypeStruct((n, d), BF16))
    return pl.pallas_call(
        functools.partial(_moe_combine_kernel, tm=tm, alpha=alpha, emit_h=emit_h),
        grid=(nt,),
        in_specs=in_specs,
        out_specs=out_specs,
        out_shape=out_shape,
        scratch_shapes=[pltpu.VMEM((2, TOP_K, tm, d), F32), pltpu.SemaphoreType.DMA((2,))],
        compiler_params=_cparams(1),
        name="moe_combine_ln",
    )(*args)


def _route(top_e, e, tm):
    n = top_e.shape[0]
    e_flat = top_e.reshape(-1).astype(jnp.int32)
    onehot = (e_flat[:, None] == jnp.arange(e, dtype=jnp.int32)[None, :]).astype(jnp.int32)
    cum = jnp.cumsum(onehot, axis=0)
    rank = jnp.take_along_axis(cum, e_flat[:, None], axis=1)[:, 0] - 1
    counts = cum[-1]
    padded = (counts + tm - 1) // tm * tm
    pend = jnp.cumsum(padded)
    pstart = pend - padded
    dest = (pstart[e_flat] + rank).astype(jnp.int32)
    n_tiles = (n * TOP_K) // tm + e
    tok_flat = jnp.repeat(jnp.arange(n, dtype=jnp.int32), TOP_K)
    row_tok = jnp.zeros((n_tiles * tm,), jnp.int32).at[dest].set(tok_flat)
    n_used = (pend[-1] // tm).astype(jnp.int32)
    tidx = jnp.arange(n_tiles, dtype=jnp.int32)
    tile_e = jnp.minimum(jnp.sum((tidx[:, None] * tm >= pend[None, :]).astype(jnp.int32), axis=1), e - 1)
    tile_rows = jnp.clip(counts[tile_e] - (tidx * tm - pstart[tile_e]), 0, tm).astype(jnp.int32)
    tile_e = jnp.where(tidx < n_used, tile_e, tile_e[jnp.maximum(n_used - 1, 0)])
    return dest.reshape(n, TOP_K), row_tok.reshape(n_tiles, 1, tm), tile_e, tile_rows, n_used.reshape(1)


def _moe_layer(x, hf, g2, gamma, beta, w_router, w1, w3, w2, layer_idx, *, alpha, rows_per_batch, mod_next=None,
               tm=1024):
    n = hf.shape[0]
    tm = _pick(n * TOP_K, tm)
    top_e, gates = _router_topk(hf, w_router)
    pos, row_tok, tile_e, tile_rows, n_used = _route(top_e, w_router.shape[1], tm)
    xs = _moe_gather(hf, row_tok, n_used, tm=tm)
    y_rows = _grouped_ffn(xs, w1, w3, w2, tile_e + layer_idx * N_EXPERTS, tile_rows, tm=tm)
    return _moe_combine_ln(y_rows, pos, gates, x, g2, gamma, beta, alpha=alpha, rows_per_batch=rows_per_batch,
                           mod=mod_next)


def kernel(x, c, ada_w, ada_b, ln1_g, ln1_b, ln2_g, ln2_b, a_w_in, a_b_in, a_w_dw, a_b_dw, a_ln_g, a_ln_b, a_w_out, a_b_out, b_w_in, b_b_in, b_a_re, b_a_im, b_log_dt, b_b_re, b_b_im, b_c_re, b_c_im, b_d, b_w_glu, b_b_glu, c_w_in, c_b_in, c_w_conv, c_b_conv, c_mh_g, c_w_out, c_b_out, d_w_in, d_b_in, d_ln_g, d_ln_b, d_w_sp, d_b_sp, d_w_out, d_b_out, f_w1, f_w3, f_w2, m_router, m_w1, m_w3, m_w2):
    bsz, seq, d = x.shape
    depth = ada_w.shape[0]
    alpha = (2.0 * depth) ** 0.25
    n = bsz * seq
    mm = functools.partial(_matmul, rows_per_batch=seq)

    mods = _ada_mods(c, ada_w, ada_b)
    xf = x.reshape(n, d)
    ia = ib = ic = idd = 0
    i_dense = i_moe = 0
    hm = None
    mod_of = lambda l: [mods[l, :, k * d:(k + 1) * d].reshape(bsz, 1, d) for k in range(6)]
    for layer in range(depth):
        sh1, sc1, g1, sh2, sc2, g2 = mod_of(layer)
        mod_next = None
        if layer + 1 < depth:
            sh1n, sc1n = mod_of(layer + 1)[:2]
            mod_next = (sc1n, sh1n)

        def mix_in(w, b, xf=xf, hm=hm, sc1=sc1, sh1=sh1, tm_bf16=2048, **kw):
            if hm is None:
                return mm(xf, w, b, mod=(sc1, sh1), **kw)
            return mm(hm, w, b, tm=tm_bf16, **kw)

        kind = layer % N_MIXERS
        is_moe = layer % 2 == 1
        h_dtype = F32 if is_moe else BF16
        proj_ln = functools.partial(_proj_ln, x=xf, g=g1, gamma=ln1_g[layer], beta=ln1_b[layer], mod=(sc2, sh2),
                                    alpha=alpha, rows_per_batch=seq, h_dtype=h_dtype)
        if kind == 0:
            glu = mix_in(a_w_in[ia], a_b_in[ia], act="glu")
            u = _conv_ln_swish(glu, a_w_dw[ia], a_b_dw[ia], a_ln_g[ia], a_ln_b[ia], bsz=bsz, seq=seq)
            xf, hf = proj_ln(u, a_w_out[ia], a_b_out[ia])
            ia += 1
        elif kind == 1:
            u = mix_in(b_w_in[ib], b_b_in[ib])
            tables = _s5_tables(b_a_re[ib], b_a_im[ib], b_log_dt[ib], b_b_re[ib], b_b_im[ib],
                                b_c_re[ib], b_c_im[ib], b_d[ib])
            yg = _s5_core(u, tables, bsz=bsz, seq=seq)
            y = mm(yg, b_w_glu[ib], b_b_glu[ib], act="glu")
            xf, hf = _ln_res(xf, y, g1, ln1_g[layer], ln1_b[layer], alpha=alpha, rows_per_batch=seq,
                             mod=(sc2, sh2), h_dtype=h_dtype)
            ib += 1
        elif kind == 2:
            heads = ML_HEADS
            dv = d // heads
            n_main = 2 * heads * (dv // 2) + 2 * heads * dv
            w_in = jnp.pad(c_w_in[ic], ((0, 0), (0, V7X_LANES - 2 * heads)))
            b_in = jnp.pad(c_b_in[ic], ((0, V7X_LANES - 2 * heads),))
            assert w_in.shape[1] == n_main + V7X_LANES
            z = mix_in(w_in, b_in, tn=896, tm_bf16=1024)
            hs = _mlstm_core(z, c_w_conv[ic], c_b_conv[ic], c_mh_g[ic], bsz=bsz, seq=seq, d=d)
            xf, hf = proj_ln(hs, c_w_out[ic], c_b_out[ic])
            ic += 1
        else:
            zz = mix_in(d_w_in[idd], d_b_in[idd], act="gelu")
            gated = _gmlp_core(zz, d_ln_g[idd], d_ln_b[idd], d_w_sp[idd], d_b_sp[idd], seq=seq)
            xf, hf = proj_ln(gated, d_w_out[idd], d_b_out[idd])
            idd += 1
        if not is_moe:
            tm_ffn = _pick(n, 1024)
            tile_g = jnp.full((n // tm_ffn,), i_dense, jnp.int32)
            tile_rows = jnp.full((n // tm_ffn,), tm_ffn, jnp.int32)
            y = _grouped_ffn(hf, f_w1, f_w3, f_w2, tile_g, tile_rows, tm=tm_ffn)
            res = _ln_res(xf, y, g2, ln2_g[layer], ln2_b[layer], alpha=alpha, rows_per_batch=seq, mod=mod_next)
            i_dense += 1
        else:
            merge = lambda w: w.reshape((w.shape[0] * w.shape[1],) + w.shape[2:])
            res = _moe_layer(xf, hf, g2, ln2_g[layer], ln2_b[layer], m_router[i_moe], merge(m_w1),
                             merge(m_w3), merge(m_w2), i_moe, alpha=alpha, rows_per_batch=seq, mod_next=mod_next)
            i_moe += 1
        xf, hm = res if mod_next is not None else (res, None)
    return xf.reshape(bsz, seq, d)
```
